```python
import math
import jax, jax.numpy as jnp
from jax import lax
import numpy as np

D_MODEL = 1024
BATCH = 8
SEQ = 2048
DEPTH = 2

GRID_W = 64
CTX_LEN = 256
HEAD_DIM = 64
A_HEADS = 4
A_VDIM = 2 * HEAD_DIM
B_HEADS = 8
B_KV_HEADS = 2
C_HEADS = 8
D_GROUPS = 4
D_GROUP_DIM = 128
POOL_WINDOWS = (2, 4, 8, 16)
NA_ROWS = 8
NA_COLS = 16
N_EXPERTS = 16
EXPERT_FF = 2048
EC_FACTOR = 2
Q_BLOCK = 128
ROPE_THETA = 10000.0
ROPE_AXIS_DIM = HEAD_DIM // 2
LN_EPS = 1e-5
RMS_EPS = 1e-6
DEEPNORM_ALPHA = (2 * DEPTH) ** 0.25
DEEPNORM_BETA = (8 * DEPTH) ** -0.25

A_Q = A_HEADS * 2 * HEAD_DIM
A_K = A_HEADS * 2 * HEAD_DIM
A_V = A_HEADS * A_VDIM
B_Q = B_HEADS * HEAD_DIM
B_K = B_KV_HEADS * HEAD_DIM
B_V = B_KV_HEADS * HEAD_DIM
EVEN_IN = A_Q + A_K + A_V + B_Q + B_K + B_V
EVEN_SPLITS = (A_Q, A_Q + A_K, A_Q + A_K + A_V, A_Q + A_K + A_V + B_Q, A_Q + A_K + A_V + B_Q + B_K)
EVEN_MIX = A_V + B_Q
C_W = C_HEADS * HEAD_DIM
D_W = D_GROUPS * D_GROUP_DIM
ODD_IN = 3 * C_W + D_W
ODD_MIX = C_W + D_W

kernel_name = "hybrid_diffattn_gqa_natten_pool_ecmoe_dit"


def layer_norm(x, g, b):
    xf = x.astype(jnp.float32)
    mu = jnp.mean(xf, axis=-1, keepdims=True)
    var = jnp.mean(jnp.square(xf - mu), axis=-1, keepdims=True)
    return ((xf - mu) * lax.rsqrt(var + LN_EPS) * g + b).astype(x.dtype)


def rms_norm(x, g):
    xf = x.astype(jnp.float32)
    return (xf * lax.rsqrt(jnp.mean(jnp.square(xf), axis=-1, keepdims=True) + RMS_EPS) * g).astype(x.dtype)


def softmax_f32(s, dtype):
    return jax.nn.softmax(s.astype(jnp.float32), axis=-1).astype(dtype)


def rope_angles(n_tokens):
    t = jnp.arange(n_tokens, dtype=jnp.int32)
    n_freq = ROPE_AXIS_DIM // 2
    inv = ROPE_THETA ** (-jnp.arange(n_freq, dtype=jnp.float32) / n_freq)
    ang_r = (t // GRID_W).astype(jnp.float32)[:, None] * inv
    ang_c = (t % GRID_W).astype(jnp.float32)[:, None] * inv
    return (jnp.cos(ang_r), jnp.sin(ang_r), jnp.cos(ang_c), jnp.sin(ang_c))


def _rotate(x, cos, sin):
    x1, x2 = jnp.split(x, 2, axis=-1)
    return jnp.concatenate([x1 * cos - x2 * sin, x2 * cos + x1 * sin], axis=-1)


def rope2d(x, rope):
    cos_r, sin_r, cos_c, sin_c = (a[None, :, None, :].astype(x.dtype) for a in rope)
    xr, xc = jnp.split(x, 2, axis=-1)
    return jnp.concatenate([_rotate(xr, cos_r, sin_r), _rotate(xc, cos_c, sin_c)], axis=-1)


def sweep_query_blocks(fn, *qs):
    b, s = qs[0].shape[:2]
    nb = s // Q_BLOCK
    blocks = tuple(jnp.moveaxis(q.reshape(b, nb, Q_BLOCK, *q.shape[2:]), 1, 0) for q in qs)
    out = lax.map(lambda qb: fn(*qb), blocks)
    return jnp.moveaxis(out, 0, 1).reshape(b, s, *out.shape[3:])


def diff_attention(q1, q2, k1, k2, v, lam):
    scale = HEAD_DIM ** -0.5
    s1 = jnp.einsum('bqhd,bkhd->bhqk', q1, k1) * scale
    s2 = jnp.einsum('bqhd,bkhd->bhqk', q2, k2) * scale
    p = jax.nn.softmax(s1.astype(jnp.float32), axis=-1) - lam * jax.nn.softmax(s2.astype(jnp.float32), axis=-1)
    return jnp.einsum('bhqk,bkhd->bqhd', p.astype(v.dtype), v)


def gqa_attention(q, k, v):
    b, nq = q.shape[:2]
    qg = q.reshape(b, nq, B_KV_HEADS, B_HEADS // B_KV_HEADS, HEAD_DIM)
    s = jnp.einsum('bqngd,bknd->bngqk', qg, k) * (HEAD_DIM ** -0.5)
    p = softmax_f32(s, v.dtype)
    o = jnp.einsum('bngqk,bknd->bqngd', p, v)
    return o.reshape(b, nq, B_HEADS, HEAD_DIM)


def even_heads(p):
    lead = p.shape[:2]
    aq, ak, av, bq, bk, bv = jnp.split(p, list(EVEN_SPLITS), axis=-1)
    aq = aq.reshape(*lead, A_HEADS, 2, HEAD_DIM)
    ak = ak.reshape(*lead, A_HEADS, 2, HEAD_DIM)
    return (aq[..., 0, :], aq[..., 1, :], ak[..., 0, :], ak[..., 1, :],
            av.reshape(*lead, A_HEADS, A_VDIM),
            bq.reshape(*lead, B_HEADS, HEAD_DIM),
            bk.reshape(*lead, B_KV_HEADS, HEAD_DIM),
            bv.reshape(*lead, B_KV_HEADS, HEAD_DIM))


def even_mixer(h, h_ctx, rope, lam_init, w_in, w_out, lam_q1, lam_k1, lam_q2, lam_k2, subln_g, qn_g, kn_g):
    b, s, _ = h.shape
    nc = h_ctx.shape[1]
    a_q1, a_q2, a_k1, a_k2, a_v, b_q, b_k, b_v = even_heads(h @ w_in)
    c_q1, c_q2, c_k1, c_k2, c_v, cb_q, cb_k, cb_v = even_heads(h_ctx @ w_in)
    lam = (jnp.exp(jnp.sum(lam_q1 * lam_k1).astype(jnp.float32))
           - jnp.exp(jnp.sum(lam_q2 * lam_k2).astype(jnp.float32)) + lam_init)
    a_q1, a_q2, a_k1, a_k2 = (rope2d(t, rope) for t in (a_q1, a_q2, a_k1, a_k2))
    b_q = rope2d(rms_norm(b_q, qn_g), rope)
    b_k = rope2d(rms_norm(b_k, kn_g), rope)
    cb_q = rms_norm(cb_q, qn_g)
    cb_k = rms_norm(cb_k, kn_g)
    cat = lambda u, w: jnp.concatenate([u, w], axis=1)
    ka1, ka2, va = cat(a_k1, c_k1), cat(a_k2, c_k2), cat(a_v, c_v)
    kb, vb = cat(b_k, cb_k), cat(b_v, cb_v)
    o_a = sweep_query_blocks(lambda q1, q2: diff_attention(q1, q2, ka1, ka2, va, lam), a_q1, a_q2)
    o_b = sweep_query_blocks(lambda q: gqa_attention(q, kb, vb), b_q)
    oc_a = diff_attention(c_q1, c_q2, c_k1, c_k2, c_v, lam)
    oc_b = gqa_attention(cb_q, cb_k, cb_v)
    sub = lambda o: rms_norm(o, subln_g) * (1.0 - lam_init)
    y = jnp.concatenate([sub(o_a).reshape(b, s, A_V), o_b.reshape(b, s, B_Q)], axis=-1) @ w_out
    y_ctx = jnp.concatenate([sub(oc_a).reshape(b, nc, A_V), oc_b.reshape(b, nc, B_Q)], axis=-1) @ w_out
    return y, y_ctx


def neighbourhood_attention(q, k, v, k_ctx, v_ctx, rpb):
    b, s, h, d = q.shape
    rows = s // GRID_W
    kh = min(NA_ROWS, rows)
    kw = NA_COLS
    grid = lambda t: t.reshape(b, rows, GRID_W, h, d)
    qg, kg, vg = grid(q), grid(k), grid(v)
    cols = jnp.arange(GRID_W, dtype=jnp.int32)
    col_start = jnp.clip(cols - kw // 2, 0, GRID_W - kw)
    col_idx = col_start[:, None] + jnp.arange(kw, dtype=jnp.int32)[None, :]
    dc = col_idx - cols[:, None] + (NA_COLS - 1)
    scale = d ** -0.5
    n_loc = kh * kw

    def row_block(r):
        rs = jnp.clip(r - kh // 2, 0, rows - kh)
        k_rows = lax.dynamic_slice_in_dim(kg, rs, kh, axis=1)
        v_rows = lax.dynamic_slice_in_dim(vg, rs, kh, axis=1)
        k_nb = jnp.moveaxis(k_rows[:, :, col_idx], 1, 2).reshape(b, GRID_W, n_loc, h, d)
        v_nb = jnp.moveaxis(v_rows[:, :, col_idx], 1, 2).reshape(b, GRID_W, n_loc, h, d)
        q_r = lax.dynamic_index_in_dim(qg, r, axis=1, keepdims=False)
        dr = rs + jnp.arange(kh, dtype=jnp.int32) - r + (NA_ROWS - 1)
        bias = rpb[:, dr[None, :, None], dc[:, None, :]].reshape(h, GRID_W, n_loc)
        s_loc = jnp.einsum('bwhd,bwnhd->bhwn', q_r, k_nb) * scale + bias[None]
        s_ctx = jnp.einsum('bwhd,bkhd->bhwk', q_r, k_ctx) * scale
        p = softmax_f32(jnp.concatenate([s_loc, s_ctx], axis=-1), v.dtype)
        return (jnp.einsum('bhwn,bwnhd->bwhd', p[..., :n_loc], v_nb)
                + jnp.einsum('bhwk,bkhd->bwhd', p[..., n_loc:], v_ctx))

    out = lax.map(row_block, jnp.arange(rows, dtype=jnp.int32))
    return jnp.moveaxis(out, 0, 1).reshape(b, s, h * d)


def multi_scale_pool(u):
    s = u.shape[1]
    uf = u.astype(jnp.float32)
    csum = jnp.concatenate([jnp.zeros_like(uf[:, :1]), jnp.cumsum(uf, axis=1)], axis=1)
    t = jnp.arange(s, dtype=jnp.int32)[:, None]
    half = jnp.array(POOL_WINDOWS, dtype=jnp.int32)[None, :] // 2
    lo = jnp.clip(t - half, 0, s)
    hi = jnp.clip(t + half, 0, s)
    g = jnp.arange(D_GROUPS, dtype=jnp.int32)[None, :]
    mean = (csum[:, hi, g] - csum[:, lo, g]) / (hi - lo).astype(jnp.float32)[None, :, :, None]
    return (mean - uf).astype(u.dtype)


def odd_mixer(h, h_ctx, w_in, w_out, rpb, pool_w, pool_scale):
    b, s, _ = h.shape
    p = h @ w_in
    cq, ck, cv, du = jnp.split(p, [C_W, 2 * C_W, 3 * C_W], axis=-1)
    ctx_k, ctx_v = jnp.split(h_ctx @ w_in[:, C_W:3 * C_W], 2, axis=-1)
    heads = lambda t: t.reshape(*t.shape[:2], C_HEADS, HEAD_DIM)
    o_c = neighbourhood_attention(heads(cq), heads(ck), heads(cv), heads(ctx_k), heads(ctx_v), rpb)
    pooled = multi_scale_pool(du.reshape(b, s, D_GROUPS, D_GROUP_DIM))
    o_d = jnp.einsum('bsgc,gce->bsge', pooled, pool_w).reshape(b, s, D_W) * pool_scale
    return jnp.concatenate([o_c, o_d], axis=-1) @ w_out


def expert_choice_ffn(h, router, w_gate, w_up, w_down):
    b, n, _ = h.shape
    cap = EC_FACTOR * n // N_EXPERTS
    aff = jax.nn.softmax((h @ router).astype(jnp.float32), axis=-1)
    gates, idx = lax.top_k(jnp.swapaxes(aff, 1, 2), cap)
    bidx = jnp.arange(b, dtype=jnp.int32)[:, None, None]
    xg = h[bidx, idx]
    hid = jax.nn.silu(jnp.einsum('becd,edf->becf', xg, w_gate)) * jnp.einsum('becd,edf->becf', xg, w_up)
    y = jnp.einsum('becf,efd->becd', hid, w_down) * gates[..., None].astype(h.dtype)
    return jnp.zeros_like(h).at[bidx, idx].add(y)


def ffn_sublayer(x, shift, scale, gate, g, b, router, w_gate, w_up, w_down):
    hm = x * (1 + scale) + shift
    return layer_norm(DEEPNORM_ALPHA * x + gate * expert_choice_ffn(hm, router, w_gate, w_up, w_down), g, b)


def setup_inputs(seed: int = 0) -> dict:
    key = jax.random.key(seed)
    ks = iter(jax.random.split(key, 32))
    nrm = lambda shape, s: jax.random.normal(next(ks), shape, jnp.float32) * s
    D = D_MODEL
    return {
        "x": nrm((BATCH, SEQ, D), 1.0),
        "c": nrm((BATCH, D), 1.0),
        "ctx": nrm((BATCH, CTX_LEN, D), 1.0),
        "c_ctx": nrm((D,), 1.0),
        "ada_w": nrm((DEPTH, D, 6 * D), D ** -0.5),
        "ada_b": nrm((DEPTH, 6 * D), 0.02),
        "ln_g": 1.0 + nrm((DEPTH, 2, D), 0.02),
        "ln_b": nrm((DEPTH, 2, D), 0.02),
        "l0_w_in": nrm((D, EVEN_IN), D ** -0.5),
        "l0_w_out": nrm((EVEN_MIX, D), EVEN_MIX ** -0.5 * DEEPNORM_BETA),
        "l0_lam_q1": nrm((HEAD_DIM,), 0.1),
        "l0_lam_k1": nrm((HEAD_DIM,), 0.1),
        "l0_lam_q2": nrm((HEAD_DIM,), 0.1),
        "l0_lam_k2": nrm((HEAD_DIM,), 0.1),
        "l0_subln_g": 1.0 + nrm((A_VDIM,), 0.02),
        "l0_qnorm_g": 1.0 + nrm((HEAD_DIM,), 0.02),
        "l0_knorm_g": 1.0 + nrm((HEAD_DIM,), 0.02),
        "l1_w_in": nrm((D, ODD_IN), D ** -0.5),
        "l1_w_out": nrm((ODD_MIX, D), ODD_MIX ** -0.5 * DEEPNORM_BETA),
        "l1_rpb": nrm((C_HEADS, 2 * NA_ROWS - 1, 2 * NA_COLS - 1), 0.1),
        "l1_pool_w": nrm((D_GROUPS, D_GROUP_DIM, D_GROUP_DIM), D_GROUP_DIM ** -0.5),
        "l1_pool_scale": 1.0 + nrm((D_W,), 0.02),
        "moe_router": nrm((DEPTH, D, N_EXPERTS), D ** -0.5),
        "moe_w_gate": nrm((DEPTH, N_EXPERTS, D, EXPERT_FF), D ** -0.5),
        "moe_w_up": nrm((DEPTH, N_EXPERTS, D, EXPERT_FF), D ** -0.5),
        "moe_w_down": nrm((DEPTH, N_EXPERTS, EXPERT_FF, D), EXPERT_FF ** -0.5 * DEEPNORM_BETA),
    }


def reference(x, c, ctx, c_ctx, ada_w, ada_b, ln_g, ln_b,
              l0_w_in, l0_w_out, l0_lam_q1, l0_lam_k1, l0_lam_q2, l0_lam_k2, l0_subln_g, l0_qnorm_g, l0_knorm_g,
              l1_w_in, l1_w_out, l1_rpb, l1_pool_w, l1_pool_scale,
              moe_router, moe_w_gate, moe_w_up, moe_w_down):
    rope = rope_angles(x.shape[1])
    for i in range(DEPTH):
        mod = jax.nn.silu(c) @ ada_w[i] + ada_b[i]
        mod_c = jax.nn.silu(c_ctx) @ ada_w[i] + ada_b[i]
        sh1, sc1, g1, sh2, sc2, g2 = (m[:, None, :] for m in jnp.split(mod, 6, axis=-1))
        csh1, csc1, cg1, csh2, csc2, cg2 = (m[None, None, :] for m in jnp.split(mod_c, 6, axis=-1))
        h = x * (1 + sc1) + sh1
        h_ctx = ctx * (1 + csc1) + csh1
        if i % 2 == 0:
            y, y_ctx = even_mixer(h, h_ctx, rope, 0.8 - 0.6 * math.exp(-0.3 * i),
                                  l0_w_in, l0_w_out, l0_lam_q1, l0_lam_k1, l0_lam_q2, l0_lam_k2,
                                  l0_subln_g, l0_qnorm_g, l0_knorm_g)
        else:
            y = odd_mixer(h, h_ctx, l1_w_in, l1_w_out, l1_rpb, l1_pool_w, l1_pool_scale)
        x = layer_norm(DEEPNORM_ALPHA * x + g1 * y, ln_g[i, 0], ln_b[i, 0])
        x = ffn_sublayer(x, sh2, sc2, g2, ln_g[i, 1], ln_b[i, 1],
                         moe_router[i], moe_w_gate[i], moe_w_up[i], moe_w_down[i])
        if i % 2 == 0:
            ctx = layer_norm(DEEPNORM_ALPHA * ctx + cg1 * y_ctx, ln_g[i, 0], ln_b[i, 0])
            ctx = ffn_sublayer(ctx, csh2, csc2, cg2, ln_g[i, 1], ln_b[i, 1],
                               moe_router[i], moe_w_gate[i], moe_w_up[i], moe_w_down[i])
    return x
```

```python
import functools
import math

import jax
import jax.numpy as jnp
from jax import lax
from jax.experimental import pallas as pl
from jax.experimental.pallas import tpu as pltpu

F32 = jnp.float32
BF16 = jnp.bfloat16

DEPTH = 2
GRID_W = 64
HEAD_DIM = 64
A_HEADS = 4
B_HEADS = 8
B_KV_HEADS = 2
C_HEADS = 8
D_GROUPS = 4
D_GROUP_DIM = 128
POOL_WINDOWS = (2, 4, 8, 16)
NA_ROWS = 8
NA_COLS = 16
N_EXPERTS = 16
EC_FACTOR = 2
ROPE_THETA = 10000.0
LN_EPS = 1e-5
RMS_EPS = 1e-6
DEEPNORM_ALPHA = (2 * DEPTH) ** 0.25
Q_SCALE = HEAD_DIM ** -0.5

LANES = 128
VMEM_LIMIT = 56 * 1024 * 1024

NA_QROWS = 4
NEG_BIG = -1e30

_NN = (((1,), (0,)), ((), ()))
_NT = (((1,), (1,)), ((), ()))


def _dot(a, b, dims=_NN):
    return lax.dot_general(a, b, dims, preferred_element_type=F32)


def _split_bf16(a):
    hi = a.astype(BF16)
    lo = (a - hi.astype(F32)).astype(BF16)
    return hi, lo


def _dot3(a, b, dims=_NN):
    a_hi, a_lo = _split_bf16(a)
    b_hi, b_lo = _split_bf16(b)
    return _dot(a_hi, b_hi, dims) + (_dot(a_hi, b_lo, dims) + _dot(a_lo, b_hi, dims))


def _params(sem):
    return pltpu.CompilerParams(dimension_semantics=sem, vmem_limit_bytes=VMEM_LIMIT)


def _layer_norm(z, g, b):
    mu = jnp.mean(z, axis=-1, keepdims=True)
    zc = z - mu
    var = jnp.mean(zc * zc, axis=-1, keepdims=True)
    return zc * lax.rsqrt(var + LN_EPS) * g + b


def _lane_masks():
    lane = lax.broadcasted_iota(jnp.int32, (1, LANES), 1)
    lo = jnp.where(lane < HEAD_DIM, 1.0, 0.0).astype(F32)
    return lo, 1.0 - lo


def _ada_kernel(c_ref, w_ref, b_ref, o_ref):
    c = c_ref[...]
    s = c * (1.0 / (1.0 + jnp.exp(-c)))
    o_ref[0] = _dot3(s, w_ref[0]) + b_ref[0]


def _ada_call(cc, ada_w, ada_b):
    depth, d, n = ada_w.shape
    rows = cc.shape[0]
    tn = 1536 if n % 1536 == 0 else n
    return pl.pallas_call(
        _ada_kernel,
        out_shape=jax.ShapeDtypeStruct((depth, rows, n), F32),
        grid=(depth, n // tn),
        in_specs=[pl.BlockSpec((rows, d), lambda l, j: (0, 0)),
                  pl.BlockSpec((1, d, tn), lambda l, j: (l, 0, j)),
                  pl.BlockSpec((1, 1, tn), lambda l, j: (l, 0, j))],
        out_specs=pl.BlockSpec((1, rows, tn), lambda l, j: (l, 0, j)),
        compiler_params=_params(("arbitrary", "arbitrary")),
        name="ada_mod",
    )(cc, ada_w, ada_b.reshape(depth, 1, n))


def _inproj_kernel(x_ref, sc_ref, sh_ref, w_ref, cos_ref, sinp_ref, sinm_ref, gn_ref, gmat_ref, *o_refs,
                   blocks, chunk, use_rope):
    h = (x_ref[0] * (1.0 + sc_ref[0]) + sh_ref[0]).astype(BF16)
    n = len(blocks) * LANES
    for c0 in range(0, n, chunk):
        cw = min(chunk, n - c0)
        acc = _dot(h, w_ref[:, c0:c0 + cw])
        for j in range(cw // LANES):
            kind, gain_row, factor, oi, oc = blocks[(c0 // LANES) + j]
            v = acc[:, j * LANES:(j + 1) * LANES]
            if kind == "norm":
                v2 = v * v
                hi, lo = _split_bf16(v2)
                ms = _dot(hi, gmat_ref[...]) + _dot(lo, gmat_ref[...])
                v = v * lax.rsqrt(ms + RMS_EPS) * gn_ref[gain_row:gain_row + 1, :]
            if kind in ("rope", "norm") and use_rope:
                v = (v * cos_ref[...] + pltpu.roll(v, 16, 1) * sinp_ref[...]
                     + pltpu.roll(v, LANES - 16, 1) * sinm_ref[...])
            if factor != 1.0:
                v = v * factor
            o_refs[oi][0, :, oc:oc + LANES] = v.astype(o_refs[oi].dtype)


def _inproj_call(x, sc, sh, w, blocks, out_defs, rope_tabs, gains, gmat, *, use_rope, name):
    b, s, d = x.shape
    n = w.shape[1]
    tm = min(256, s)
    per_b = sc.shape[0] == b
    mod_map = (lambda bi, i: (bi, 0, 0)) if per_b else (lambda bi, i: (0, 0, 0))
    cos, sinp, sinm = rope_tabs
    tab_spec = pl.BlockSpec((tm, LANES), lambda bi, i: (i, 0))
    kern = functools.partial(_inproj_kernel, blocks=tuple(blocks), chunk=512, use_rope=use_rope)
    return pl.pallas_call(
        kern,
        out_shape=[jax.ShapeDtypeStruct((b, s, nc), dt) for nc, dt in out_defs],
        grid=(b, s // tm),
        in_specs=[pl.BlockSpec((1, tm, d), lambda bi, i: (bi, i, 0)),
                  pl.BlockSpec((1, 1, d), mod_map),
                  pl.BlockSpec((1, 1, d), mod_map),
                  pl.BlockSpec((d, n), lambda bi, i: (0, 0)),
                  tab_spec, tab_spec, tab_spec,
                  pl.BlockSpec(gains.shape, lambda bi, i: (0, 0)),
                  pl.BlockSpec(gmat.shape, lambda bi, i: (0, 0))],
        out_specs=[pl.BlockSpec((1, tm, nc), lambda bi, i: (bi, i, 0)) for nc, _ in out_defs],
        compiler_params=_params(("arbitrary", "arbitrary")),
        name=name,
    )(x, sc, sh, w, cos, sinp, sinm, gains, gmat)


def _rope_tables(s):
    n_freq = HEAD_DIM // 4
    t = jnp.arange(s, dtype=jnp.int32)
    inv = ROPE_THETA ** (-jnp.arange(n_freq, dtype=F32) / n_freq)
    ang_r = (t // GRID_W).astype(F32)[:, None] * inv
    ang_c = (t % GRID_W).astype(F32)[:, None] * inv
    ang = jnp.concatenate([ang_r, ang_r, ang_c, ang_c] * (LANES // HEAD_DIM), axis=-1)
    first = (jnp.arange(LANES) % 32) < 16
    cos, sin = jnp.cos(ang), jnp.sin(ang)
    return cos, jnp.where(first, 0.0, sin), jnp.where(first, -sin, 0.0)


def _softmax_parts(s_list):
    m = s_list[0].max(axis=-1, keepdims=True)
    for s in s_list[1:]:
        m = jnp.maximum(m, s.max(axis=-1, keepdims=True))
    es = [jnp.exp(s - m) for s in s_list]
    den = es[0].sum(axis=-1, keepdims=True)
    for e in es[1:]:
        den = den + e.sum(axis=-1, keepdims=True)
    return es, 1.0 / den


def _attn_even_kernel(q_ref, lam_ref, sg_ref, *refs, n_kv, tq, lam_init):
    kv_refs, o_ref = refs[:n_kv], refs[n_kv]
    lo, hi = _lane_masks()
    lo_b, hi_b = lo.astype(BF16), hi.astype(BF16)
    lv = lam_ref[...]
    lam = (jnp.exp(jnp.sum(lv[0:1] * lv[1:2], axis=-1, keepdims=True))
           - jnp.exp(jnp.sum(lv[2:3] * lv[3:4], axis=-1, keepdims=True)) + lam_init)

    def scores(qq, col):
        return [_dot(qq, kv[0, :, col:col + LANES], _NT) for kv in kv_refs]

    def pv(ps, col):
        out = _dot(ps[0], kv_refs[0][0, :, col:col + LANES])
        for p, kv in zip(ps[1:], kv_refs[1:]):
            out = out + _dot(p, kv[0, :, col:col + LANES])
        return out

    for h in range(A_HEADS):
        q = q_ref[0, :, h * LANES:(h + 1) * LANES]
        qq = jnp.concatenate([q * lo_b, q * hi_b], axis=0)
        es, inv = _softmax_parts(scores(qq, 512 + h * LANES))
        coef = jnp.concatenate([inv[:tq], -lam * inv[tq:]], axis=0)
        ps = []
        for e in es:
            pe = e * coef
            ps.append((pe[:tq] + pe[tq:]).astype(BF16))
        o = pv(ps, 1024 + h * LANES)
        ms = jnp.mean(o * o, axis=-1, keepdims=True)
        o = o * lax.rsqrt(ms + RMS_EPS) * sg_ref[...] * (1.0 - lam_init)
        o_ref[0, :, h * LANES:(h + 1) * LANES] = o.astype(o_ref.dtype)

    for j in range(B_HEADS // 2):
        q = q_ref[0, :, 1536 + j * LANES:1536 + (j + 1) * LANES]
        qq = jnp.concatenate([q * lo_b, q * hi_b], axis=0)
        es, inv = _softmax_parts(scores(qq, 2048))
        o2 = pv([(e * inv).astype(BF16) for e in es], 2176)
        o = o2[:tq] * lo + o2[tq:] * hi
        o_ref[0, :, 512 + j * LANES:512 + (j + 1) * LANES] = o.astype(o_ref.dtype)


def _attn_even_call(q_arr, kv_arrs, lamv, subln_g, lam_init, name):
    b, sq, n = q_arr.shape
    tq = min(256, sq)
    kern = functools.partial(_attn_even_kernel, n_kv=len(kv_arrs), tq=tq, lam_init=lam_init)
    return pl.pallas_call(
        kern,
        out_shape=jax.ShapeDtypeStruct((b, sq, 1024), BF16),
        grid=(b, sq // tq),
        in_specs=[pl.BlockSpec((1, tq, n), lambda bi, i: (bi, i, 0)),
                  pl.BlockSpec(lamv.shape, lambda bi, i: (0, 0)),
                  pl.BlockSpec(subln_g.shape, lambda bi, i: (0, 0))]
                 + [pl.BlockSpec((1,) + a.shape[1:], lambda bi, i: (bi, 0, 0)) for a in kv_arrs],
        out_specs=pl.BlockSpec((1, tq, 1024), lambda bi, i: (bi, i, 0)),
        compiler_params=_params(("arbitrary", "arbitrary")),
        name=name,
    )(q_arr, lamv, subln_g, *kv_arrs)


def _odd_mixer_kernel(qkv_ref, ckv_ref, bias_ref, du_ref, band_ref, pw_ref, ps_ref, o_ref, *,
                      tq, slab_rows, n_rows, n_blk, seq):
    i = pl.program_id(1)
    lo, hi = _lane_masks()
    lo_b, hi_b = lo.astype(BF16), hi.astype(BF16)
    t0 = pl.multiple_of(i * tq, tq)
    base = jnp.clip(i * NA_QROWS - NA_ROWS // 2, 0, n_rows - slab_rows)
    k0 = pl.multiple_of(base * GRID_W, GRID_W)
    nk = slab_rows * GRID_W
    cw = C_HEADS * HEAD_DIM

    for j in range(C_HEADS // 2):
        cs = slice(j * LANES, (j + 1) * LANES)
        q = qkv_ref[0, pl.ds(t0, tq), cs]
        qq = jnp.concatenate([q * lo_b, q * hi_b], axis=0)
        kl = qkv_ref[0, pl.ds(k0, nk), cw + j * LANES:cw + (j + 1) * LANES]
        vl = qkv_ref[0, pl.ds(k0, nk), 2 * cw + j * LANES:2 * cw + (j + 1) * LANES]
        kc = ckv_ref[0, :, cs]
        vc = ckv_ref[0, :, cw + j * LANES:cw + (j + 1) * LANES]
        bias = jnp.concatenate([bias_ref[0, 2 * j], bias_ref[0, 2 * j + 1]], axis=0)
        s_l = _dot(qq, kl, _NT) + bias
        s_c = _dot(qq, kc, _NT)
        (e_l, e_c), inv = _softmax_parts([s_l, s_c])
        o2 = _dot((e_l * inv).astype(BF16), vl) + _dot((e_c * inv).astype(BF16), vc)
        o = o2[:tq] * lo + o2[tq:] * hi
        o_ref[0, :, cs] = o.astype(o_ref.dtype)

    tprev = pl.multiple_of(jnp.maximum(i - 1, 0) * tq, tq)
    tnext = pl.multiple_of(jnp.minimum(i + 1, n_blk - 1) * tq, tq)
    has_prev = jnp.where(i > 0, 1.0, 0.0).astype(F32)
    has_next = jnp.where(i < n_blk - 1, 1.0, 0.0).astype(F32)
    tpos = t0 + lax.broadcasted_iota(jnp.int32, (tq, 1), 0)
    for g in range(D_GROUPS):
        half = POOL_WINDOWS[g] // 2
        gs = slice(g * D_GROUP_DIM, (g + 1) * D_GROUP_DIM)
        cur = du_ref[0, pl.ds(t0, tq), gs]
        prv = du_ref[0, pl.ds(tprev, tq), gs]
        nxt = du_ref[0, pl.ds(tnext, tq), gs]

        def band_sum(m, u):
            u_hi, u_lo = _split_bf16(u)
            return _dot(band_ref[g, m], u_hi) + _dot(band_ref[g, m], u_lo)

        wsum = band_sum(1, cur) + has_prev * band_sum(0, prv) + has_next * band_sum(2, nxt)
        cnt = (jnp.minimum(tpos + half, seq) - jnp.maximum(tpos - half, 0)).astype(F32)
        pooled = wsum * (1.0 / cnt) - cur
        od = _dot(pooled.astype(BF16), pw_ref[g]) * ps_ref[:, gs]
        o_ref[0, :, cw + g * D_GROUP_DIM:cw + (g + 1) * D_GROUP_DIM] = od.astype(o_ref.dtype)


def _na_geometry(n_rows):
    kh = min(NA_ROWS, n_rows)
    slab = min(n_rows, NA_QROWS + kh)
    n_blk = n_rows // NA_QROWS
    bases = [min(max(i * NA_QROWS - NA_ROWS // 2, 0), n_rows - slab) for i in range(n_blk)]
    sigs, type_of = [], []
    for i in range(n_blk):
        sig = tuple((min(max(i * NA_QROWS + r - kh // 2, 0), n_rows - kh) - bases[i],
                     i * NA_QROWS + r - bases[i]) for r in range(NA_QROWS))
        if sig not in sigs:
            sigs.append(sig)
        type_of.append(sigs.index(sig))
    return kh, slab, n_blk, sigs, type_of


def _na_bias_table(rpb, n_rows):
    kh, slab, _, sigs, _ = _na_geometry(n_rows)
    kw = NA_COLS
    w = jnp.arange(GRID_W, dtype=jnp.int32)
    cstart = jnp.clip(w - kw // 2, 0, GRID_W - kw)
    jcol = jnp.arange(GRID_W, dtype=jnp.int32)
    col_ok = (jcol[None, :] >= cstart[:, None]) & (jcol[None, :] < cstart[:, None] + kw)
    dc = jnp.clip(jcol[None, :] - w[:, None] + (NA_COLS - 1), 0, 2 * NA_COLS - 2)
    tabs = []
    for sig in sigs:
        q_rows = []
        for rs_rel, qr_rel in sig:
            k_blocks = []
            for m in range(slab):
                row_ok = rs_rel <= m < rs_rel + kh
                dr = min(max(m - qr_rel + (NA_ROWS - 1), 0), 2 * NA_ROWS - 2)
                blk = jnp.where(col_ok[None] & row_ok, rpb[:, dr][:, dc], NEG_BIG)
                k_blocks.append(blk)
            q_rows.append(jnp.concatenate(k_blocks, axis=-1))
        tabs.append(jnp.concatenate(q_rows, axis=1))
    return jnp.stack(tabs, axis=0).astype(F32)


def _pool_bands(tq):
    t = jnp.arange(tq, dtype=jnp.int32)[:, None]
    sidx = jnp.arange(tq, dtype=jnp.int32)[None, :]
    out = []
    for wdw in POOL_WINDOWS:
        half = wdw // 2
        per = []
        for m in range(3):
            srel = sidx + (m - 1) * tq
            per.append(((srel >= t - half) & (srel < t + half)).astype(BF16))
        out.append(jnp.stack(per, axis=0))
    return jnp.stack(out, axis=0)


def _odd_mixer_call(qkv, ckv, rpb, du, pool_w, pool_scale):
    b, s, n = qkv.shape
    n_rows = s // GRID_W
    _, slab, n_blk, _, type_of = _na_geometry(n_rows)
    tq = NA_QROWS * GRID_W
    bias = _na_bias_table(rpb, n_rows)
    bands = _pool_bands(tq)

    def bias_map(bi, i):
        t = jnp.int32(type_of[-1])
        for blk in range(n_blk - 2, -1, -1):
            t = jnp.where(i == blk, jnp.int32(type_of[blk]), t)
        return (t, 0, 0, 0)

    kern = functools.partial(_odd_mixer_kernel, tq=tq, slab_rows=slab, n_rows=n_rows, n_blk=n_blk, seq=s)
    return pl.pallas_call(
        kern,
        out_shape=jax.ShapeDtypeStruct((b, s, 1024), BF16),
        grid=(b, n_blk),
        in_specs=[pl.BlockSpec((1, s, n), lambda bi, i: (bi, 0, 0)),
                  pl.BlockSpec((1,) + ckv.shape[1:], lambda bi, i: (bi, 0, 0)),
                  pl.BlockSpec((1,) + bias.shape[1:], bias_map),
                  pl.BlockSpec((1, s, du.shape[2]), lambda bi, i: (bi, 0, 0)),
                  pl.BlockSpec(bands.shape, lambda bi, i: (0, 0, 0, 0)),
                  pl.BlockSpec(pool_w.shape, lambda bi, i: (0, 0, 0)),
                  pl.BlockSpec(pool_scale.shape, lambda bi, i: (0, 0))],
        out_specs=pl.BlockSpec((1, tq, 1024), lambda bi, i: (bi, i, 0)),
        compiler_params=_params(("arbitrary", "arbitrary")),
        name="odd_mixer",
    )(qkv, ckv, bias, du, bands, pool_w, pool_scale)


def _outproj_kernel(mix_ref, w_ref, x_ref, g1_ref, lng_ref, lnb_ref, sc_ref, sh_ref, rt_ref,
                    x1_ref, hm_ref, aff_ref):
    y = _dot(mix_ref[0], w_ref[...])
    x1 = _layer_norm(DEEPNORM_ALPHA * x_ref[0] + g1_ref[0] * y, lng_ref[...], lnb_ref[...])
    x1_ref[0] = x1
    hm = x1 * (1.0 + sc_ref[0]) + sh_ref[0]
    hm_ref[0] = hm.astype(hm_ref.dtype)
    logits = _dot3(rt_ref[...], hm, _NT)
    ex = jnp.exp(logits - logits.max(axis=0, keepdims=True))
    aff_ref[0] = ex * (1.0 / ex.sum(axis=0, keepdims=True))


def _outproj_call(mix, w, x, g1, lng, lnb, sc2, sh2, router_t, name):
    b, s, d = x.shape
    dm = mix.shape[2]
    e = router_t.shape[0]
    tm = min(256, s)
    per_b = g1.shape[0] == b
    mod_map = (lambda bi, i: (bi, 0, 0)) if per_b else (lambda bi, i: (0, 0, 0))
    vec = pl.BlockSpec((1, 1, d), mod_map)
    row = pl.BlockSpec((1, d), lambda bi, i: (0, 0))
    return pl.pallas_call(
        _outproj_kernel,
        out_shape=[jax.ShapeDtypeStruct((b, s, d), F32),
                   jax.ShapeDtypeStruct((b, s, d), BF16),
                   jax.ShapeDtypeStruct((b, e, s), F32)],
        grid=(b, s // tm),
        in_specs=[pl.BlockSpec((1, tm, dm), lambda bi, i: (bi, i, 0)),
                  pl.BlockSpec((dm, d), lambda bi, i: (0, 0)),
                  pl.BlockSpec((1, tm, d), lambda bi, i: (bi, i, 0)),
                  vec, row, row, vec, vec,
                  pl.BlockSpec((e, d), lambda bi, i: (0, 0))],
        out_specs=[pl.BlockSpec((1, tm, d), lambda bi, i: (bi, i, 0)),
                   pl.BlockSpec((1, tm, d), lambda bi, i: (bi, i, 0)),
                   pl.BlockSpec((1, e, tm), lambda bi, i: (bi, 0, i))],
        compiler_params=_params(("arbitrary", "arbitrary")),
        name=name,
    )(mix, w, x, g1, lng, lnb, sc2, sh2, router_t)


def _lane_cumsum(m):
    rows, s = m.shape
    r_i = lax.broadcasted_iota(jnp.int32, (LANES, LANES), 0)
    c_i = lax.broadcasted_iota(jnp.int32, (LANES, LANES), 1)
    tri = jnp.where(r_i <= c_i, 1.0, 0.0).astype(BF16)
    carry = jnp.zeros((rows, 1), F32)
    out = []
    for c in range(s // LANES):
        blk = m[:, c * LANES:(c + 1) * LANES]
        out.append(_dot(blk.astype(BF16), tri) + carry)
        carry = carry + blk.sum(axis=-1, keepdims=True)
    return jnp.concatenate(out, axis=-1)


def _route_kernel(aff_ref, pos_ref, *, cap):
    a = aff_ref[0]
    bits = pltpu.bitcast(a, jnp.int32)
    thr = jnp.zeros((a.shape[0], 1), jnp.int32)
    for bit in range(30, -1, -1):
        cand = thr | jnp.int32(1 << bit)
        cnt = jnp.where(bits >= cand, 1.0, 0.0).sum(axis=-1, keepdims=True)
        thr = jnp.where(cnt >= cap, cand, thr)
    gt = jnp.where(bits > thr, 1.0, 0.0)
    eq = jnp.where(bits == thr, 1.0, 0.0)
    need = cap - gt.sum(axis=-1, keepdims=True)
    sel = gt + eq * jnp.where(_lane_cumsum(eq) <= need, 1.0, 0.0)
    pos_ref[0] = jnp.where(sel > 0.5, _lane_cumsum(sel) - 1.0, -1.0)


def _route_call(aff, cap, name):
    b, e, s = aff.shape
    return pl.pallas_call(
        functools.partial(_route_kernel, cap=cap),
        out_shape=jax.ShapeDtypeStruct((b, e, s), F32),
        grid=(b,),
        in_specs=[pl.BlockSpec((1, e, s), lambda bi: (bi, 0, 0))],
        out_specs=pl.BlockSpec((1, e, s), lambda bi: (bi, 0, 0)),
        compiler_params=_params(("arbitrary",)),
        name=name,
    )(aff)


def _gather_kernel(pos_ref, aff_ref, hm_ref, xg_ref, gs_ref, *, cap):
    prow = pos_ref[0, 0]
    slot = lax.broadcasted_iota(jnp.int32, (cap, 1), 0).astype(F32)
    hit = prow == slot
    xg_ref[0, 0] = _dot(jnp.where(hit, 1.0, 0.0).astype(BF16), hm_ref[0]).astype(xg_ref.dtype)
    gs_ref[0, 0] = jnp.where(hit, aff_ref[0, 0], 0.0).sum(axis=-1, keepdims=True)


def _gather_call(pos, aff, hm, cap, name):
    b, e, s = pos.shape
    d = hm.shape[2]
    row = pl.BlockSpec((1, 1, 1, s), lambda bi, ei: (bi, ei, 0, 0))
    return pl.pallas_call(
        functools.partial(_gather_kernel, cap=cap),
        out_shape=[jax.ShapeDtypeStruct((b, e, cap, d), BF16),
                   jax.ShapeDtypeStruct((b, e, cap, 1), F32)],
        grid=(b, e),
        in_specs=[row, row, pl.BlockSpec((1, s, d), lambda bi, ei: (bi, 0, 0))],
        out_specs=[pl.BlockSpec((1, 1, cap, d), lambda bi, ei: (bi, ei, 0, 0)),
                   pl.BlockSpec((1, 1, cap, 1), lambda bi, ei: (bi, ei, 0, 0))],
        compiler_params=_params(("arbitrary", "arbitrary")),
        name=name,
    )(pos.reshape(b, e, 1, s), aff.reshape(b, e, 1, s), hm)


def _ffn_kernel(*refs, n_grp, row_chunks):
    xg_refs = refs[:n_grp]
    gs_refs = refs[n_grp:2 * n_grp]
    wg_ref, wu_ref, wd_ref = refs[2 * n_grp:2 * n_grp + 3]
    y_refs = refs[2 * n_grp + 3:3 * n_grp + 3]
    acc_refs = refs[3 * n_grp + 3:]
    f = pl.program_id(1)
    last = pl.num_programs(1) - 1
    wg = wg_ref[0].astype(BF16)
    wu = wu_ref[0].astype(BF16)
    wd = wd_ref[0].astype(BF16)
    for xg_ref, gs_ref, y_ref, acc_ref, nb in zip(xg_refs, gs_refs, y_refs, acc_refs, row_chunks):
        bt, _, cap, d = xg_ref.shape
        for b0 in range(0, bt, nb):
            rows = nb * cap
            r0 = b0 * cap
            x = xg_ref[b0:b0 + nb, 0].reshape(rows, d)
            hg = _dot(x, wg)
            hu = _dot(x, wu)
            hid = (hg * (1.0 / (1.0 + jnp.exp(-hg))) * hu).astype(BF16)
            part = _dot(hid, wd)

            @pl.when(f == 0)
            def _():
                acc_ref[r0:r0 + rows, :] = part

            @pl.when(f > 0)
            def _():
                acc_ref[r0:r0 + rows, :] += part

            @pl.when(f == last)
            def _():
                gate = gs_ref[b0:b0 + nb, 0].reshape(rows, 1)
                y_ref[b0:b0 + nb, 0] = (acc_ref[r0:r0 + rows, :] * gate).reshape(nb, cap, d).astype(y_ref.dtype)


def _ffn_call(xgs, gss, w_gate, w_up, w_down, name):
    e, d, ff = w_gate.shape
    tf = 512 if ff % 512 == 0 else ff
    n_grp = len(xgs)
    row_chunks = []
    for xg in xgs:
        bt, _, cap, _ = xg.shape
        nb = max(1, min(bt, 512 // cap))
        while bt % nb:
            nb -= 1
        row_chunks.append(nb)
    tok = lambda a: pl.BlockSpec((a.shape[0], 1) + a.shape[2:], lambda ei, fi: (0, ei, 0, 0))
    kern = functools.partial(_ffn_kernel, n_grp=n_grp, row_chunks=tuple(row_chunks))
    return pl.pallas_call(
        kern,
        out_shape=[jax.ShapeDtypeStruct(xg.shape, BF16) for xg in xgs],
        grid=(e, ff // tf),
        in_specs=[tok(a) for a in xgs] + [tok(a) for a in gss]
                 + [pl.BlockSpec((1, d, tf), lambda ei, fi: (ei, 0, fi)),
                    pl.BlockSpec((1, d, tf), lambda ei, fi: (ei, 0, fi)),
                    pl.BlockSpec((1, tf, d), lambda ei, fi: (ei, fi, 0))],
        out_specs=[tok(a) for a in xgs],
        scratch_shapes=[pltpu.VMEM((xg.shape[0] * xg.shape[2], d), F32) for xg in xgs],
        compiler_params=_params(("arbitrary", "arbitrary")),
        name=name,
    )(*xgs, *gss, w_gate, w_up, w_down)


def _combine_kernel(pos_ref, y_ref, x_ref, g_ref, lng_ref, lnb_ref, o_ref, *, cap, n_exp):
    pos = pos_ref[0]
    slot = lax.broadcasted_iota(jnp.int32, (1, cap), 1).astype(F32)
    acc = None
    for ei in range(n_exp):
        onehot = jnp.where(pos[:, ei:ei + 1] == slot, 1.0, 0.0).astype(BF16)
        part = _dot(onehot, y_ref[0, ei * cap:(ei + 1) * cap, :])
        acc = part if acc is None else acc + part
    o_ref[0] = _layer_norm(DEEPNORM_ALPHA * x_ref[0] + g_ref[0] * acc, lng_ref[...], lnb_ref[...])


def _combine_call(pos_t, y, x, g2, lng, lnb, cap, name):
    b, s, d = x.shape
    e = pos_t.shape[2]
    tm = min(256, s)
    per_b = g2.shape[0] == b
    mod_map = (lambda bi, i: (bi, 0, 0)) if per_b else (lambda bi, i: (0, 0, 0))
    row = pl.BlockSpec((1, d), lambda bi, i: (0, 0))
    return pl.pallas_call(
        functools.partial(_combine_kernel, cap=cap, n_exp=e),
        out_shape=jax.ShapeDtypeStruct((b, s, d), F32),
        grid=(b, s // tm),
        in_specs=[pl.BlockSpec((1, tm, e), lambda bi, i: (bi, i, 0)),
                  pl.BlockSpec((1, e * cap, d), lambda bi, i: (bi, 0, 0)),
                  pl.BlockSpec((1, tm, d), lambda bi, i: (bi, i, 0)),
                  pl.BlockSpec((1, 1, d), mod_map), row, row],
        out_specs=pl.BlockSpec((1, tm, d), lambda bi, i: (bi, i, 0)),
        compiler_params=_params(("arbitrary", "arbitrary")),
        name=name,
    )(pos_t, y.reshape(b, e * cap, d), x, g2, lng, lnb)


def _ffn_sublayer(streams, mods, lng, lnb, router, w_gate, w_up, w_down, layer):
    router_t = router.T
    staged = []
    for si, (mix, w_out, x, g1, sc2, sh2, g2) in enumerate(streams):
        tag = f"l{layer}s{si}"
        n_tok = x.shape[1]
        cap = EC_FACTOR * n_tok // N_EXPERTS
        x1, hm, aff = _outproj_call(mix, w_out, x, g1, lng[0:1], lnb[0:1], sc2, sh2, router_t, "outproj_" + tag)
        pos = _route_call(aff, cap, "route_" + tag)
        xg, gs = _gather_call(pos, aff, hm, cap, "gather_" + tag)
        staged.append((x1, pos, xg, gs, g2, cap, tag))
    ys = _ffn_call([st[2] for st in staged], [st[3] for st in staged], w_gate, w_up, w_down, f"ffn_l{layer}")
    outs = []
    for (x1, pos, _, _, g2, cap, tag), y in zip(staged, ys):
        outs.append(_combine_call(jnp.swapaxes(pos, 1, 2), y, x1, g2, lng[1:2], lnb[1:2], cap, "combine_" + tag))
    return outs


def kernel(x, c, ctx, c_ctx, ada_w, ada_b, ln_g, ln_b, l0_w_in, l0_w_out, l0_lam_q1, l0_lam_k1, l0_lam_q2,
           l0_lam_k2, l0_subln_g, l0_qnorm_g, l0_knorm_g, l1_w_in, l1_w_out, l1_rpb, l1_pool_w, l1_pool_scale,
           moe_router, moe_w_gate, moe_w_up, moe_w_down):
    b, s, d = x.shape
    n_ctx = ctx.shape[1]

    rows = -(-(b + 1) // 8) * 8
    cc = jnp.zeros((rows, d), F32).at[:b].set(c).at[b].set(c_ctx)
    mod = _ada_call(cc, ada_w, ada_b)

    def mods(i):
        lat = [mod[i, :b, k * d:(k + 1) * d].reshape(b, 1, d) for k in range(6)]
        cx = [mod[i, b:b + 1, k * d:(k + 1) * d].reshape(1, 1, d) for k in range(6)]
        return lat, cx

    rope = _rope_tables(s)
    rope_id = (jnp.ones((n_ctx, LANES), F32), jnp.zeros((n_ctx, LANES), F32), jnp.zeros((n_ctx, LANES), F32))
    gmat = jnp.where((jnp.arange(LANES)[:, None] // HEAD_DIM) == (jnp.arange(LANES)[None, :] // HEAD_DIM),
                     1.0 / HEAD_DIM, 0.0).astype(BF16)
    tile2 = lambda g: jnp.concatenate([g, g]).reshape(1, LANES)

    (sh1, sc1, g1, sh2, sc2, g2), (csh1, csc1, cg1, csh2, csc2, cg2) = mods(0)
    bq0 = 3 * 512
    pair_cols = jnp.concatenate([jnp.arange(HEAD_DIM) + bq0 + hh * HEAD_DIM
                                 for j in range(B_HEADS // 2) for hh in (j, j + B_HEADS // 2)])
    in_perm = jnp.concatenate([jnp.arange(bq0), pair_cols, jnp.arange(bq0 + 512, l0_w_in.shape[1])])
    w_in0 = l0_w_in[:, in_perm].astype(BF16)
    w_out0 = l0_w_out[jnp.concatenate([jnp.arange(512), pair_cols - bq0 + 512])].astype(BF16)
    gains0 = jnp.concatenate([tile2(l0_qnorm_g), tile2(l0_knorm_g)], axis=0)
    blocks0 = ([("rope", 0, Q_SCALE, 0, k * LANES) for k in range(4)]
               + [("rope", 0, 1.0, 0, (4 + k) * LANES) for k in range(4)]
               + [("plain", 0, 1.0, 0, (8 + k) * LANES) for k in range(4)]
               + [("norm", 0, Q_SCALE, 0, (12 + k) * LANES) for k in range(4)]
               + [("norm", 1, 1.0, 0, 16 * LANES), ("plain", 0, 1.0, 0, 17 * LANES)])
    n0 = len(blocks0) * LANES
    qkv = _inproj_call(x, sc1, sh1, w_in0, blocks0, [(n0, BF16)], rope, gains0, gmat,
                       use_rope=True, name="inproj_l0")[0]
    qkv_c = _inproj_call(ctx, csc1, csh1, w_in0, blocks0, [(n0, BF16)], rope_id, gains0, gmat,
                         use_rope=False, name="inproj_l0c")[0]
    lam_init = 0.8 - 0.6 * math.exp(-0.3 * 0)
    lamv = jnp.stack([l0_lam_q1, l0_lam_k1, l0_lam_q2, l0_lam_k2], axis=0)
    sub_g = l0_subln_g.reshape(1, LANES)
    mix = _attn_even_call(qkv, [qkv, qkv_c], lamv, sub_g, lam_init, "attn_l0")
    mix_c = _attn_even_call(qkv_c, [qkv_c], lamv, sub_g, lam_init, "attn_l0c")
    x, ctx = _ffn_sublayer(
        [(mix, w_out0, x, g1, sc2, sh2, g2), (mix_c, w_out0, ctx, cg1, csc2, csh2, cg2)],
        None, ln_g[0], ln_b[0], moe_router[0], moe_w_gate[0], moe_w_up[0], moe_w_down[0], 0)

    (sh1, sc1, g1, sh2, sc2, g2), (csh1, csc1, _, _, _, _) = mods(1)
    cw = C_HEADS * HEAD_DIM
    w_in1 = l1_w_in.astype(BF16)
    blocks1 = ([("plain", 0, Q_SCALE, 0, k * LANES) for k in range(4)]
               + [("plain", 0, 1.0, 0, (4 + k) * LANES) for k in range(8)]
               + [("plain", 0, 1.0, 1, k * LANES) for k in range(4)])
    qkv1, du = _inproj_call(x, sc1, sh1, w_in1, blocks1, [(3 * cw, BF16), (D_GROUPS * D_GROUP_DIM, F32)],
                            rope, gains0, gmat, use_rope=False, name="inproj_l1")
    blocks1c = [("plain", 0, 1.0, 0, k * LANES) for k in range(8)]
    ckv = _inproj_call(ctx, csc1, csh1, w_in1[:, cw:3 * cw], blocks1c, [(2 * cw, BF16)], rope_id, gains0, gmat,
                       use_rope=False, name="inproj_l1c")[0]
    mix1 = _odd_mixer_call(qkv1, ckv, l1_rpb, du, l1_pool_w.astype(BF16), l1_pool_scale.reshape(1, -1))
    (x,) = _ffn_sublayer([(mix1, l1_w_out.astype(BF16), x, g1, sc2, sh2, g2)],
                         None, ln_g[1], ln_b[1], moe_router[1], moe_w_gate[1], moe_w_up[1], moe_w_down[1], 1)
    return x
```

```python
import functools
import math

import numpy as np

import jax
import jax.numpy as jnp
from jax import lax
from jax.experimental import pallas as pl
from jax.experimental.pallas import tpu as pltpu

F32 = jnp.float32
BF16 = jnp.bfloat16

DEPTH = 2
GRID_W = 64
HEAD_DIM = 64
A_HEADS = 4
B_HEADS = 8
B_KV_HEADS = 2
C_HEADS = 8
D_GROUPS = 4
D_GROUP_DIM = 128
POOL_WINDOWS = (2, 4, 8, 16)
NA_ROWS = 8
NA_COLS = 16
N_EXPERTS = 16
EC_FACTOR = 2
ROPE_THETA = 10000.0
LN_EPS = 1e-5
RMS_EPS = 1e-6
DEEPNORM_ALPHA = (2 * DEPTH) ** 0.25
Q_SCALE = HEAD_DIM ** -0.5

LANES = 128
VMEM_LIMIT = 56 * 1024 * 1024

NA_QROWS = 4
NEG_BIG = -1e30

_NN = (((1,), (0,)), ((), ()))
_NT = (((1,), (1,)), ((), ()))


def _dot(a, b, dims=_NN):
    return lax.dot_general(a, b, dims, preferred_element_type=F32)


def _split_bf16(a):
    hi = a.astype(BF16)
    lo = (a - hi.astype(F32)).astype(BF16)
    return hi, lo


def _dot3(a, b, dims=_NN):
    a_hi, a_lo = _split_bf16(a)
    b_hi, b_lo = _split_bf16(b)
    return _dot(a_hi, b_hi, dims) + (_dot(a_hi, b_lo, dims) + _dot(a_lo, b_hi, dims))


def _params(sem):
    return pltpu.CompilerParams(dimension_semantics=sem, vmem_limit_bytes=VMEM_LIMIT)


def _layer_norm(z, g, b):
    mu = jnp.mean(z, axis=-1, keepdims=True)
    zc = z - mu
    var = jnp.mean(zc * zc, axis=-1, keepdims=True)
    return zc * lax.rsqrt(var + LN_EPS) * g + b


def _lane_masks():
    lane = lax.broadcasted_iota(jnp.int32, (1, LANES), 1)
    lo = jnp.where(lane < HEAD_DIM, 1.0, 0.0).astype(F32)
    return lo, 1.0 - lo


def _ada_kernel(c_ref, w_ref, b_ref, o_ref):
    c = c_ref[...]
    s = c * (1.0 / (1.0 + jnp.exp(-c)))
    o_ref[0] = _dot3(s, w_ref[0]) + b_ref[0]


def _ada_call(cc, ada_w, ada_b):
    depth, d, n = ada_w.shape
    rows = cc.shape[0]
    tn = 1536 if n % 1536 == 0 else n
    return pl.pallas_call(
        _ada_kernel,
        out_shape=jax.ShapeDtypeStruct((depth, rows, n), F32),
        grid=(depth, n // tn),
        in_specs=[pl.BlockSpec((rows, d), lambda l, j: (0, 0)),
                  pl.BlockSpec((1, d, tn), lambda l, j: (l, 0, j)),
                  pl.BlockSpec((1, 1, tn), lambda l, j: (l, 0, j))],
        out_specs=pl.BlockSpec((1, rows, tn), lambda l, j: (l, 0, j)),
        compiler_params=_params(("arbitrary", "arbitrary")),
        name="ada_mod",
    )(cc, ada_w, ada_b.reshape(depth, 1, n))


def _inproj_kernel(x_ref, sc_ref, sh_ref, w_ref, cos_ref, sinp_ref, sinm_ref, gn_ref, gmat_ref, *o_refs,
                   blocks, chunk, use_rope):
    h = (x_ref[0] * (1.0 + sc_ref[0]) + sh_ref[0]).astype(BF16)
    n = len(blocks) * LANES
    for c0 in range(0, n, chunk):
        cw = min(chunk, n - c0)
        acc = _dot(h, w_ref[:, c0:c0 + cw])
        for j in range(cw // LANES):
            kind, gain_row, factor, oi, oc = blocks[(c0 // LANES) + j]
            v = acc[:, j * LANES:(j + 1) * LANES]
            if kind == "norm":
                v2 = v * v
                hi, lo = _split_bf16(v2)
                ms = _dot(hi, gmat_ref[...]) + _dot(lo, gmat_ref[...])
                v = v * lax.rsqrt(ms + RMS_EPS) * gn_ref[gain_row:gain_row + 1, :]
            if kind in ("rope", "norm") and use_rope:
                v = (v * cos_ref[...] + pltpu.roll(v, 16, 1) * sinp_ref[...]
                     + pltpu.roll(v, LANES - 16, 1) * sinm_ref[...])
            if factor != 1.0:
                v = v * factor
            o_refs[oi][0, :, oc:oc + LANES] = v.astype(o_refs[oi].dtype)


def _inproj_call(x, sc, sh, w, blocks, out_defs, rope_tabs, gains, gmat, *, use_rope, name):
    b, s, d = x.shape
    n = w.shape[1]
    tm = min(256, s)
    per_b = sc.shape[0] == b
    mod_map = (lambda bi, i: (bi, 0, 0)) if per_b else (lambda bi, i: (0, 0, 0))
    cos, sinp, sinm = rope_tabs
    tab_spec = pl.BlockSpec((tm, LANES), lambda bi, i: (i, 0))
    kern = functools.partial(_inproj_kernel, blocks=tuple(blocks), chunk=512, use_rope=use_rope)
    return pl.pallas_call(
        kern,
        out_shape=[jax.ShapeDtypeStruct((b, s, nc), dt) for nc, dt in out_defs],
        grid=(b, s // tm),
        in_specs=[pl.BlockSpec((1, tm, d), lambda bi, i: (bi, i, 0)),
                  pl.BlockSpec((1, 1, d), mod_map),
                  pl.BlockSpec((1, 1, d), mod_map),
                  pl.BlockSpec((d, n), lambda bi, i: (0, 0)),
                  tab_spec, tab_spec, tab_spec,
                  pl.BlockSpec(gains.shape, lambda bi, i: (0, 0)),
                  pl.BlockSpec(gmat.shape, lambda bi, i: (0, 0))],
        out_specs=[pl.BlockSpec((1, tm, nc), lambda bi, i: (bi, i, 0)) for nc, _ in out_defs],
        compiler_params=_params(("arbitrary", "arbitrary")),
        name=name,
    )(x, sc, sh, w, cos, sinp, sinm, gains, gmat)


def _rope_tables(s):
    n_freq = HEAD_DIM // 4
    t = jnp.arange(s, dtype=jnp.int32)
    inv = ROPE_THETA ** (-jnp.arange(n_freq, dtype=F32) / n_freq)
    ang_r = (t // GRID_W).astype(F32)[:, None] * inv
    ang_c = (t % GRID_W).astype(F32)[:, None] * inv
    ang = jnp.concatenate([ang_r, ang_r, ang_c, ang_c] * (LANES // HEAD_DIM), axis=-1)
    first = (jnp.arange(LANES) % 32) < 16
    cos, sin = jnp.cos(ang), jnp.sin(ang)
    return cos, jnp.where(first, 0.0, sin), jnp.where(first, -sin, 0.0)


def _softmax_parts(s_list):
    m = s_list[0].max(axis=-1, keepdims=True)
    for s in s_list[1:]:
        m = jnp.maximum(m, s.max(axis=-1, keepdims=True))
    es = [jnp.exp(s - m) for s in s_list]
    den = es[0].sum(axis=-1, keepdims=True)
    for e in es[1:]:
        den = den + e.sum(axis=-1, keepdims=True)
    return es, 1.0 / den


def _attn_even_kernel(q_ref, lam_ref, sg_ref, *refs, n_kv, tq, lam_init):
    kv_refs, o_ref = refs[:n_kv], refs[n_kv]
    lo, hi = _lane_masks()
    lo_b, hi_b = lo.astype(BF16), hi.astype(BF16)
    lv = lam_ref[...]
    lam = (jnp.exp(jnp.sum(lv[0:1] * lv[1:2], axis=-1, keepdims=True))
           - jnp.exp(jnp.sum(lv[2:3] * lv[3:4], axis=-1, keepdims=True)) + lam_init)

    def scores(qq, col):
        return [_dot(qq, kv[0, :, col:col + LANES], _NT) for kv in kv_refs]

    def pv(ps, col):
        out = _dot(ps[0], kv_refs[0][0, :, col:col + LANES])
        for p, kv in zip(ps[1:], kv_refs[1:]):
            out = out + _dot(p, kv[0, :, col:col + LANES])
        return out

    for h in range(A_HEADS):
        q = q_ref[0, :, h * LANES:(h + 1) * LANES]
        qq = jnp.concatenate([q * lo_b, q * hi_b], axis=0)
        es, inv = _softmax_parts(scores(qq, 512 + h * LANES))
        coef = jnp.concatenate([inv[:tq], -lam * inv[tq:]], axis=0)
        ps = []
        for e in es:
            pe = e * coef
            ps.append((pe[:tq] + pe[tq:]).astype(BF16))
        o = pv(ps, 1024 + h * LANES)
        ms = jnp.mean(o * o, axis=-1, keepdims=True)
        o = o * lax.rsqrt(ms + RMS_EPS) * sg_ref[...] * (1.0 - lam_init)
        o_ref[0, :, h * LANES:(h + 1) * LANES] = o.astype(o_ref.dtype)

    for j in range(B_HEADS // 2):
        q = q_ref[0, :, 1536 + j * LANES:1536 + (j + 1) * LANES]
        qq = jnp.concatenate([q * lo_b, q * hi_b], axis=0)
        es, inv = _softmax_parts(scores(qq, 2048))
        o2 = pv([(e * inv).astype(BF16) for e in es], 2176)
        o = o2[:tq] * lo + o2[tq:] * hi
        o_ref[0, :, 512 + j * LANES:512 + (j + 1) * LANES] = o.astype(o_ref.dtype)


def _attn_even_call(q_arr, kv_arrs, lamv, subln_g, lam_init, name):
    b, sq, n = q_arr.shape
    tq = min(256, sq)
    kern = functools.partial(_attn_even_kernel, n_kv=len(kv_arrs), tq=tq, lam_init=lam_init)
    return pl.pallas_call(
        kern,
        out_shape=jax.ShapeDtypeStruct((b, sq, 1024), BF16),
        grid=(b, sq // tq),
        in_specs=[pl.BlockSpec((1, tq, n), lambda bi, i: (bi, i, 0)),
                  pl.BlockSpec(lamv.shape, lambda bi, i: (0, 0)),
                  pl.BlockSpec(subln_g.shape, lambda bi, i: (0, 0))]
                 + [pl.BlockSpec((1,) + a.shape[1:], lambda bi, i: (bi, 0, 0)) for a in kv_arrs],
        out_specs=pl.BlockSpec((1, tq, 1024), lambda bi, i: (bi, i, 0)),
        compiler_params=_params(("arbitrary", "arbitrary")),
        name=name,
    )(q_arr, lamv, subln_g, *kv_arrs)


def _odd_mixer_kernel(qkv_ref, ckv_ref, bias_ref, du_ref, band_ref, pw_ref, ps_ref, o_ref, *,
                      tq, slab_rows, n_rows, n_blk, seq):
    i = pl.program_id(1)
    lo, hi = _lane_masks()
    lo_b, hi_b = lo.astype(BF16), hi.astype(BF16)
    t0 = pl.multiple_of(i * tq, tq)
    base = jnp.clip(i * NA_QROWS - NA_ROWS // 2, 0, n_rows - slab_rows)
    k0 = pl.multiple_of(base * GRID_W, GRID_W)
    nk = slab_rows * GRID_W
    cw = C_HEADS * HEAD_DIM

    for j in range(C_HEADS // 2):
        cs = slice(j * LANES, (j + 1) * LANES)
        q = qkv_ref[0, pl.ds(t0, tq), cs]
        qq = jnp.concatenate([q * lo_b, q * hi_b], axis=0)
        kl = qkv_ref[0, pl.ds(k0, nk), cw + j * LANES:cw + (j + 1) * LANES]
        vl = qkv_ref[0, pl.ds(k0, nk), 2 * cw + j * LANES:2 * cw + (j + 1) * LANES]
        kc = ckv_ref[0, :, cs]
        vc = ckv_ref[0, :, cw + j * LANES:cw + (j + 1) * LANES]
        bias = jnp.concatenate([bias_ref[2 * j], bias_ref[2 * j + 1]], axis=0)
        s_l = _dot(qq, kl, _NT) + bias
        s_c = _dot(qq, kc, _NT)
        (e_l, e_c), inv = _softmax_parts([s_l, s_c])
        o2 = _dot((e_l * inv).astype(BF16), vl) + _dot((e_c * inv).astype(BF16), vc)
        o = o2[:tq] * lo + o2[tq:] * hi
        o_ref[0, :, cs] = o.astype(o_ref.dtype)

    tprev = pl.multiple_of(jnp.maximum(i - 1, 0) * tq, tq)
    tnext = pl.multiple_of(jnp.minimum(i + 1, n_blk - 1) * tq, tq)
    has_prev = jnp.where(i > 0, 1.0, 0.0).astype(F32)
    has_next = jnp.where(i < n_blk - 1, 1.0, 0.0).astype(F32)
    tpos = t0 + lax.broadcasted_iota(jnp.int32, (tq, 1), 0)
    for g in range(D_GROUPS):
        half = POOL_WINDOWS[g] // 2
        gs = slice(g * D_GROUP_DIM, (g + 1) * D_GROUP_DIM)
        cur = du_ref[0, pl.ds(t0, tq), gs]
        prv = du_ref[0, pl.ds(tprev, tq), gs]
        nxt = du_ref[0, pl.ds(tnext, tq), gs]

        def band_sum(m, u):
            u_hi, u_lo = _split_bf16(u)
            return _dot(band_ref[g, m], u_hi) + _dot(band_ref[g, m], u_lo)

        wsum = band_sum(1, cur) + has_prev * band_sum(0, prv) + has_next * band_sum(2, nxt)
        cnt = (jnp.minimum(tpos + half, seq) - jnp.maximum(tpos - half, 0)).astype(F32)
        pooled = wsum * (1.0 / cnt) - cur
        od = _dot(pooled.astype(BF16), pw_ref[g]) * ps_ref[:, gs]
        o_ref[0, :, cw + g * D_GROUP_DIM:cw + (g + 1) * D_GROUP_DIM] = od.astype(o_ref.dtype)


def _na_geometry(n_rows):
    kh = min(NA_ROWS, n_rows)
    slab = min(n_rows, NA_QROWS + kh)
    n_blk = n_rows // NA_QROWS
    bases = [min(max(i * NA_QROWS - NA_ROWS // 2, 0), n_rows - slab) for i in range(n_blk)]
    sigs, type_of = [], []
    for i in range(n_blk):
        sig = tuple((min(max(i * NA_QROWS + r - kh // 2, 0), n_rows - kh) - bases[i],
                     i * NA_QROWS + r - bases[i]) for r in range(NA_QROWS))
        if sig not in sigs:
            sigs.append(sig)
        type_of.append(sigs.index(sig))
    return kh, slab, n_blk, sigs, type_of


def _na_bias_table(rpb, n_rows):
    kh, slab, _, sigs, _ = _na_geometry(n_rows)
    kw = NA_COLS
    n_dc = 2 * NA_COLS - 1
    w = np.arange(GRID_W)
    cstart = np.clip(w - kw // 2, 0, GRID_W - kw)
    col_ok = (w[None, :] >= cstart[:, None]) & (w[None, :] < cstart[:, None] + kw)
    dc = np.clip(w[None, :] - w[:, None] + (NA_COLS - 1), 0, n_dc - 1)
    idx = np.zeros((len(sigs), NA_QROWS, GRID_W, slab, GRID_W), np.int32)
    ok = np.zeros(idx.shape, bool)
    for t, sig in enumerate(sigs):
        for r, (rs_rel, qr_rel) in enumerate(sig):
            for m in range(slab):
                dr = min(max(m - qr_rel + (NA_ROWS - 1), 0), 2 * NA_ROWS - 2)
                idx[t, r, :, m, :] = dr * n_dc + dc
                ok[t, r, :, m, :] = col_ok & (rs_rel <= m < rs_rel + kh)
    nq, nk = NA_QROWS * GRID_W, slab * GRID_W
    flat = jnp.take(rpb.reshape(rpb.shape[0], -1), jnp.asarray(idx.reshape(-1)), axis=1)
    flat = flat.reshape(rpb.shape[0], len(sigs), nq, nk)
    return jnp.where(jnp.asarray(ok.reshape(1, len(sigs), nq, nk)), flat, NEG_BIG).astype(F32)


def _pool_bands(tq):
    t = jnp.arange(tq, dtype=jnp.int32)[:, None]
    sidx = jnp.arange(tq, dtype=jnp.int32)[None, :]
    out = []
    for wdw in POOL_WINDOWS:
        half = wdw // 2
        per = []
        for m in range(3):
            srel = sidx + (m - 1) * tq
            per.append(((srel >= t - half) & (srel < t + half)).astype(BF16))
        out.append(jnp.stack(per, axis=0))
    return jnp.stack(out, axis=0)


def _odd_mixer_call(qkv, ckv, rpb, du, pool_w, pool_scale):
    b, s, n = qkv.shape
    n_rows = s // GRID_W
    _, slab, n_blk, _, type_of = _na_geometry(n_rows)
    tq = NA_QROWS * GRID_W
    bias = _na_bias_table(rpb, n_rows)
    bands = _pool_bands(tq)

    def bias_map(bi, i):
        t = jnp.int32(type_of[-1])
        for blk in range(n_blk - 2, -1, -1):
            t = jnp.where(i == blk, jnp.int32(type_of[blk]), t)
        return (0, t, 0, 0)

    kern = functools.partial(_odd_mixer_kernel, tq=tq, slab_rows=slab, n_rows=n_rows, n_blk=n_blk, seq=s)
    return pl.pallas_call(
        kern,
        out_shape=jax.ShapeDtypeStruct((b, s, 1024), BF16),
        grid=(b, n_blk),
        in_specs=[pl.BlockSpec((1, s, n), lambda bi, i: (bi, 0, 0)),
                  pl.BlockSpec((1,) + ckv.shape[1:], lambda bi, i: (bi, 0, 0)),
                  pl.BlockSpec((bias.shape[0], None) + bias.shape[2:], bias_map),
                  pl.BlockSpec((1, s, du.shape[2]), lambda bi, i: (bi, 0, 0)),
                  pl.BlockSpec(bands.shape, lambda bi, i: (0, 0, 0, 0)),
                  pl.BlockSpec(pool_w.shape, lambda bi, i: (0, 0, 0)),
                  pl.BlockSpec(pool_scale.shape, lambda bi, i: (0, 0))],
        out_specs=pl.BlockSpec((1, tq, 1024), lambda bi, i: (bi, i, 0)),
        compiler_params=_params(("arbitrary", "arbitrary")),
        name="odd_mixer",
    )(qkv, ckv, bias, du, bands, pool_w, pool_scale)


def _outproj_kernel(mix_ref, w_ref, x_ref, g1_ref, lng_ref, lnb_ref, sc_ref, sh_ref, rt_ref,
                    x1_ref, hm_ref, aff_ref):
    y = _dot(mix_ref[0], w_ref[...])
    x1 = _layer_norm(DEEPNORM_ALPHA * x_ref[0] + g1_ref[0] * y, lng_ref[...], lnb_ref[...])
    x1_ref[0] = x1
    hm = x1 * (1.0 + sc_ref[0]) + sh_ref[0]
    hm_ref[0] = hm.astype(hm_ref.dtype)
    logits = _dot3(rt_ref[...], hm, _NT)
    ex = jnp.exp(logits - logits.max(axis=0, keepdims=True))
    aff_ref[0] = ex * (1.0 / ex.sum(axis=0, keepdims=True))


def _outproj_call(mix, w, x, g1, lng, lnb, sc2, sh2, router_t, name):
    b, s, d = x.shape
    dm = mix.shape[2]
    e = router_t.shape[0]
    tm = min(256, s)
    per_b = g1.shape[0] == b
    mod_map = (lambda bi, i: (bi, 0, 0)) if per_b else (lambda bi, i: (0, 0, 0))
    vec = pl.BlockSpec((1, 1, d), mod_map)
    row = pl.BlockSpec((1, d), lambda bi, i: (0, 0))
    return pl.pallas_call(
        _outproj_kernel,
        out_shape=[jax.ShapeDtypeStruct((b, s, d), F32),
                   jax.ShapeDtypeStruct((b, s, d), BF16),
                   jax.ShapeDtypeStruct((b, e, s), F32)],
        grid=(b, s // tm),
        in_specs=[pl.BlockSpec((1, tm, dm), lambda bi, i: (bi, i, 0)),
                  pl.BlockSpec((dm, d), lambda bi, i: (0, 0)),
                  pl.BlockSpec((1, tm, d), lambda bi, i: (bi, i, 0)),
                  vec, row, row, vec, vec,
                  pl.BlockSpec((e, d), lambda bi, i: (0, 0))],
        out_specs=[pl.BlockSpec((1, tm, d), lambda bi, i: (bi, i, 0)),
                   pl.BlockSpec((1, tm, d), lambda bi, i: (bi, i, 0)),
                   pl.BlockSpec((1, e, tm), lambda bi, i: (bi, 0, i))],
        compiler_params=_params(("arbitrary", "arbitrary")),
        name=name,
    )(mix, w, x, g1, lng, lnb, sc2, sh2, router_t)


def _lane_cumsum(m):
    rows, s = m.shape
    r_i = lax.broadcasted_iota(jnp.int32, (LANES, LANES), 0)
    c_i = lax.broadcasted_iota(jnp.int32, (LANES, LANES), 1)
    tri = jnp.where(r_i <= c_i, 1.0, 0.0).astype(BF16)
    carry = jnp.zeros((rows, 1), F32)
    out = []
    for c in range(s // LANES):
        blk = m[:, c * LANES:(c + 1) * LANES]
        out.append(_dot(blk.astype(BF16), tri) + carry)
        carry = carry + blk.sum(axis=-1, keepdims=True)
    return jnp.concatenate(out, axis=-1)


def _route_kernel(aff_ref, pos_ref, *, cap):
    a = aff_ref[0]
    bits = pltpu.bitcast(a, jnp.int32)
    thr = jnp.zeros((a.shape[0], 1), jnp.int32)
    for bit in range(30, -1, -1):
        cand = thr | jnp.int32(1 << bit)
        cnt = jnp.where(bits >= cand, 1.0, 0.0).sum(axis=-1, keepdims=True)
        thr = jnp.where(cnt >= cap, cand, thr)
    gt = jnp.where(bits > thr, 1.0, 0.0)
    eq = jnp.where(bits == thr, 1.0, 0.0)
    need = cap - gt.sum(axis=-1, keepdims=True)
    sel = gt + eq * jnp.where(_lane_cumsum(eq) <= need, 1.0, 0.0)
    pos_ref[0] = jnp.where(sel > 0.5, _lane_cumsum(sel) - 1.0, -1.0)


def _route_call(aff, cap, name):
    b, e, s = aff.shape
    return pl.pallas_call(
        functools.partial(_route_kernel, cap=cap),
        out_shape=jax.ShapeDtypeStruct((b, e, s), F32),
        grid=(b,),
        in_specs=[pl.BlockSpec((1, e, s), lambda bi: (bi, 0, 0))],
        out_specs=pl.BlockSpec((1, e, s), lambda bi: (bi, 0, 0)),
        compiler_params=_params(("arbitrary",)),
        name=name,
    )(aff)


def _gather_kernel(pos_ref, aff_ref, hm_ref, xg_ref, gs_ref, *, cap):
    prow = pos_ref[0, 0]
    slot = lax.broadcasted_iota(jnp.int32, (cap, 1), 0).astype(F32)
    hit = prow == slot
    xg_ref[0, 0] = _dot(jnp.where(hit, 1.0, 0.0).astype(BF16), hm_ref[0]).astype(xg_ref.dtype)
    gs_ref[0, 0] = jnp.where(hit, aff_ref[0, 0], 0.0).sum(axis=-1, keepdims=True)


def _gather_call(pos, aff, hm, cap, name):
    b, e, s = pos.shape
    d = hm.shape[2]
    row = pl.BlockSpec((1, 1, 1, s), lambda bi, ei: (bi, ei, 0, 0))
    return pl.pallas_call(
        functools.partial(_gather_kernel, cap=cap),
        out_shape=[jax.ShapeDtypeStruct((b, e, cap, d), BF16),
                   jax.ShapeDtypeStruct((b, e, cap, 1), F32)],
        grid=(b, e),
        in_specs=[row, row, pl.BlockSpec((1, s, d), lambda bi, ei: (bi, 0, 0))],
        out_specs=[pl.BlockSpec((1, 1, cap, d), lambda bi, ei: (bi, ei, 0, 0)),
                   pl.BlockSpec((1, 1, cap, 1), lambda bi, ei: (bi, ei, 0, 0))],
        compiler_params=_params(("arbitrary", "arbitrary")),
        name=name,
    )(pos.reshape(b, e, 1, s), aff.reshape(b, e, 1, s), hm)


def _ffn_kernel(*refs, n_grp, row_chunks):
    xg_refs = refs[:n_grp]
    gs_refs = refs[n_grp:2 * n_grp]
    wg_ref, wu_ref, wd_ref = refs[2 * n_grp:2 * n_grp + 3]
    y_refs = refs[2 * n_grp + 3:3 * n_grp + 3]
    acc_refs = refs[3 * n_grp + 3:]
    f = pl.program_id(1)
    last = pl.num_programs(1) - 1
    wg = wg_ref[0].astype(BF16)
    wu = wu_ref[0].astype(BF16)
    wd = wd_ref[0].astype(BF16)

    @pl.when(f == 0)
    def _():
        for acc_ref in acc_refs:
            acc_ref[...] = jnp.zeros(acc_ref.shape, F32)

    for xg_ref, acc_ref, nb in zip(xg_refs, acc_refs, row_chunks):
        bt, _, cap, d = xg_ref.shape
        for b0 in range(0, bt, nb):
            rows = nb * cap
            r0 = b0 * cap
            x = xg_ref[b0:b0 + nb, 0].reshape(rows, d)
            hg = _dot(x, wg)
            hu = _dot(x, wu)
            hid = (hg * (1.0 / (1.0 + jnp.exp(-hg))) * hu).astype(BF16)
            acc_ref[r0:r0 + rows, :] += _dot(hid, wd)

    @pl.when(f == last)
    def _():
        for xg_ref, gs_ref, y_ref, acc_ref in zip(xg_refs, gs_refs, y_refs, acc_refs):
            bt, _, cap, d = xg_ref.shape
            gate = gs_ref[:, 0].reshape(bt * cap, 1)
            y_ref[:, 0] = (acc_ref[...] * gate).reshape(bt, cap, d).astype(y_ref.dtype)


def _ffn_call(xgs, gss, w_gate, w_up, w_down, layer, name):
    _, e, d, ff = w_gate.shape
    tf = 512 if ff % 512 == 0 else ff
    n_grp = len(xgs)
    row_chunks = []
    for xg in xgs:
        bt, _, cap, _ = xg.shape
        nb = max(1, min(bt, 512 // cap))
        while bt % nb:
            nb -= 1
        row_chunks.append(nb)
    tok = lambda a: pl.BlockSpec((a.shape[0], 1) + a.shape[2:], lambda ei, fi: (0, ei, 0, 0))
    kern = functools.partial(_ffn_kernel, n_grp=n_grp, row_chunks=tuple(row_chunks))
    return pl.pallas_call(
        kern,
        out_shape=[jax.ShapeDtypeStruct(xg.shape, BF16) for xg in xgs],
        grid=(e, ff // tf),
        in_specs=[tok(a) for a in xgs] + [tok(a) for a in gss]
                 + [pl.BlockSpec((None, 1, d, tf), lambda ei, fi: (layer, ei, 0, fi)),
                    pl.BlockSpec((None, 1, d, tf), lambda ei, fi: (layer, ei, 0, fi)),
                    pl.BlockSpec((None, 1, tf, d), lambda ei, fi: (layer, ei, fi, 0))],
        out_specs=[tok(a) for a in xgs],
        scratch_shapes=[pltpu.VMEM((xg.shape[0] * xg.shape[2], d), F32) for xg in xgs],
        compiler_params=_params(("arbitrary", "arbitrary")),
        name=name,
    )(*xgs, *gss, w_gate, w_up, w_down)


def _combine_kernel(pos_ref, y_ref, x_ref, g_ref, lng_ref, lnb_ref, o_ref, *, cap, n_exp):
    pos = pos_ref[0]
    slot = lax.broadcasted_iota(jnp.int32, (1, cap), 1).astype(F32)
    acc = None
    for ei in range(n_exp):
        onehot = jnp.where(pos[:, ei:ei + 1] == slot, 1.0, 0.0).astype(BF16)
        part = _dot(onehot, y_ref[0, ei * cap:(ei + 1) * cap, :])
        acc = part if acc is None else acc + part
    o_ref[0] = _layer_norm(DEEPNORM_ALPHA * x_ref[0] + g_ref[0] * acc, lng_ref[...], lnb_ref[...])


def _combine_call(pos_t, y, x, g2, lng, lnb, cap, name):
    b, s, d = x.shape
    e = pos_t.shape[2]
    tm = min(256, s)
    per_b = g2.shape[0] == b
    mod_map = (lambda bi, i: (bi, 0, 0)) if per_b else (lambda bi, i: (0, 0, 0))
    row = pl.BlockSpec((1, d), lambda bi, i: (0, 0))
    return pl.pallas_call(
        functools.partial(_combine_kernel, cap=cap, n_exp=e),
        out_shape=jax.ShapeDtypeStruct((b, s, d), F32),
        grid=(b, s // tm),
        in_specs=[pl.BlockSpec((1, tm, e), lambda bi, i: (bi, i, 0)),
                  pl.BlockSpec((1, e * cap, d), lambda bi, i: (bi, 0, 0)),
                  pl.BlockSpec((1, tm, d), lambda bi, i: (bi, i, 0)),
                  pl.BlockSpec((1, 1, d), mod_map), row, row],
        out_specs=pl.BlockSpec((1, tm, d), lambda bi, i: (bi, i, 0)),
        compiler_params=_params(("arbitrary", "arbitrary")),
        name=name,
    )(pos_t, y.reshape(b, e * cap, d), x, g2, lng, lnb)


def _ffn_sublayer(streams, mods, lng, lnb, router, w_gate, w_up, w_down, layer):
    router_t = router.T
    staged = []
    for si, (mix, w_out, x, g1, sc2, sh2, g2) in enumerate(streams):
        tag = f"l{layer}s{si}"
        n_tok = x.shape[1]
        cap = EC_FACTOR * n_tok // N_EXPERTS
        x1, hm, aff = _outproj_call(mix, w_out, x, g1, lng[0:1], lnb[0:1], sc2, sh2, router_t, "outproj_" + tag)
        pos = _route_call(aff, cap, "route_" + tag)
        xg, gs = _gather_call(pos, aff, hm, cap, "gather_" + tag)
        staged.append((x1, pos, xg, gs, g2, cap, tag))
    ys = _ffn_call([st[2] for st in staged], [st[3] for st in staged], w_gate, w_up, w_down, layer,
                   f"ffn_l{layer}")
    outs = []
    for (x1, pos, _, _, g2, cap, tag), y in zip(staged, ys):
        outs.append(_combine_call(jnp.swapaxes(pos, 1, 2), y, x1, g2, lng[1:2], lnb[1:2], cap, "combine_" + tag))
    return outs


def kernel(x, c, ctx, c_ctx, ada_w, ada_b, ln_g, ln_b, l0_w_in, l0_w_out, l0_lam_q1, l0_lam_k1, l0_lam_q2,
           l0_lam_k2, l0_subln_g, l0_qnorm_g, l0_knorm_g, l1_w_in, l1_w_out, l1_rpb, l1_pool_w, l1_pool_scale,
           moe_router, moe_w_gate, moe_w_up, moe_w_down):
    b, s, d = x.shape
    n_ctx = ctx.shape[1]

    rows = -(-(b + 1) // 8) * 8
    cc = jnp.zeros((rows, d), F32).at[:b].set(c).at[b].set(c_ctx)
    mod = _ada_call(cc, ada_w, ada_b)

    def mods(i):
        lat = [mod[i, :b, k * d:(k + 1) * d].reshape(b, 1, d) for k in range(6)]
        cx = [mod[i, b:b + 1, k * d:(k + 1) * d].reshape(1, 1, d) for k in range(6)]
        return lat, cx

    rope = _rope_tables(s)
    rope_id = (jnp.ones((n_ctx, LANES), F32), jnp.zeros((n_ctx, LANES), F32), jnp.zeros((n_ctx, LANES), F32))
    gmat = jnp.where((jnp.arange(LANES)[:, None] // HEAD_DIM) == (jnp.arange(LANES)[None, :] // HEAD_DIM),
                     1.0 / HEAD_DIM, 0.0).astype(BF16)
    tile2 = lambda g: jnp.concatenate([g, g]).reshape(1, LANES)

    (sh1, sc1, g1, sh2, sc2, g2), (csh1, csc1, cg1, csh2, csc2, cg2) = mods(0)
    bq0 = 3 * 512
    pair_cols = jnp.concatenate([jnp.arange(HEAD_DIM) + bq0 + hh * HEAD_DIM
                                 for j in range(B_HEADS // 2) for hh in (j, j + B_HEADS // 2)])
    in_perm = jnp.concatenate([jnp.arange(bq0), pair_cols, jnp.arange(bq0 + 512, l0_w_in.shape[1])])
    w_in0 = l0_w_in[:, in_perm].astype(BF16)
    w_out0 = l0_w_out[jnp.concatenate([jnp.arange(512), pair_cols - bq0 + 512])].astype(BF16)
    gains0 = jnp.concatenate([tile2(l0_qnorm_g), tile2(l0_knorm_g)], axis=0)
    blocks0 = ([("rope", 0, Q_SCALE, 0, k * LANES) for k in range(4)]
               + [("rope", 0, 1.0, 0, (4 + k) * LANES) for k in range(4)]
               + [("plain", 0, 1.0, 0, (8 + k) * LANES) for k in range(4)]
               + [("norm", 0, Q_SCALE, 0, (12 + k) * LANES) for k in range(4)]
               + [("norm", 1, 1.0, 0, 16 * LANES), ("plain", 0, 1.0, 0, 17 * LANES)])
    n0 = len(blocks0) * LANES
    qkv = _inproj_call(x, sc1, sh1, w_in0, blocks0, [(n0, BF16)], rope, gains0, gmat,
                       use_rope=True, name="inproj_l0")[0]
    qkv_c = _inproj_call(ctx, csc1, csh1, w_in0, blocks0, [(n0, BF16)], rope_id, gains0, gmat,
                         use_rope=False, name="inproj_l0c")[0]
    lam_init = 0.8 - 0.6 * math.exp(-0.3 * 0)
    lamv = jnp.stack([l0_lam_q1, l0_lam_k1, l0_lam_q2, l0_lam_k2], axis=0)
    sub_g = l0_subln_g.reshape(1, LANES)
    mix = _attn_even_call(qkv, [qkv, qkv_c], lamv, sub_g, lam_init, "attn_l0")
    mix_c = _attn_even_call(qkv_c, [qkv_c], lamv, sub_g, lam_init, "attn_l0c")
    x, ctx = _ffn_sublayer(
        [(mix, w_out0, x, g1, sc2, sh2, g2), (mix_c, w_out0, ctx, cg1, csc2, csh2, cg2)],
        None, ln_g[0], ln_b[0], moe_router[0], moe_w_gate, moe_w_up, moe_w_down, 0)

    (sh1, sc1, g1, sh2, sc2, g2), (csh1, csc1, _, _, _, _) = mods(1)
    cw = C_HEADS * HEAD_DIM
    w_in1 = l1_w_in.astype(BF16)
    blocks1 = ([("plain", 0, Q_SCALE, 0, k * LANES) for k in range(4)]
               + [("plain", 0, 1.0, 0, (4 + k) * LANES) for k in range(8)]
               + [("plain", 0, 1.0, 1, k * LANES) for k in range(4)])
    qkv1, du = _inproj_call(x, sc1, sh1, w_in1, blocks1, [(3 * cw, BF16), (D_GROUPS * D_GROUP_DIM, F32)],
                            rope, gains0, gmat, use_rope=False, name="inproj_l1")
    blocks1c = [("plain", 0, 1.0, 0, k * LANES) for k in range(8)]
    ckv = _inproj_call(ctx, csc1, csh1, w_in1[:, cw:3 * cw], blocks1c, [(2 * cw, BF16)], rope_id, gains0, gmat,
                       use_rope=False, name="inproj_l1c")[0]
    mix1 = _odd_mixer_call(qkv1, ckv, l1_rpb, du, l1_pool_w.astype(BF16), l1_pool_scale.reshape(1, -1))
    (x,) = _ffn_sublayer([(mix1, l1_w_out.astype(BF16), x, g1, sc2, sh2, g2)],
                         None, ln_g[1], ln_b[1], moe_router[1], moe_w_gate, moe_w_up, moe_w_down, 1)
    return x
```

```python
import functools
import math

import numpy as np

import jax
import jax.numpy as jnp
from jax import lax
from jax.experimental import pallas as pl
from jax.experimental.pallas import tpu as pltpu

F32 = jnp.float32
BF16 = jnp.bfloat16

DEPTH = 2
GRID_W = 64
HEAD_DIM = 64
A_HEADS = 4
B_HEADS = 8
B_KV_HEADS = 2
C_HEADS = 8
D_GROUPS = 4
D_GROUP_DIM = 128
POOL_WINDOWS = (2, 4, 8, 16)
NA_ROWS = 8
NA_COLS = 16
N_EXPERTS = 16
EC_FACTOR = 2
ROPE_THETA = 10000.0
LN_EPS = 1e-5
RMS_EPS = 1e-6
DEEPNORM_ALPHA = (2 * DEPTH) ** 0.25
Q_SCALE = HEAD_DIM ** -0.5

LANES = 128
VMEM_LIMIT = 56 * 1024 * 1024

NA_QROWS = 4
NEG_BIG = -1e30

_NN = (((1,), (0,)), ((), ()))
_NT = (((1,), (1,)), ((), ()))


def _dot(a, b, dims=_NN):
    return lax.dot_general(a, b, dims, preferred_element_type=F32)


def _split_bf16(a):
    hi = a.astype(BF16)
    lo = (a - hi.astype(F32)).astype(BF16)
    return hi, lo


def _dot3(a, b, dims=_NN):
    a_hi, a_lo = _split_bf16(a)
    b_hi, b_lo = _split_bf16(b)
    return _dot(a_hi, b_hi, dims) + (_dot(a_hi, b_lo, dims) + _dot(a_lo, b_hi, dims))


def _params(sem):
    return pltpu.CompilerParams(dimension_semantics=sem, vmem_limit_bytes=VMEM_LIMIT)


def _layer_norm(z, g, b):
    mu = jnp.mean(z, axis=-1, keepdims=True)
    zc = z - mu
    var = jnp.mean(zc * zc, axis=-1, keepdims=True)
    return zc * lax.rsqrt(var + LN_EPS) * g + b


def _lane_masks():
    lane = lax.broadcasted_iota(jnp.int32, (1, LANES), 1)
    lo = jnp.where(lane < HEAD_DIM, 1.0, 0.0).astype(F32)
    return lo, 1.0 - lo


def _ada_kernel(c_ref, w_ref, b_ref, o_ref):
    c = c_ref[...]
    s = c * (1.0 / (1.0 + jnp.exp(-c)))
    o_ref[0] = _dot3(s, w_ref[0]) + b_ref[0]


def _ada_call(cc, ada_w, ada_b):
    depth, d, n = ada_w.shape
    rows = cc.shape[0]
    tn = 1536 if n % 1536 == 0 else n
    return pl.pallas_call(
        _ada_kernel,
        out_shape=jax.ShapeDtypeStruct((depth, rows, n), F32),
        grid=(depth, n // tn),
        in_specs=[pl.BlockSpec((rows, d), lambda l, j: (0, 0)),
                  pl.BlockSpec((1, d, tn), lambda l, j: (l, 0, j)),
                  pl.BlockSpec((1, 1, tn), lambda l, j: (l, 0, j))],
        out_specs=pl.BlockSpec((1, rows, tn), lambda l, j: (l, 0, j)),
        compiler_params=_params(("arbitrary", "arbitrary")),
        name="ada_mod",
    )(cc, ada_w, ada_b.reshape(depth, 1, n))


def _inproj_kernel(x_ref, sc_ref, sh_ref, w_ref, cos_ref, sinp_ref, sinm_ref, gn_ref, gmat_ref, *o_refs,
                   blocks, chunk, use_rope):
    h = (x_ref[0] * (1.0 + sc_ref[0]) + sh_ref[0]).astype(BF16)
    n = len(blocks) * LANES
    for c0 in range(0, n, chunk):
        cw = min(chunk, n - c0)
        acc = _dot(h, w_ref[:, c0:c0 + cw])
        for j in range(cw // LANES):
            kind, gain_row, factor, oi, oc = blocks[(c0 // LANES) + j]
            v = acc[:, j * LANES:(j + 1) * LANES]
            if kind == "norm":
                v2 = v * v
                hi, lo = _split_bf16(v2)
                ms = _dot(hi, gmat_ref[...]) + _dot(lo, gmat_ref[...])
                v = v * lax.rsqrt(ms + RMS_EPS) * gn_ref[gain_row:gain_row + 1, :]
            if kind in ("rope", "norm") and use_rope:
                v = (v * cos_ref[...] + pltpu.roll(v, 16, 1) * sinp_ref[...]
                     + pltpu.roll(v, LANES - 16, 1) * sinm_ref[...])
            if factor != 1.0:
                v = v * factor
            o_refs[oi][0, :, oc:oc + LANES] = v.astype(o_refs[oi].dtype)


def _inproj_call(x, sc, sh, w, blocks, out_defs, rope_tabs, gains, gmat, *, use_rope, name):
    b, s, d = x.shape
    n = w.shape[1]
    tm = min(256, s)
    per_b = sc.shape[0] == b
    mod_map = (lambda bi, i: (bi, 0, 0)) if per_b else (lambda bi, i: (0, 0, 0))
    cos, sinp, sinm = rope_tabs
    tab_spec = pl.BlockSpec((tm, LANES), lambda bi, i: (i, 0))
    kern = functools.partial(_inproj_kernel, blocks=tuple(blocks), chunk=512, use_rope=use_rope)
    return pl.pallas_call(
        kern,
        out_shape=[jax.ShapeDtypeStruct((b, s, nc), dt) for nc, dt in out_defs],
        grid=(b, s // tm),
        in_specs=[pl.BlockSpec((1, tm, d), lambda bi, i: (bi, i, 0)),
                  pl.BlockSpec((1, 1, d), mod_map),
                  pl.BlockSpec((1, 1, d), mod_map),
                  pl.BlockSpec((d, n), lambda bi, i: (0, 0)),
                  tab_spec, tab_spec, tab_spec,
                  pl.BlockSpec(gains.shape, lambda bi, i: (0, 0)),
                  pl.BlockSpec(gmat.shape, lambda bi, i: (0, 0))],
        out_specs=[pl.BlockSpec((1, tm, nc), lambda bi, i: (bi, i, 0)) for nc, _ in out_defs],
        compiler_params=_params(("arbitrary", "arbitrary")),
        name=name,
    )(x, sc, sh, w, cos, sinp, sinm, gains, gmat)


def _rope_tables(s):
    n_freq = HEAD_DIM // 4
    t = jnp.arange(s, dtype=jnp.int32)
    inv = ROPE_THETA ** (-jnp.arange(n_freq, dtype=F32) / n_freq)
    ang_r = (t // GRID_W).astype(F32)[:, None] * inv
    ang_c = (t % GRID_W).astype(F32)[:, None] * inv
    ang = jnp.concatenate([ang_r, ang_r, ang_c, ang_c] * (LANES // HEAD_DIM), axis=-1)
    first = (jnp.arange(LANES) % 32) < 16
    cos, sin = jnp.cos(ang), jnp.sin(ang)
    return cos, jnp.where(first, 0.0, sin), jnp.where(first, -sin, 0.0)


def _softmax_parts(s_list):
    m = s_list[0].max(axis=-1, keepdims=True)
    for s in s_list[1:]:
        m = jnp.maximum(m, s.max(axis=-1, keepdims=True))
    es = [jnp.exp(s - m) for s in s_list]
    den = es[0].sum(axis=-1, keepdims=True)
    for e in es[1:]:
        den = den + e.sum(axis=-1, keepdims=True)
    return es, 1.0 / den


def _attn_even_kernel(q_ref, lam_ref, sg_ref, *refs, n_kv, tq, lam_init):
    kv_refs, o_ref = refs[:n_kv], refs[n_kv]
    lo, hi = _lane_masks()
    lo_b, hi_b = lo.astype(BF16), hi.astype(BF16)
    lv = lam_ref[...]
    lam = (jnp.exp(jnp.sum(lv[0:1] * lv[1:2], axis=-1, keepdims=True))
           - jnp.exp(jnp.sum(lv[2:3] * lv[3:4], axis=-1, keepdims=True)) + lam_init)

    def scores(qq, col):
        return [_dot(qq, kv[0, :, col:col + LANES], _NT) for kv in kv_refs]

    def pv(ps, col):
        out = _dot(ps[0], kv_refs[0][0, :, col:col + LANES])
        for p, kv in zip(ps[1:], kv_refs[1:]):
            out = out + _dot(p, kv[0, :, col:col + LANES])
        return out

    for h in range(A_HEADS):
        q = q_ref[0, :, h * LANES:(h + 1) * LANES]
        qq = jnp.concatenate([q * lo_b, q * hi_b], axis=0)
        es, inv = _softmax_parts(scores(qq, 512 + h * LANES))
        coef = jnp.concatenate([inv[:tq], -lam * inv[tq:]], axis=0)
        ps = []
        for e in es:
            pe = e * coef
            ps.append((pe[:tq] + pe[tq:]).astype(BF16))
        o = pv(ps, 1024 + h * LANES)
        ms = jnp.mean(o * o, axis=-1, keepdims=True)
        o = o * lax.rsqrt(ms + RMS_EPS) * sg_ref[...] * (1.0 - lam_init)
        o_ref[0, :, h * LANES:(h + 1) * LANES] = o.astype(o_ref.dtype)

    for j in range(B_HEADS // 2):
        q = q_ref[0, :, 1536 + j * LANES:1536 + (j + 1) * LANES]
        qq = jnp.concatenate([q * lo_b, q * hi_b], axis=0)
        es, inv = _softmax_parts(scores(qq, 2048))
        o2 = pv([(e * inv).astype(BF16) for e in es], 2176)
        o = o2[:tq] * lo + o2[tq:] * hi
        o_ref[0, :, 512 + j * LANES:512 + (j + 1) * LANES] = o.astype(o_ref.dtype)


def _attn_even_call(q_arr, kv_arrs, lamv, subln_g, lam_init, name):
    b, sq, n = q_arr.shape
    tq = min(256, sq)
    kern = functools.partial(_attn_even_kernel, n_kv=len(kv_arrs), tq=tq, lam_init=lam_init)
    return pl.pallas_call(
        kern,
        out_shape=jax.ShapeDtypeStruct((b, sq, 1024), BF16),
        grid=(b, sq // tq),
        in_specs=[pl.BlockSpec((1, tq, n), lambda bi, i: (bi, i, 0)),
                  pl.BlockSpec(lamv.shape, lambda bi, i: (0, 0)),
                  pl.BlockSpec(subln_g.shape, lambda bi, i: (0, 0))]
                 + [pl.BlockSpec((1,) + a.shape[1:], lambda bi, i: (bi, 0, 0)) for a in kv_arrs],
        out_specs=pl.BlockSpec((1, tq, 1024), lambda bi, i: (bi, i, 0)),
        compiler_params=_params(("arbitrary", "arbitrary")),
        name=name,
    )(q_arr, lamv, subln_g, *kv_arrs)


def _odd_mixer_kernel(qkv_ref, ckv_ref, bias_ref, du_ref, band_ref, pw_ref, ps_ref, o_ref, *,
                      tq, slab_rows, n_rows, n_blk, seq):
    i = pl.program_id(1)
    lo, hi = _lane_masks()
    lo_b, hi_b = lo.astype(BF16), hi.astype(BF16)
    t0 = pl.multiple_of(i * tq, tq)
    base = jnp.clip(i * NA_QROWS - NA_ROWS // 2, 0, n_rows - slab_rows)
    k0 = pl.multiple_of(base * GRID_W, GRID_W)
    nk = slab_rows * GRID_W
    cw = C_HEADS * HEAD_DIM

    for j in range(C_HEADS // 2):
        cs = slice(j * LANES, (j + 1) * LANES)
        q = qkv_ref[0, pl.ds(t0, tq), cs]
        qq = jnp.concatenate([q * lo_b, q * hi_b], axis=0)
        kl = qkv_ref[0, pl.ds(k0, nk), cw + j * LANES:cw + (j + 1) * LANES]
        vl = qkv_ref[0, pl.ds(k0, nk), 2 * cw + j * LANES:2 * cw + (j + 1) * LANES]
        kc = ckv_ref[0, :, cs]
        vc = ckv_ref[0, :, cw + j * LANES:cw + (j + 1) * LANES]
        bias = jnp.concatenate([bias_ref[2 * j], bias_ref[2 * j + 1]], axis=0)
        s_l = _dot(qq, kl, _NT) + bias
        s_c = _dot(qq, kc, _NT)
        (e_l, e_c), inv = _softmax_parts([s_l, s_c])
        o2 = _dot((e_l * inv).astype(BF16), vl) + _dot((e_c * inv).astype(BF16), vc)
        o = o2[:tq] * lo + o2[tq:] * hi
        o_ref[0, :, cs] = o.astype(o_ref.dtype)

    tprev = pl.multiple_of(jnp.maximum(i - 1, 0) * tq, tq)
    tnext = pl.multiple_of(jnp.minimum(i + 1, n_blk - 1) * tq, tq)
    has_prev = jnp.where(i > 0, 1.0, 0.0).astype(F32)
    has_next = jnp.where(i < n_blk - 1, 1.0, 0.0).astype(F32)
    tpos = t0 + lax.broadcasted_iota(jnp.int32, (tq, 1), 0)
    for g in range(D_GROUPS):
        half = POOL_WINDOWS[g] // 2
        gs = slice(g * D_GROUP_DIM, (g + 1) * D_GROUP_DIM)
        cur = du_ref[0, pl.ds(t0, tq), gs]
        prv = du_ref[0, pl.ds(tprev, tq), gs]
        nxt = du_ref[0, pl.ds(tnext, tq), gs]

        def band_sum(m, u):
            u_hi, u_lo = _split_bf16(u)
            return _dot(band_ref[g, m], u_hi) + _dot(band_ref[g, m], u_lo)

        wsum = band_sum(1, cur) + has_prev * band_sum(0, prv) + has_next * band_sum(2, nxt)
        cnt = (jnp.minimum(tpos + half, seq) - jnp.maximum(tpos - half, 0)).astype(F32)
        pooled = wsum * (1.0 / cnt) - cur
        od = _dot(pooled.astype(BF16), pw_ref[g]) * ps_ref[:, gs]
        o_ref[0, :, cw + g * D_GROUP_DIM:cw + (g + 1) * D_GROUP_DIM] = od.astype(o_ref.dtype)


def _na_geometry(n_rows):
    kh = min(NA_ROWS, n_rows)
    slab = min(n_rows, NA_QROWS + kh)
    n_blk = n_rows // NA_QROWS
    bases = [min(max(i * NA_QROWS - NA_ROWS // 2, 0), n_rows - slab) for i in range(n_blk)]
    sigs, type_of = [], []
    for i in range(n_blk):
        sig = tuple((min(max(i * NA_QROWS + r - kh // 2, 0), n_rows - kh) - bases[i],
                     i * NA_QROWS + r - bases[i]) for r in range(NA_QROWS))
        if sig not in sigs:
            sigs.append(sig)
        type_of.append(sigs.index(sig))
    return kh, slab, n_blk, sigs, type_of


def _na_bias_table(rpb, n_rows):
    kh, slab, _, sigs, _ = _na_geometry(n_rows)
    kw = NA_COLS
    n_dc = 2 * NA_COLS - 1
    w = np.arange(GRID_W)
    cstart = np.clip(w - kw // 2, 0, GRID_W - kw)
    col_ok = (w[None, :] >= cstart[:, None]) & (w[None, :] < cstart[:, None] + kw)
    dc = np.clip(w[None, :] - w[:, None] + (NA_COLS - 1), 0, n_dc - 1)
    n_dr = 2 * NA_ROWS - 1
    heads = rpb.shape[0]
    onehot = ((dc[None] == np.arange(n_dc)[:, None, None]) & col_ok[None]).astype(np.float32)
    toep = jnp.einsum("hrd,dwj->hrwj", rpb, jnp.asarray(onehot), precision=lax.Precision.HIGHEST)
    toep = jnp.where(jnp.asarray(col_ok), toep, NEG_BIG)
    toep = jnp.concatenate([toep, jnp.full((heads, 1, GRID_W, GRID_W), NEG_BIG, F32)], axis=1)
    dr_idx = np.full((len(sigs), NA_QROWS, slab), n_dr, np.int32)
    for t, sig in enumerate(sigs):
        for r, (rs_rel, qr_rel) in enumerate(sig):
            for m in range(slab):
                if rs_rel <= m < rs_rel + kh:
                    dr_idx[t, r, m] = m - qr_rel + (NA_ROWS - 1)
    big = jnp.take(toep, jnp.asarray(dr_idx.reshape(-1)), axis=1)
    big = big.reshape(heads, len(sigs), NA_QROWS, slab, GRID_W, GRID_W).transpose(0, 1, 2, 4, 3, 5)
    return big.reshape(heads, len(sigs), NA_QROWS * GRID_W, slab * GRID_W)


def _pool_bands(tq):
    t = jnp.arange(tq, dtype=jnp.int32)[:, None]
    sidx = jnp.arange(tq, dtype=jnp.int32)[None, :]
    out = []
    for wdw in POOL_WINDOWS:
        half = wdw // 2
        per = []
        for m in range(3):
            srel = sidx + (m - 1) * tq
            per.append(((srel >= t - half) & (srel < t + half)).astype(BF16))
        out.append(jnp.stack(per, axis=0))
    return jnp.stack(out, axis=0)


def _odd_mixer_call(qkv, ckv, rpb, du, pool_w, pool_scale):
    b, s, n = qkv.shape
    n_rows = s // GRID_W
    _, slab, n_blk, _, type_of = _na_geometry(n_rows)
    tq = NA_QROWS * GRID_W
    bias = _na_bias_table(rpb, n_rows)
    bands = _pool_bands(tq)

    def bias_map(bi, i):
        t = jnp.int32(type_of[-1])
        for blk in range(n_blk - 2, -1, -1):
            t = jnp.where(i == blk, jnp.int32(type_of[blk]), t)
        return (0, t, 0, 0)

    kern = functools.partial(_odd_mixer_kernel, tq=tq, slab_rows=slab, n_rows=n_rows, n_blk=n_blk, seq=s)
    return pl.pallas_call(
        kern,
        out_shape=jax.ShapeDtypeStruct((b, s, 1024), BF16),
        grid=(b, n_blk),
        in_specs=[pl.BlockSpec((1, s, n), lambda bi, i: (bi, 0, 0)),
                  pl.BlockSpec((1,) + ckv.shape[1:], lambda bi, i: (bi, 0, 0)),
                  pl.BlockSpec((bias.shape[0], None) + bias.shape[2:], bias_map),
                  pl.BlockSpec((1, s, du.shape[2]), lambda bi, i: (bi, 0, 0)),
                  pl.BlockSpec(bands.shape, lambda bi, i: (0, 0, 0, 0)),
                  pl.BlockSpec(pool_w.shape, lambda bi, i: (0, 0, 0)),
                  pl.BlockSpec(pool_scale.shape, lambda bi, i: (0, 0))],
        out_specs=pl.BlockSpec((1, tq, 1024), lambda bi, i: (bi, i, 0)),
        compiler_params=_params(("arbitrary", "arbitrary")),
        name="odd_mixer",
    )(qkv, ckv, bias, du, bands, pool_w, pool_scale)


def _outproj_kernel(mix_ref, w_ref, x_ref, g1_ref, lng_ref, lnb_ref, sc_ref, sh_ref, rt_ref,
                    x1_ref, hm_ref, aff_ref):
    y = _dot(mix_ref[0], w_ref[...])
    x1 = _layer_norm(DEEPNORM_ALPHA * x_ref[0] + g1_ref[0] * y, lng_ref[...], lnb_ref[...])
    x1_ref[0] = x1
    hm = x1 * (1.0 + sc_ref[0]) + sh_ref[0]
    hm_ref[0] = hm.astype(hm_ref.dtype)
    logits = _dot3(rt_ref[...], hm, _NT)
    ex = jnp.exp(logits - logits.max(axis=0, keepdims=True))
    aff_ref[0] = ex * (1.0 / ex.sum(axis=0, keepdims=True))


def _outproj_call(mix, w, x, g1, lng, lnb, sc2, sh2, router_t, name):
    b, s, d = x.shape
    dm = mix.shape[2]
    e = router_t.shape[0]
    tm = min(256, s)
    per_b = g1.shape[0] == b
    mod_map = (lambda bi, i: (bi, 0, 0)) if per_b else (lambda bi, i: (0, 0, 0))
    vec = pl.BlockSpec((1, 1, d), mod_map)
    row = pl.BlockSpec((1, d), lambda bi, i: (0, 0))
    return pl.pallas_call(
        _outproj_kernel,
        out_shape=[jax.ShapeDtypeStruct((b, s, d), F32),
                   jax.ShapeDtypeStruct((b, s, d), BF16),
                   jax.ShapeDtypeStruct((b, e, s), F32)],
        grid=(b, s // tm),
        in_specs=[pl.BlockSpec((1, tm, dm), lambda bi, i: (bi, i, 0)),
                  pl.BlockSpec((dm, d), lambda bi, i: (0, 0)),
                  pl.BlockSpec((1, tm, d), lambda bi, i: (bi, i, 0)),
                  vec, row, row, vec, vec,
                  pl.BlockSpec((e, d), lambda bi, i: (0, 0))],
        out_specs=[pl.BlockSpec((1, tm, d), lambda bi, i: (bi, i, 0)),
                   pl.BlockSpec((1, tm, d), lambda bi, i: (bi, i, 0)),
                   pl.BlockSpec((1, e, tm), lambda bi, i: (bi, 0, i))],
        compiler_params=_params(("arbitrary", "arbitrary")),
        name=name,
    )(mix, w, x, g1, lng, lnb, sc2, sh2, router_t)


def _lane_cumsum(m):
    rows, s = m.shape
    r_i = lax.broadcasted_iota(jnp.int32, (LANES, LANES), 0)
    c_i = lax.broadcasted_iota(jnp.int32, (LANES, LANES), 1)
    tri = jnp.where(r_i <= c_i, 1.0, 0.0).astype(BF16)
    carry = jnp.zeros((rows, 1), F32)
    out = []
    for c in range(s // LANES):
        blk = m[:, c * LANES:(c + 1) * LANES]
        out.append(_dot(blk.astype(BF16), tri) + carry)
        carry = carry + blk.sum(axis=-1, keepdims=True)
    return jnp.concatenate(out, axis=-1)


def _route_kernel(aff_ref, pos_ref, *, cap):
    a = aff_ref[0]
    bits = pltpu.bitcast(a, jnp.int32)
    thr = jnp.zeros((a.shape[0], 1), jnp.int32)
    for bit in range(30, -1, -1):
        cand = thr | jnp.int32(1 << bit)
        cnt = jnp.where(bits >= cand, 1.0, 0.0).sum(axis=-1, keepdims=True)
        thr = jnp.where(cnt >= cap, cand, thr)
    gt = jnp.where(bits > thr, 1.0, 0.0)
    eq = jnp.where(bits == thr, 1.0, 0.0)
    need = cap - gt.sum(axis=-1, keepdims=True)
    sel = gt + eq * jnp.where(_lane_cumsum(eq) <= need, 1.0, 0.0)
    pos_ref[0] = jnp.where(sel > 0.5, _lane_cumsum(sel) - 1.0, -1.0)


def _route_call(aff, cap, name):
    b, e, s = aff.shape
    return pl.pallas_call(
        functools.partial(_route_kernel, cap=cap),
        out_shape=jax.ShapeDtypeStruct((b, e, s), F32),
        grid=(b,),
        in_specs=[pl.BlockSpec((1, e, s), lambda bi: (bi, 0, 0))],
        out_specs=pl.BlockSpec((1, e, s), lambda bi: (bi, 0, 0)),
        compiler_params=_params(("arbitrary",)),
        name=name,
    )(aff)


def _gather_kernel(pos_ref, aff_ref, hm_ref, xg_ref, gs_ref, *, cap):
    prow = pos_ref[0, 0]
    slot = lax.broadcasted_iota(jnp.int32, (cap, 1), 0).astype(F32)
    hit = prow == slot
    xg_ref[0, 0] = _dot(jnp.where(hit, 1.0, 0.0).astype(BF16), hm_ref[0]).astype(xg_ref.dtype)
    gs_ref[0, 0] = jnp.where(hit, aff_ref[0, 0], 0.0).sum(axis=-1, keepdims=True)


def _gather_call(pos, aff, hm, cap, name):
    b, e, s = pos.shape
    d = hm.shape[2]
    row = pl.BlockSpec((1, 1, 1, s), lambda bi, ei: (bi, ei, 0, 0))
    return pl.pallas_call(
        functools.partial(_gather_kernel, cap=cap),
        out_shape=[jax.ShapeDtypeStruct((b, e, cap, d), BF16),
                   jax.ShapeDtypeStruct((b, e, cap, 1), F32)],
        grid=(b, e),
        in_specs=[row, row, pl.BlockSpec((1, s, d), lambda bi, ei: (bi, 0, 0))],
        out_specs=[pl.BlockSpec((1, 1, cap, d), lambda bi, ei: (bi, ei, 0, 0)),
                   pl.BlockSpec((1, 1, cap, 1), lambda bi, ei: (bi, ei, 0, 0))],
        compiler_params=_params(("arbitrary", "arbitrary")),
        name=name,
    )(pos.reshape(b, e, 1, s), aff.reshape(b, e, 1, s), hm)


def _ffn_kernel(*refs, n_grp, row_chunks):
    xg_refs = refs[:n_grp]
    gs_refs = refs[n_grp:2 * n_grp]
    wg_ref, wu_ref, wd_ref = refs[2 * n_grp:2 * n_grp + 3]
    y_refs = refs[2 * n_grp + 3:3 * n_grp + 3]
    acc_refs = refs[3 * n_grp + 3:]
    f = pl.program_id(1)
    last = pl.num_programs(1) - 1
    wg = wg_ref[0].astype(BF16)
    wu = wu_ref[0].astype(BF16)
    wd = wd_ref[0].astype(BF16)

    @pl.when(f == 0)
    def _():
        for acc_ref in acc_refs:
            acc_ref[...] = jnp.zeros(acc_ref.shape, F32)

    for xg_ref, acc_ref, nb in zip(xg_refs, acc_refs, row_chunks):
        bt, _, cap, d = xg_ref.shape
        for b0 in range(0, bt, nb):
            rows = nb * cap
            r0 = b0 * cap
            x = xg_ref[b0:b0 + nb, 0].reshape(rows, d)
            hg = _dot(x, wg)
            hu = _dot(x, wu)
            hid = (hg * (1.0 / (1.0 + jnp.exp(-hg))) * hu).astype(BF16)
            acc_ref[r0:r0 + rows, :] += _dot(hid, wd)

    @pl.when(f == last)
    def _():
        for xg_ref, gs_ref, y_ref, acc_ref in zip(xg_refs, gs_refs, y_refs, acc_refs):
            bt, _, cap, d = xg_ref.shape
            gate = gs_ref[:, 0].reshape(bt * cap, 1)
            y_ref[:, 0] = (acc_ref[...] * gate).reshape(bt, cap, d).astype(y_ref.dtype)


def _ffn_call(xgs, gss, w_gate, w_up, w_down, layer, name):
    _, e, d, ff = w_gate.shape
    tf = 512 if ff % 512 == 0 else ff
    n_grp = len(xgs)
    row_chunks = []
    for xg in xgs:
        bt, _, cap, _ = xg.shape
        nb = max(1, min(bt, 512 // cap))
        while bt % nb:
            nb -= 1
        row_chunks.append(nb)
    tok = lambda a: pl.BlockSpec((a.shape[0], 1) + a.shape[2:], lambda ei, fi: (0, ei, 0, 0))
    kern = functools.partial(_ffn_kernel, n_grp=n_grp, row_chunks=tuple(row_chunks))
    return pl.pallas_call(
        kern,
        out_shape=[jax.ShapeDtypeStruct(xg.shape, BF16) for xg in xgs],
        grid=(e, ff // tf),
        in_specs=[tok(a) for a in xgs] + [tok(a) for a in gss]
                 + [pl.BlockSpec((None, 1, d, tf), lambda ei, fi: (layer, ei, 0, fi)),
                    pl.BlockSpec((None, 1, d, tf), lambda ei, fi: (layer, ei, 0, fi)),
                    pl.BlockSpec((None, 1, tf, d), lambda ei, fi: (layer, ei, fi, 0))],
        out_specs=[tok(a) for a in xgs],
        scratch_shapes=[pltpu.VMEM((xg.shape[0] * xg.shape[2], d), F32) for xg in xgs],
        compiler_params=_params(("arbitrary", "arbitrary")),
        name=name,
    )(*xgs, *gss, w_gate, w_up, w_down)


def _combine_kernel(pos_ref, y_ref, x_ref, g_ref, lng_ref, lnb_ref, o_ref, *, cap, n_exp):
    pos = pos_ref[0]
    slot = lax.broadcasted_iota(jnp.int32, (1, cap), 1).astype(F32)
    acc = None
    for ei in range(n_exp):
        onehot = jnp.where(pos[:, ei:ei + 1] == slot, 1.0, 0.0).astype(BF16)
        part = _dot(onehot, y_ref[0, ei * cap:(ei + 1) * cap, :])
        acc = part if acc is None else acc + part
    o_ref[0] = _layer_norm(DEEPNORM_ALPHA * x_ref[0] + g_ref[0] * acc, lng_ref[...], lnb_ref[...])


def _combine_call(pos_t, y, x, g2, lng, lnb, cap, name):
    b, s, d = x.shape
    e = pos_t.shape[2]
    tm = min(256, s)
    per_b = g2.shape[0] == b
    mod_map = (lambda bi, i: (bi, 0, 0)) if per_b else (lambda bi, i: (0, 0, 0))
    row = pl.BlockSpec((1, d), lambda bi, i: (0, 0))
    return pl.pallas_call(
        functools.partial(_combine_kernel, cap=cap, n_exp=e),
        out_shape=jax.ShapeDtypeStruct((b, s, d), F32),
        grid=(b, s // tm),
        in_specs=[pl.BlockSpec((1, tm, e), lambda bi, i: (bi, i, 0)),
                  pl.BlockSpec((1, e * cap, d), lambda bi, i: (bi, 0, 0)),
                  pl.BlockSpec((1, tm, d), lambda bi, i: (bi, i, 0)),
                  pl.BlockSpec((1, 1, d), mod_map), row, row],
        out_specs=pl.BlockSpec((1, tm, d), lambda bi, i: (bi, i, 0)),
        compiler_params=_params(("arbitrary", "arbitrary")),
        name=name,
    )(pos_t, y.reshape(b, e * cap, d), x, g2, lng, lnb)


def _ffn_sublayer(streams, mods, lng, lnb, router, w_gate, w_up, w_down, layer):
    router_t = router.T
    staged = []
    for si, (mix, w_out, x, g1, sc2, sh2, g2) in enumerate(streams):
        tag = f"l{layer}s{si}"
        n_tok = x.shape[1]
        cap = EC_FACTOR * n_tok // N_EXPERTS
        x1, hm, aff = _outproj_call(mix, w_out, x, g1, lng[0:1], lnb[0:1], sc2, sh2, router_t, "outproj_" + tag)
        pos = _route_call(aff, cap, "route_" + tag)
        xg, gs = _gather_call(pos, aff, hm, cap, "gather_" + tag)
        staged.append((x1, pos, xg, gs, g2, cap, tag))
    ys = _ffn_call([st[2] for st in staged], [st[3] for st in staged], w_gate, w_up, w_down, layer,
                   f"ffn_l{layer}")
    outs = []
    for (x1, pos, _, _, g2, cap, tag), y in zip(staged, ys):
        outs.append(_combine_call(jnp.swapaxes(pos, 1, 2), y, x1, g2, lng[1:2], lnb[1:2], cap, "combine_" + tag))
    return outs


def kernel(x, c, ctx, c_ctx, ada_w, ada_b, ln_g, ln_b, l0_w_in, l0_w_out, l0_lam_q1, l0_lam_k1, l0_lam_q2,
           l0_lam_k2, l0_subln_g, l0_qnorm_g, l0_knorm_g, l1_w_in, l1_w_out, l1_rpb, l1_pool_w, l1_pool_scale,
           moe_router, moe_w_gate, moe_w_up, moe_w_down):
    b, s, d = x.shape
    n_ctx = ctx.shape[1]

    rows = -(-(b + 1) // 8) * 8
    cc = jnp.zeros((rows, d), F32).at[:b].set(c).at[b].set(c_ctx)
    mod = _ada_call(cc, ada_w, ada_b)

    def mods(i):
        lat = [mod[i, :b, k * d:(k + 1) * d].reshape(b, 1, d) for k in range(6)]
        cx = [mod[i, b:b + 1, k * d:(k + 1) * d].reshape(1, 1, d) for k in range(6)]
        return lat, cx

    rope = _rope_tables(s)
    rope_id = (jnp.ones((n_ctx, LANES), F32), jnp.zeros((n_ctx, LANES), F32), jnp.zeros((n_ctx, LANES), F32))
    gmat = jnp.where((jnp.arange(LANES)[:, None] // HEAD_DIM) == (jnp.arange(LANES)[None, :] // HEAD_DIM),
                     1.0 / HEAD_DIM, 0.0).astype(BF16)
    tile2 = lambda g: jnp.concatenate([g, g]).reshape(1, LANES)

    (sh1, sc1, g1, sh2, sc2, g2), (csh1, csc1, cg1, csh2, csc2, cg2) = mods(0)
    bq0 = 3 * 512
    pair_cols = jnp.concatenate([jnp.arange(HEAD_DIM) + bq0 + hh * HEAD_DIM
                                 for j in range(B_HEADS // 2) for hh in (j, j + B_HEADS // 2)])
    in_perm = jnp.concatenate([jnp.arange(bq0), pair_cols, jnp.arange(bq0 + 512, l0_w_in.shape[1])])
    w_in0 = l0_w_in[:, in_perm].astype(BF16)
    w_out0 = l0_w_out[jnp.concatenate([jnp.arange(512), pair_cols - bq0 + 512])].astype(BF16)
    gains0 = jnp.concatenate([tile2(l0_qnorm_g), tile2(l0_knorm_g)], axis=0)
    blocks0 = ([("rope", 0, Q_SCALE, 0, k * LANES) for k in range(4)]
               + [("rope", 0, 1.0, 0, (4 + k) * LANES) for k in range(4)]
               + [("plain", 0, 1.0, 0, (8 + k) * LANES) for k in range(4)]
               + [("norm", 0, Q_SCALE, 0, (12 + k) * LANES) for k in range(4)]
               + [("norm", 1, 1.0, 0, 16 * LANES), ("plain", 0, 1.0, 0, 17 * LANES)])
    n0 = len(blocks0) * LANES
    qkv = _inproj_call(x, sc1, sh1, w_in0, blocks0, [(n0, BF16)], rope, gains0, gmat,
                       use_rope=True, name="inproj_l0")[0]
    qkv_c = _inproj_call(ctx, csc1, csh1, w_in0, blocks0, [(n0, BF16)], rope_id, gains0, gmat,
                         use_rope=False, name="inproj_l0c")[0]
    lam_init = 0.8 - 0.6 * math.exp(-0.3 * 0)
    lamv = jnp.stack([l0_lam_q1, l0_lam_k1, l0_lam_q2, l0_lam_k2], axis=0)
    sub_g = l0_subln_g.reshape(1, LANES)
    mix = _attn_even_call(qkv, [qkv, qkv_c], lamv, sub_g, lam_init, "attn_l0")
    mix_c = _attn_even_call(qkv_c, [qkv_c], lamv, sub_g, lam_init, "attn_l0c")
    x, ctx = _ffn_sublayer(
        [(mix, w_out0, x, g1, sc2, sh2, g2), (mix_c, w_out0, ctx, cg1, csc2, csh2, cg2)],
        None, ln_g[0], ln_b[0], moe_router[0], moe_w_gate, moe_w_up, moe_w_down, 0)

    (sh1, sc1, g1, sh2, sc2, g2), (csh1, csc1, _, _, _, _) = mods(1)
    cw = C_HEADS * HEAD_DIM
    w_in1 = l1_w_in.astype(BF16)
    blocks1 = ([("plain", 0, Q_SCALE, 0, k * LANES) for k in range(4)]
               + [("plain", 0, 1.0, 0, (4 + k) * LANES) for k in range(8)]
               + [("plain", 0, 1.0, 1, k * LANES) for k in range(4)])
    qkv1, du = _inproj_call(x, sc1, sh1, w_in1, blocks1, [(3 * cw, BF16), (D_GROUPS * D_GROUP_DIM, F32)],
                            rope, gains0, gmat, use_rope=False, name="inproj_l1")
    blocks1c = [("plain", 0, 1.0, 0, k * LANES) for k in range(8)]
    ckv = _inproj_call(ctx, csc1, csh1, w_in1[:, cw:3 * cw], blocks1c, [(2 * cw, BF16)], rope_id, gains0, gmat,
                       use_rope=False, name="inproj_l1c")[0]
    mix1 = _odd_mixer_call(qkv1, ckv, l1_rpb, du, l1_pool_w.astype(BF16), l1_pool_scale.reshape(1, -1))
    (x,) = _ffn_sublayer([(mix1, l1_w_out.astype(BF16), x, g1, sc2, sh2, g2)],
                         None, ln_g[1], ln_b[1], moe_router[1], moe_w_gate, moe_w_up, moe_w_down, 1)
    return x
```

```python
import functools
import math

import numpy as np

import jax
import jax.numpy as jnp
from jax import lax
from jax.experimental import pallas as pl
from jax.experimental.pallas import tpu as pltpu

F32 = jnp.float32
BF16 = jnp.bfloat16

DEPTH = 2
GRID_W = 64
HEAD_DIM = 64
A_HEADS = 4
B_HEADS = 8
B_KV_HEADS = 2
C_HEADS = 8
D_GROUPS = 4
D_GROUP_DIM = 128
POOL_WINDOWS = (2, 4, 8, 16)
NA_ROWS = 8
NA_COLS = 16
N_EXPERTS = 16
EC_FACTOR = 2
ROPE_THETA = 10000.0
LN_EPS = 1e-5
RMS_EPS = 1e-6
DEEPNORM_ALPHA = (2 * DEPTH) ** 0.25
Q_SCALE = HEAD_DIM ** -0.5

LANES = 128
VMEM_LIMIT = 56 * 1024 * 1024

NA_QROWS = 4
NEG_BIG = -1e30

_NN = (((1,), (0,)), ((), ()))
_NT = (((1,), (1,)), ((), ()))


def _dot(a, b, dims=_NN):
    return lax.dot_general(a, b, dims, preferred_element_type=F32)


def _split_bf16(a):
    hi = a.astype(BF16)
    lo = (a - hi.astype(F32)).astype(BF16)
    return hi, lo


def _dot3(a, b, dims=_NN):
    a_hi, a_lo = _split_bf16(a)
    b_hi, b_lo = _split_bf16(b)
    return _dot(a_hi, b_hi, dims) + (_dot(a_hi, b_lo, dims) + _dot(a_lo, b_hi, dims))


def _params(sem):
    return pltpu.CompilerParams(dimension_semantics=sem, vmem_limit_bytes=VMEM_LIMIT)


def _layer_norm(z, g, b):
    mu = jnp.mean(z, axis=-1, keepdims=True)
    zc = z - mu
    var = jnp.mean(zc * zc, axis=-1, keepdims=True)
    return zc * lax.rsqrt(var + LN_EPS) * g + b


def _lane_masks():
    lane = lax.broadcasted_iota(jnp.int32, (1, LANES), 1)
    lo = jnp.where(lane < HEAD_DIM, 1.0, 0.0).astype(F32)
    return lo, 1.0 - lo


def _ada_kernel(c_ref, w_ref, b_ref, o_ref):
    c = c_ref[...]
    s = c * (1.0 / (1.0 + jnp.exp(-c)))
    o_ref[0] = _dot3(s, w_ref[0]) + b_ref[0]


def _ada_call(cc, ada_w, ada_b):
    depth, d, n = ada_w.shape
    rows = cc.shape[0]
    tn = 1536 if n % 1536 == 0 else n
    return pl.pallas_call(
        _ada_kernel,
        out_shape=jax.ShapeDtypeStruct((depth, rows, n), F32),
        grid=(depth, n // tn),
        in_specs=[pl.BlockSpec((rows, d), lambda l, j: (0, 0)),
                  pl.BlockSpec((1, d, tn), lambda l, j: (l, 0, j)),
                  pl.BlockSpec((1, 1, tn), lambda l, j: (l, 0, j))],
        out_specs=pl.BlockSpec((1, rows, tn), lambda l, j: (l, 0, j)),
        compiler_params=_params(("arbitrary", "arbitrary")),
        name="ada_mod",
    )(cc, ada_w, ada_b.reshape(depth, 1, n))


def _inproj_kernel(x_ref, sc_ref, sh_ref, w_ref, cos_ref, sinp_ref, sinm_ref, gn_ref, gmat_ref, *o_refs,
                   blocks, ones_cols, chunk, use_rope):
    h = (x_ref[0] * (1.0 + sc_ref[0]) + sh_ref[0]).astype(BF16)
    n = len(blocks) * LANES
    for oi, oc in ones_cols:
        o_refs[oi][0, :, oc:oc + LANES] = jnp.ones((h.shape[0], LANES), o_refs[oi].dtype)
    for c0 in range(0, n, chunk):
        cw = min(chunk, n - c0)
        acc = _dot(h, w_ref[:, c0:c0 + cw])
        for j in range(cw // LANES):
            kind, gain_row, factor, oi, oc = blocks[(c0 // LANES) + j]
            v = acc[:, j * LANES:(j + 1) * LANES]
            if kind == "norm":
                v2 = v * v
                hi, lo = _split_bf16(v2)
                ms = _dot(hi, gmat_ref[...]) + _dot(lo, gmat_ref[...])
                v = v * lax.rsqrt(ms + RMS_EPS) * gn_ref[gain_row:gain_row + 1, :]
            if kind in ("rope", "norm") and use_rope:
                v = (v * cos_ref[...] + pltpu.roll(v, 16, 1) * sinp_ref[...]
                     + pltpu.roll(v, LANES - 16, 1) * sinm_ref[...])
            if factor != 1.0:
                v = v * factor
            o_refs[oi][0, :, oc:oc + LANES] = v.astype(o_refs[oi].dtype)


def _inproj_call(x, sc, sh, w, blocks, out_defs, rope_tabs, gains, gmat, *, use_rope, name, ones_cols=()):
    b, s, d = x.shape
    n = w.shape[1]
    tm = min(512, s)
    per_b = sc.shape[0] == b
    mod_map = (lambda bi, i: (bi, 0, 0)) if per_b else (lambda bi, i: (0, 0, 0))
    cos, sinp, sinm = rope_tabs
    tab_spec = pl.BlockSpec((tm, LANES), lambda bi, i: (i, 0))
    kern = functools.partial(_inproj_kernel, blocks=tuple(blocks), ones_cols=tuple(ones_cols), chunk=512,
                             use_rope=use_rope)
    return pl.pallas_call(
        kern,
        out_shape=[jax.ShapeDtypeStruct((b, s, nc), dt) for nc, dt in out_defs],
        grid=(b, s // tm),
        in_specs=[pl.BlockSpec((1, tm, d), lambda bi, i: (bi, i, 0)),
                  pl.BlockSpec((1, 1, d), mod_map),
                  pl.BlockSpec((1, 1, d), mod_map),
                  pl.BlockSpec((d, n), lambda bi, i: (0, 0)),
                  tab_spec, tab_spec, tab_spec,
                  pl.BlockSpec(gains.shape, lambda bi, i: (0, 0)),
                  pl.BlockSpec(gmat.shape, lambda bi, i: (0, 0))],
        out_specs=[pl.BlockSpec((1, tm, nc), lambda bi, i: (bi, i, 0)) for nc, _ in out_defs],
        compiler_params=_params(("arbitrary", "arbitrary")),
        name=name,
    )(x, sc, sh, w, cos, sinp, sinm, gains, gmat)


def _rope_tables(s):
    n_freq = HEAD_DIM // 4
    t = jnp.arange(s, dtype=jnp.int32)
    inv = ROPE_THETA ** (-jnp.arange(n_freq, dtype=F32) / n_freq)
    ang_r = (t // GRID_W).astype(F32)[:, None] * inv
    ang_c = (t % GRID_W).astype(F32)[:, None] * inv
    ang = jnp.concatenate([ang_r, ang_r, ang_c, ang_c] * (LANES // HEAD_DIM), axis=-1)
    first = (jnp.arange(LANES) % 32) < 16
    cos, sin = jnp.cos(ang), jnp.sin(ang)
    return cos, jnp.where(first, 0.0, sin), jnp.where(first, -sin, 0.0)


EV_AQ, EV_AK, EV_AV = 0, 512, 1024
EV_BQ = EV_AV + A_HEADS * 2 * LANES
EV_BK = EV_BQ + (B_HEADS // 2) * LANES
EV_BV = EV_BK + LANES
EV_WIDTH = EV_BV + 2 * LANES


def _attn_even_kernel(q_ref, lam_ref, sg_ref, *refs, n_kv, tq, lam_init):
    kv_refs, o_ref = refs[:n_kv], refs[n_kv]
    lo, hi = _lane_masks()
    lo_b, hi_b = lo.astype(BF16), hi.astype(BF16)
    lv = lam_ref[...]
    lam = (jnp.exp(jnp.sum(lv[0:1] * lv[1:2], axis=-1, keepdims=True))
           - jnp.exp(jnp.sum(lv[2:3] * lv[3:4], axis=-1, keepdims=True)) + lam_init)

    def attend(q, kcol, vcol):
        qq = jnp.concatenate([q * lo_b, q * hi_b], axis=0)
        ss = [_dot(qq, kv[0, :, kcol:kcol + LANES], _NT) for kv in kv_refs]
        m = ss[0].max(axis=-1, keepdims=True)
        for s in ss[1:]:
            m = jnp.maximum(m, s.max(axis=-1, keepdims=True))
        acc = None
        for s, kv in zip(ss, kv_refs):
            part = _dot(jnp.exp2(s - m).astype(BF16), kv[0, :, vcol:vcol + 2 * LANES])
            acc = part if acc is None else acc + part
        return acc[:, :LANES] * (1.0 / acc[:, LANES:])

    for h in range(A_HEADS):
        on = attend(q_ref[0, :, EV_AQ + h * LANES:EV_AQ + (h + 1) * LANES],
                    EV_AK + h * LANES, EV_AV + h * 2 * LANES)
        o = on[:tq] - lam * on[tq:]
        ms = jnp.mean(o * o, axis=-1, keepdims=True)
        o = o * lax.rsqrt(ms + RMS_EPS) * sg_ref[...] * (1.0 - lam_init)
        o_ref[0, :, h * LANES:(h + 1) * LANES] = o.astype(o_ref.dtype)

    for j in range(B_HEADS // 2):
        on = attend(q_ref[0, :, EV_BQ + j * LANES:EV_BQ + (j + 1) * LANES], EV_BK, EV_BV)
        o = on[:tq] * lo + on[tq:] * hi
        o_ref[0, :, 512 + j * LANES:512 + (j + 1) * LANES] = o.astype(o_ref.dtype)


def _attn_even_call(q_arr, kv_arrs, lamv, subln_g, lam_init, name):
    b, sq, n = q_arr.shape
    tq = min(256, sq)
    kern = functools.partial(_attn_even_kernel, n_kv=len(kv_arrs), tq=tq, lam_init=lam_init)
    return pl.pallas_call(
        kern,
        out_shape=jax.ShapeDtypeStruct((b, sq, 1024), BF16),
        grid=(b, sq // tq),
        in_specs=[pl.BlockSpec((1, tq, n), lambda bi, i: (bi, i, 0)),
                  pl.BlockSpec(lamv.shape, lambda bi, i: (0, 0)),
                  pl.BlockSpec(subln_g.shape, lambda bi, i: (0, 0))]
                 + [pl.BlockSpec((1,) + a.shape[1:], lambda bi, i: (bi, 0, 0)) for a in kv_arrs],
        out_specs=pl.BlockSpec((1, tq, 1024), lambda bi, i: (bi, i, 0)),
        compiler_params=_params(("arbitrary", "arbitrary")),
        name=name,
    )(q_arr, lamv, subln_g, *kv_arrs)


def _odd_mixer_kernel(qkv_ref, ckv_ref, bias_ref, du_ref, band_ref, edge_ref, pw_ref, ps_ref, o_ref, *,
                      tq, slab_rows, n_rows, n_blk, seq):
    i = pl.program_id(1)
    lo, hi = _lane_masks()
    lo_b, hi_b = lo.astype(BF16), hi.astype(BF16)
    t0 = pl.multiple_of(i * tq, tq)
    base = jnp.clip(i * NA_QROWS - NA_ROWS // 2, 0, n_rows - slab_rows)
    k0 = pl.multiple_of(base * GRID_W, GRID_W)
    nk = slab_rows * GRID_W
    cw = C_HEADS * HEAD_DIM

    for j in range(C_HEADS // 2):
        cs = slice(j * LANES, (j + 1) * LANES)
        q = qkv_ref[0, pl.ds(t0, tq), cs]
        qq = jnp.concatenate([q * lo_b, q * hi_b], axis=0)
        kl = qkv_ref[0, pl.ds(k0, nk), cw + j * LANES:cw + (j + 1) * LANES]
        vl = qkv_ref[0, pl.ds(k0, nk), 2 * cw + 2 * j * LANES:2 * cw + 2 * (j + 1) * LANES]
        kc = ckv_ref[0, :, cs]
        vc = ckv_ref[0, :, cw + 2 * j * LANES:cw + 2 * (j + 1) * LANES]
        bias = jnp.concatenate([bias_ref[2 * j], bias_ref[2 * j + 1]], axis=0)
        s_l = _dot(qq, kl, _NT) + bias
        s_c = _dot(qq, kc, _NT)
        m = jnp.maximum(s_l.max(axis=-1, keepdims=True), s_c.max(axis=-1, keepdims=True))
        acc = _dot(jnp.exp2(s_l - m).astype(BF16), vl) + _dot(jnp.exp2(s_c - m).astype(BF16), vc)
        on = acc[:, :LANES] * (1.0 / acc[:, LANES:])
        o = on[:tq] * lo + on[tq:] * hi
        o_ref[0, :, cs] = o.astype(o_ref.dtype)

    tprev = pl.multiple_of(jnp.maximum(i - 1, 0) * tq, tq)
    tnext = pl.multiple_of(jnp.minimum(i + 1, n_blk - 1) * tq, tq)
    has_prev = jnp.where(i > 0, 1.0, 0.0).astype(F32)
    has_next = jnp.where(i < n_blk - 1, 1.0, 0.0).astype(F32)
    tpos = t0 + lax.broadcasted_iota(jnp.int32, (tq, 1), 0)
    for g in range(D_GROUPS):
        half = POOL_WINDOWS[g] // 2
        gs = slice(g * D_GROUP_DIM, (g + 1) * D_GROUP_DIM)
        cur = du_ref[0, pl.ds(t0, tq), gs]
        prv = du_ref[0, pl.ds(tprev + (tq - POOL_EDGE), POOL_EDGE), gs]
        nxt = du_ref[0, pl.ds(tnext, POOL_EDGE), gs]

        def band_sum(band, u):
            u_hi, u_lo = _split_bf16(u)
            return _dot(band, u_hi) + _dot(band, u_lo)

        wsum = band_sum(band_ref[g], cur)
        wsum = jnp.concatenate([wsum[:POOL_EDGE] + has_prev * band_sum(edge_ref[g, 0], prv),
                                wsum[POOL_EDGE:tq - POOL_EDGE],
                                wsum[tq - POOL_EDGE:] + has_next * band_sum(edge_ref[g, 1], nxt)], axis=0)
        cnt = (jnp.minimum(tpos + half, seq) - jnp.maximum(tpos - half, 0)).astype(F32)
        pooled = wsum * (1.0 / cnt) - cur
        od = _dot(pooled.astype(BF16), pw_ref[g]) * ps_ref[:, gs]
        o_ref[0, :, cw + g * D_GROUP_DIM:cw + (g + 1) * D_GROUP_DIM] = od.astype(o_ref.dtype)


def _na_geometry(n_rows):
    kh = min(NA_ROWS, n_rows)
    slab = min(n_rows, NA_QROWS + kh)
    n_blk = n_rows // NA_QROWS
    bases = [min(max(i * NA_QROWS - NA_ROWS // 2, 0), n_rows - slab) for i in range(n_blk)]
    sigs, type_of = [], []
    for i in range(n_blk):
        sig = tuple((min(max(i * NA_QROWS + r - kh // 2, 0), n_rows - kh) - bases[i],
                     i * NA_QROWS + r - bases[i]) for r in range(NA_QROWS))
        if sig not in sigs:
            sigs.append(sig)
        type_of.append(sigs.index(sig))
    return kh, slab, n_blk, sigs, type_of


def _na_bias_table(rpb, n_rows):
    kh, slab, _, sigs, _ = _na_geometry(n_rows)
    kw = NA_COLS
    n_dc = 2 * NA_COLS - 1
    w = np.arange(GRID_W)
    cstart = np.clip(w - kw // 2, 0, GRID_W - kw)
    col_ok = (w[None, :] >= cstart[:, None]) & (w[None, :] < cstart[:, None] + kw)
    dc = np.clip(w[None, :] - w[:, None] + (NA_COLS - 1), 0, n_dc - 1)
    n_dr = 2 * NA_ROWS - 1
    heads = rpb.shape[0]
    onehot = ((dc[None] == np.arange(n_dc)[:, None, None]) & col_ok[None]).astype(np.float32)
    toep = jnp.einsum("hrd,dwj->hrwj", rpb, jnp.asarray(onehot), precision=lax.Precision.HIGHEST)
    toep = jnp.where(jnp.asarray(col_ok), toep, NEG_BIG)
    toep = jnp.concatenate([toep, jnp.full((heads, 1, GRID_W, GRID_W), NEG_BIG, F32)], axis=1)
    dr_idx = np.full((len(sigs), NA_QROWS, slab), n_dr, np.int32)
    for t, sig in enumerate(sigs):
        for r, (rs_rel, qr_rel) in enumerate(sig):
            for m in range(slab):
                if rs_rel <= m < rs_rel + kh:
                    dr_idx[t, r, m] = m - qr_rel + (NA_ROWS - 1)
    big = jnp.take(toep, jnp.asarray(dr_idx.reshape(-1)), axis=1)
    big = big.reshape(heads, len(sigs), NA_QROWS, slab, GRID_W, GRID_W).transpose(0, 1, 2, 4, 3, 5)
    return big.reshape(heads, len(sigs), NA_QROWS * GRID_W, slab * GRID_W)


POOL_EDGE = 16


def _pool_bands(tq):
    t = jnp.arange(tq, dtype=jnp.int32)[:, None]
    sidx = jnp.arange(tq, dtype=jnp.int32)[None, :]
    main, edge = [], []
    for wdw in POOL_WINDOWS:
        half = wdw // 2
        full = [((sidx + (m - 1) * tq >= t - half) & (sidx + (m - 1) * tq < t + half)).astype(BF16)
                for m in range(3)]
        main.append(full[1])
        edge.append(jnp.stack([full[0][:POOL_EDGE, tq - POOL_EDGE:], full[2][tq - POOL_EDGE:, :POOL_EDGE]]))
    return jnp.stack(main), jnp.stack(edge)


def _odd_mixer_call(qkv, ckv, rpb, du, pool_w, pool_scale):
    b, s, n = qkv.shape
    n_rows = s // GRID_W
    _, slab, n_blk, _, type_of = _na_geometry(n_rows)
    tq = NA_QROWS * GRID_W
    bias = _na_bias_table(rpb, n_rows)
    bands, edges = _pool_bands(tq)

    def bias_map(bi, i):
        t = jnp.int32(type_of[-1])
        for blk in range(n_blk - 2, -1, -1):
            t = jnp.where(i == blk, jnp.int32(type_of[blk]), t)
        return (0, t, 0, 0)

    kern = functools.partial(_odd_mixer_kernel, tq=tq, slab_rows=slab, n_rows=n_rows, n_blk=n_blk, seq=s)
    return pl.pallas_call(
        kern,
        out_shape=jax.ShapeDtypeStruct((b, s, 1024), BF16),
        grid=(b, n_blk),
        in_specs=[pl.BlockSpec((1, s, n), lambda bi, i: (bi, 0, 0)),
                  pl.BlockSpec((1,) + ckv.shape[1:], lambda bi, i: (bi, 0, 0)),
                  pl.BlockSpec((bias.shape[0], None) + bias.shape[2:], bias_map),
                  pl.BlockSpec((1, s, du.shape[2]), lambda bi, i: (bi, 0, 0)),
                  pl.BlockSpec(bands.shape, lambda bi, i: (0, 0, 0)),
                  pl.BlockSpec(edges.shape, lambda bi, i: (0, 0, 0, 0)),
                  pl.BlockSpec(pool_w.shape, lambda bi, i: (0, 0, 0)),
                  pl.BlockSpec(pool_scale.shape, lambda bi, i: (0, 0))],
        out_specs=pl.BlockSpec((1, tq, 1024), lambda bi, i: (bi, i, 0)),
        compiler_params=_params(("arbitrary", "arbitrary")),
        name="odd_mixer",
    )(qkv, ckv, bias, du, bands, edges, pool_w, pool_scale)


OUTPROJ_SUB = 256


def _outproj_kernel(mix_ref, w_ref, x_ref, g1_ref, lng_ref, lnb_ref, sc_ref, sh_ref, r2_ref, rh_ref,
                    x1_ref, hm_ref, aff_ref, *, n_exp):
    tm = x_ref.shape[1]
    for r0 in range(0, tm, OUTPROJ_SUB):
        rows = slice(r0, r0 + OUTPROJ_SUB)
        y = _dot(mix_ref[0, rows, :], w_ref[...])
        x1 = _layer_norm(DEEPNORM_ALPHA * x_ref[0, rows, :] + g1_ref[0] * y, lng_ref[...], lnb_ref[...])
        x1_ref[0, rows, :] = x1
        hm = x1 * (1.0 + sc_ref[0]) + sh_ref[0]
        hm_hi = hm.astype(BF16)
        hm_ref[0, rows, :] = hm_hi
        hm_lo = (hm - hm_hi.astype(F32)).astype(BF16)
        part = _dot(hm_hi, r2_ref[...])
        logits = part[:, :n_exp] + part[:, n_exp:] + _dot(hm_lo, rh_ref[...])
        ex = jnp.exp(logits - logits.max(axis=-1, keepdims=True))
        aff_ref[0, rows, :] = ex * (1.0 / ex.sum(axis=-1, keepdims=True))


def _outproj_call(mix, w, x, g1, lng, lnb, sc2, sh2, router, name):
    b, s, d = x.shape
    dm = mix.shape[2]
    e = router.shape[1]
    tm = min(2 * OUTPROJ_SUB, s)
    r_hi = router.astype(BF16)
    r_lo = (router - r_hi.astype(F32)).astype(BF16)
    r2 = jnp.concatenate([r_hi, r_lo], axis=1)
    per_b = g1.shape[0] == b
    mod_map = (lambda bi, i: (bi, 0, 0)) if per_b else (lambda bi, i: (0, 0, 0))
    vec = pl.BlockSpec((1, 1, d), mod_map)
    row = pl.BlockSpec((1, d), lambda bi, i: (0, 0))
    tile = lambda n: pl.BlockSpec((1, tm, n), lambda bi, i: (bi, i, 0))
    return pl.pallas_call(
        functools.partial(_outproj_kernel, n_exp=e),
        out_shape=[jax.ShapeDtypeStruct((b, s, d), F32),
                   jax.ShapeDtypeStruct((b, s, d), BF16),
                   jax.ShapeDtypeStruct((b, s, e), F32)],
        grid=(b, s // tm),
        in_specs=[tile(dm),
                  pl.BlockSpec((dm, d), lambda bi, i: (0, 0)),
                  tile(d),
                  vec, row, row, vec, vec,
                  pl.BlockSpec((d, 2 * e), lambda bi, i: (0, 0)),
                  pl.BlockSpec((d, e), lambda bi, i: (0, 0))],
        out_specs=[tile(d), tile(d), tile(e)],
        compiler_params=_params(("arbitrary", "arbitrary")),
        name=name,
    )(mix, w, x, g1, lng, lnb, sc2, sh2, r2, r_hi)


def _lane_cumsum(m):
    rows, s = m.shape
    r_i = lax.broadcasted_iota(jnp.int32, (LANES, LANES), 0)
    c_i = lax.broadcasted_iota(jnp.int32, (LANES, LANES), 1)
    tri = jnp.where(r_i <= c_i, 1.0, 0.0).astype(BF16)
    carry = jnp.zeros((rows, 1), F32)
    out = []
    for c in range(s // LANES):
        blk = m[:, c * LANES:(c + 1) * LANES]
        out.append(_dot(blk.astype(BF16), tri) + carry)
        carry = carry + blk.sum(axis=-1, keepdims=True)
    return jnp.concatenate(out, axis=-1)


def _route_kernel(aff_ref, pos_ref, *, cap):
    a = aff_ref[...]
    bits = pltpu.bitcast(a, jnp.int32)
    thr = jnp.zeros((a.shape[0], 1), jnp.int32)
    for bit in range(30, -1, -1):
        cand = thr | jnp.int32(1 << bit)
        cnt = jnp.where(bits >= cand, 1.0, 0.0).sum(axis=-1, keepdims=True)
        thr = jnp.where(cnt >= cap, cand, thr)
    gt = jnp.where(bits > thr, 1.0, 0.0)
    eq = jnp.where(bits == thr, 1.0, 0.0)
    need = cap - gt.sum(axis=-1, keepdims=True)
    sel = gt + eq * jnp.where(_lane_cumsum(eq) <= need, 1.0, 0.0)
    pos_ref[...] = jnp.where(sel > 0.5, _lane_cumsum(sel) - 1.0, -1.0)


def _route_call(aff, cap, name):
    b, e, s = aff.shape
    full = pl.BlockSpec((b * e, s), lambda i: (0, 0))
    return pl.pallas_call(
        functools.partial(_route_kernel, cap=cap),
        out_shape=jax.ShapeDtypeStruct((b * e, s), F32),
        grid=(1,),
        in_specs=[full],
        out_specs=full,
        compiler_params=_params(("arbitrary",)),
        name=name,
    )(aff.reshape(b * e, s)).reshape(b, e, s)


def _gather_kernel(pos_ref, aff_ref, hm_ref, xg_ref, gs_ref, *, cap, n_grp):
    slot = lax.broadcasted_iota(jnp.int32, (cap, 1), 0).astype(F32)
    for g in range(n_grp):
        hit = pos_ref[0, 0, g:g + 1, :] == slot
        xg_ref[0, g] = _dot(jnp.where(hit, 1.0, 0.0).astype(BF16), hm_ref[0]).astype(xg_ref.dtype)
        gs_ref[0, g] = jnp.where(hit, aff_ref[0, 0, g:g + 1, :], 0.0).sum(axis=-1, keepdims=True)


def _gather_call(pos, aff, hm, cap, name):
    b, e, s = pos.shape
    d = hm.shape[2]
    n_grp = max(1, min(e, 1024 // cap))
    while e % n_grp:
        n_grp -= 1
    rows = pl.BlockSpec((1, 1, n_grp, s), lambda bi, gi: (bi, gi, 0, 0))
    return pl.pallas_call(
        functools.partial(_gather_kernel, cap=cap, n_grp=n_grp),
        out_shape=[jax.ShapeDtypeStruct((b, e, cap, d), BF16),
                   jax.ShapeDtypeStruct((b, e, cap, 1), F32)],
        grid=(b, e // n_grp),
        in_specs=[rows, rows, pl.BlockSpec((1, s, d), lambda bi, gi: (bi, 0, 0))],
        out_specs=[pl.BlockSpec((1, n_grp, cap, d), lambda bi, gi: (bi, gi, 0, 0)),
                   pl.BlockSpec((1, n_grp, cap, 1), lambda bi, gi: (bi, gi, 0, 0))],
        compiler_params=_params(("arbitrary", "arbitrary")),
        name=name,
    )(pos.reshape(b, e // n_grp, n_grp, s), aff.reshape(b, e // n_grp, n_grp, s), hm)


def _ffn_kernel(*refs, n_grp, row_chunks):
    xg_refs = refs[:n_grp]
    gs_refs = refs[n_grp:2 * n_grp]
    wg_ref, wu_ref, wd_ref = refs[2 * n_grp:2 * n_grp + 3]
    y_refs = refs[2 * n_grp + 3:3 * n_grp + 3]
    acc_refs = refs[3 * n_grp + 3:]
    f = pl.program_id(1)
    last = pl.num_programs(1) - 1
    wg = wg_ref[0].astype(BF16)
    wu = wu_ref[0].astype(BF16)
    wd = wd_ref[0].astype(BF16)

    @pl.when(f == 0)
    def _():
        for acc_ref in acc_refs:
            acc_ref[...] = jnp.zeros(acc_ref.shape, F32)

    for xg_ref, acc_ref, nb in zip(xg_refs, acc_refs, row_chunks):
        bt, _, cap, d = xg_ref.shape
        for b0 in range(0, bt, nb):
            rows = nb * cap
            r0 = b0 * cap
            x = xg_ref[b0:b0 + nb, 0].reshape(rows, d)
            hg = _dot(x, wg)
            hu = _dot(x, wu)
            hid = (hg * (1.0 / (1.0 + jnp.exp(-hg))) * hu).astype(BF16)
            acc_ref[r0:r0 + rows, :] += _dot(hid, wd)

    @pl.when(f == last)
    def _():
        for xg_ref, gs_ref, y_ref, acc_ref in zip(xg_refs, gs_refs, y_refs, acc_refs):
            bt, _, cap, d = xg_ref.shape
            gate = gs_ref[:, 0].reshape(bt * cap, 1)
            y_ref[:, 0] = (acc_ref[...] * gate).reshape(bt, cap, d).astype(y_ref.dtype)


def _ffn_call(xgs, gss, w_gate, w_up, w_down, layer, name):
    _, e, d, ff = w_gate.shape
    tf = 512 if ff % 512 == 0 else ff
    n_grp = len(xgs)
    row_chunks = []
    for xg in xgs:
        bt, _, cap, _ = xg.shape
        nb = max(1, min(bt, 512 // cap))
        while bt % nb:
            nb -= 1
        row_chunks.append(nb)
    tok = lambda a: pl.BlockSpec((a.shape[0], 1) + a.shape[2:], lambda ei, fi: (0, ei, 0, 0))
    kern = functools.partial(_ffn_kernel, n_grp=n_grp, row_chunks=tuple(row_chunks))
    return pl.pallas_call(
        kern,
        out_shape=[jax.ShapeDtypeStruct(xg.shape, BF16) for xg in xgs],
        grid=(e, ff // tf),
        in_specs=[tok(a) for a in xgs] + [tok(a) for a in gss]
                 + [pl.BlockSpec((None, 1, d, tf), lambda ei, fi: (layer, ei, 0, fi)),
                    pl.BlockSpec((None, 1, d, tf), lambda ei, fi: (layer, ei, 0, fi)),
                    pl.BlockSpec((None, 1, tf, d), lambda ei, fi: (layer, ei, fi, 0))],
        out_specs=[tok(a) for a in xgs],
        scratch_shapes=[pltpu.VMEM((xg.shape[0] * xg.shape[2], d), F32) for xg in xgs],
        compiler_params=_params(("arbitrary", "arbitrary")),
        name=name,
    )(*xgs, *gss, w_gate, w_up, w_down)


def _combine_kernel(pos_ref, y_ref, x_ref, g_ref, lng_ref, lnb_ref, o_ref, *, cap, n_exp):
    pos = pos_ref[0]
    slot = lax.broadcasted_iota(jnp.int32, (1, cap), 1).astype(F32)
    acc = None
    for ei in range(n_exp):
        onehot = jnp.where(pos[:, ei:ei + 1] == slot, 1.0, 0.0).astype(BF16)
        part = _dot(onehot, y_ref[0, ei * cap:(ei + 1) * cap, :])
        acc = part if acc is None else acc + part
    o_ref[0] = _layer_norm(DEEPNORM_ALPHA * x_ref[0] + g_ref[0] * acc, lng_ref[...], lnb_ref[...])


def _combine_call(pos_t, y, x, g2, lng, lnb, cap, name):
    b, s, d = x.shape
    e = pos_t.shape[2]
    tm = min(256, s)
    per_b = g2.shape[0] == b
    mod_map = (lambda bi, i: (bi, 0, 0)) if per_b else (lambda bi, i: (0, 0, 0))
    row = pl.BlockSpec((1, d), lambda bi, i: (0, 0))
    return pl.pallas_call(
        functools.partial(_combine_kernel, cap=cap, n_exp=e),
        out_shape=jax.ShapeDtypeStruct((b, s, d), F32),
        grid=(b, s // tm),
        in_specs=[pl.BlockSpec((1, tm, e), lambda bi, i: (bi, i, 0)),
                  pl.BlockSpec((1, e * cap, d), lambda bi, i: (bi, 0, 0)),
                  pl.BlockSpec((1, tm, d), lambda bi, i: (bi, i, 0)),
                  pl.BlockSpec((1, 1, d), mod_map), row, row],
        out_specs=pl.BlockSpec((1, tm, d), lambda bi, i: (bi, i, 0)),
        compiler_params=_params(("arbitrary", "arbitrary")),
        name=name,
    )(pos_t, y.reshape(b, e * cap, d), x, g2, lng, lnb)


def _ffn_sublayer(streams, mods, lng, lnb, router, w_gate, w_up, w_down, layer):
    staged = []
    for si, (mix, w_out, x, g1, sc2, sh2, g2) in enumerate(streams):
        tag = f"l{layer}s{si}"
        n_tok = x.shape[1]
        cap = EC_FACTOR * n_tok // N_EXPERTS
        x1, hm, aff_t = _outproj_call(mix, w_out, x, g1, lng[0:1], lnb[0:1], sc2, sh2, router, "outproj_" + tag)
        aff = jnp.swapaxes(aff_t, 1, 2)
        pos = _route_call(aff, cap, "route_" + tag)
        xg, gs = _gather_call(pos, aff, hm, cap, "gather_" + tag)
        staged.append((x1, pos, xg, gs, g2, cap, tag))
    ys = _ffn_call([st[2] for st in staged], [st[3] for st in staged], w_gate, w_up, w_down, layer,
                   f"ffn_l{layer}")
    outs = []
    for (x1, pos, _, _, g2, cap, tag), y in zip(staged, ys):
        outs.append(_combine_call(jnp.swapaxes(pos, 1, 2), y, x1, g2, lng[1:2], lnb[1:2], cap, "combine_" + tag))
    return outs


def kernel(x, c, ctx, c_ctx, ada_w, ada_b, ln_g, ln_b, l0_w_in, l0_w_out, l0_lam_q1, l0_lam_k1, l0_lam_q2,
           l0_lam_k2, l0_subln_g, l0_qnorm_g, l0_knorm_g, l1_w_in, l1_w_out, l1_rpb, l1_pool_w, l1_pool_scale,
           moe_router, moe_w_gate, moe_w_up, moe_w_down):
    b, s, d = x.shape
    n_ctx = ctx.shape[1]

    rows = -(-(b + 1) // 8) * 8
    cc = jnp.zeros((rows, d), F32).at[:b].set(c).at[b].set(c_ctx)
    mod = _ada_call(cc, ada_w, ada_b)

    def mods(i):
        lat = [mod[i, :b, k * d:(k + 1) * d].reshape(b, 1, d) for k in range(6)]
        cx = [mod[i, b:b + 1, k * d:(k + 1) * d].reshape(1, 1, d) for k in range(6)]
        return lat, cx

    rope = _rope_tables(s)
    rope_id = (jnp.ones((n_ctx, LANES), F32), jnp.zeros((n_ctx, LANES), F32), jnp.zeros((n_ctx, LANES), F32))
    gmat = jnp.where((jnp.arange(LANES)[:, None] // HEAD_DIM) == (jnp.arange(LANES)[None, :] // HEAD_DIM),
                     1.0 / HEAD_DIM, 0.0).astype(BF16)
    tile2 = lambda g: jnp.concatenate([g, g]).reshape(1, LANES)

    (sh1, sc1, g1, sh2, sc2, g2), (csh1, csc1, cg1, csh2, csc2, cg2) = mods(0)
    bq0 = 3 * 512
    pair_cols = jnp.concatenate([jnp.arange(HEAD_DIM) + bq0 + hh * HEAD_DIM
                                 for j in range(B_HEADS // 2) for hh in (j, j + B_HEADS // 2)])
    in_perm = jnp.concatenate([jnp.arange(bq0), pair_cols, jnp.arange(bq0 + 512, l0_w_in.shape[1])])
    w_in0 = l0_w_in[:, in_perm].astype(BF16)
    w_out0 = l0_w_out[jnp.concatenate([jnp.arange(512), pair_cols - bq0 + 512])].astype(BF16)
    gains0 = jnp.concatenate([tile2(l0_qnorm_g), tile2(l0_knorm_g)], axis=0)
    q_exp2 = Q_SCALE * math.log2(math.e)
    blocks0 = ([("rope", 0, q_exp2, 0, EV_AQ + k * LANES) for k in range(4)]
               + [("rope", 0, 1.0, 0, EV_AK + k * LANES) for k in range(4)]
               + [("plain", 0, 1.0, 0, EV_AV + k * 2 * LANES) for k in range(4)]
               + [("norm", 0, q_exp2, 0, EV_BQ + k * LANES) for k in range(4)]
               + [("norm", 1, 1.0, 0, EV_BK), ("plain", 0, 1.0, 0, EV_BV)])
    ones0 = [(0, EV_AV + (2 * k + 1) * LANES) for k in range(4)] + [(0, EV_BV + LANES)]
    qkv = _inproj_call(x, sc1, sh1, w_in0, blocks0, [(EV_WIDTH, BF16)], rope, gains0, gmat,
                       use_rope=True, name="inproj_l0", ones_cols=ones0)[0]
    qkv_c = _inproj_call(ctx, csc1, csh1, w_in0, blocks0, [(EV_WIDTH, BF16)], rope_id, gains0, gmat,
                         use_rope=False, name="inproj_l0c", ones_cols=ones0)[0]
    lam_init = 0.8 - 0.6 * math.exp(-0.3 * 0)
    lamv = jnp.stack([l0_lam_q1, l0_lam_k1, l0_lam_q2, l0_lam_k2], axis=0)
    sub_g = l0_subln_g.reshape(1, LANES)
    mix = _attn_even_call(qkv, [qkv, qkv_c], lamv, sub_g, lam_init, "attn_l0")
    mix_c = _attn_even_call(qkv_c, [qkv_c], lamv, sub_g, lam_init, "attn_l0c")
    x, ctx = _ffn_sublayer(
        [(mix, w_out0, x, g1, sc2, sh2, g2), (mix_c, w_out0, ctx, cg1, csc2, csh2, cg2)],
        None, ln_g[0], ln_b[0], moe_router[0], moe_w_gate, moe_w_up, moe_w_down, 0)

    (sh1, sc1, g1, sh2, sc2, g2), (csh1, csc1, _, _, _, _) = mods(1)
    cw = C_HEADS * HEAD_DIM
    w_in1 = l1_w_in.astype(BF16)
    blocks1 = ([("plain", 0, q_exp2, 0, k * LANES) for k in range(4)]
               + [("plain", 0, 1.0, 0, cw + k * LANES) for k in range(4)]
               + [("plain", 0, 1.0, 0, 2 * cw + 2 * k * LANES) for k in range(4)]
               + [("plain", 0, 1.0, 1, k * LANES) for k in range(4)])
    ones1 = [(0, 2 * cw + (2 * k + 1) * LANES) for k in range(4)]
    qkv1, du = _inproj_call(x, sc1, sh1, w_in1, blocks1, [(4 * cw, BF16), (D_GROUPS * D_GROUP_DIM, F32)],
                            rope, gains0, gmat, use_rope=False, name="inproj_l1", ones_cols=ones1)
    blocks1c = ([("plain", 0, 1.0, 0, k * LANES) for k in range(4)]
                + [("plain", 0, 1.0, 0, cw + 2 * k * LANES) for k in range(4)])
    ones1c = [(0, cw + (2 * k + 1) * LANES) for k in range(4)]
    ckv = _inproj_call(ctx, csc1, csh1, w_in1[:, cw:3 * cw], blocks1c, [(3 * cw, BF16)], rope_id, gains0, gmat,
                       use_rope=False, name="inproj_l1c", ones_cols=ones1c)[0]
    mix1 = _odd_mixer_call(qkv1, ckv, l1_rpb * math.log2(math.e), du, l1_pool_w.astype(BF16),
                           l1_pool_scale.reshape(1, -1))
    (x,) = _ffn_sublayer([(mix1, l1_w_out.astype(BF16), x, g1, sc2, sh2, g2)],
                         None, ln_g[1], ln_b[1], moe_router[1], moe_w_gate, moe_w_up, moe_w_down, 1)
    return x
```

```python
import functools
import math

import numpy as np

import jax
import jax.numpy as jnp
from jax import lax
from jax.experimental import pallas as pl
from jax.experimental.pallas import tpu as pltpu

F32 = jnp.float32
BF16 = jnp.bfloat16

DEPTH = 2
GRID_W = 64
HEAD_DIM = 64
A_HEADS = 4
B_HEADS = 8
B_KV_HEADS = 2
C_HEADS = 8
D_GROUPS = 4
D_GROUP_DIM = 128
POOL_WINDOWS = (2, 4, 8, 16)
NA_ROWS = 8
NA_COLS = 16
N_EXPERTS = 16
EC_FACTOR = 2
ROPE_THETA = 10000.0
LN_EPS = 1e-5
RMS_EPS = 1e-6
DEEPNORM_ALPHA = (2 * DEPTH) ** 0.25
Q_SCALE = HEAD_DIM ** -0.5

LANES = 128
VMEM_LIMIT = 56 * 1024 * 1024

NA_QROWS = 4
NEG_BIG = -1e30

_NN = (((1,), (0,)), ((), ()))
_NT = (((1,), (1,)), ((), ()))


def _dot(a, b, dims=_NN):
    return lax.dot_general(a, b, dims, preferred_element_type=F32)


def _split_bf16(a):
    hi = a.astype(BF16)
    lo = (a - hi.astype(F32)).astype(BF16)
    return hi, lo


def _dot3(a, b, dims=_NN):
    a_hi, a_lo = _split_bf16(a)
    b_hi, b_lo = _split_bf16(b)
    return _dot(a_hi, b_hi, dims) + (_dot(a_hi, b_lo, dims) + _dot(a_lo, b_hi, dims))


def _params(sem):
    return pltpu.CompilerParams(dimension_semantics=sem, vmem_limit_bytes=VMEM_LIMIT)


def _layer_norm(z, g, b):
    mu = jnp.mean(z, axis=-1, keepdims=True)
    zc = z - mu
    var = jnp.mean(zc * zc, axis=-1, keepdims=True)
    return zc * lax.rsqrt(var + LN_EPS) * g + b


def _lane_masks():
    lane = lax.broadcasted_iota(jnp.int32, (1, LANES), 1)
    lo = jnp.where(lane < HEAD_DIM, 1.0, 0.0).astype(F32)
    return lo, 1.0 - lo


def _ada_kernel(c_ref, w_ref, b_ref, o_ref):
    c = c_ref[...]
    s = c * (1.0 / (1.0 + jnp.exp(-c)))
    o_ref[0] = _dot3(s, w_ref[0]) + b_ref[0]


def _ada_call(cc, ada_w, ada_b):
    depth, d, n = ada_w.shape
    rows = cc.shape[0]
    tn = 1536 if n % 1536 == 0 else n
    return pl.pallas_call(
        _ada_kernel,
        out_shape=jax.ShapeDtypeStruct((depth, rows, n), F32),
        grid=(depth, n // tn),
        in_specs=[pl.BlockSpec((rows, d), lambda l, j: (0, 0)),
                  pl.BlockSpec((1, d, tn), lambda l, j: (l, 0, j)),
                  pl.BlockSpec((1, 1, tn), lambda l, j: (l, 0, j))],
        out_specs=pl.BlockSpec((1, rows, tn), lambda l, j: (l, 0, j)),
        compiler_params=_params(("arbitrary", "arbitrary")),
        name="ada_mod",
    )(cc, ada_w, ada_b.reshape(depth, 1, n))


def _inproj_kernel(x_ref, sc_ref, sh_ref, w_ref, cos_ref, sinp_ref, sinm_ref, gn_ref, gmat_ref, *o_refs,
                   blocks, ones_cols, chunk, use_rope):
    h = (x_ref[0] * (1.0 + sc_ref[0]) + sh_ref[0]).astype(BF16)
    n = len(blocks) * LANES
    for oi, oc in ones_cols:
        o_refs[oi][0, :, oc:oc + LANES] = jnp.ones((h.shape[0], LANES), o_refs[oi].dtype)
    for c0 in range(0, n, chunk):
        cw = min(chunk, n - c0)
        acc = _dot(h, w_ref[:, c0:c0 + cw])
        for j in range(cw // LANES):
            kind, gain_row, factor, oi, oc = blocks[(c0 // LANES) + j]
            v = acc[:, j * LANES:(j + 1) * LANES]
            if kind == "norm":
                v2 = v * v
                hi, lo = _split_bf16(v2)
                ms = _dot(hi, gmat_ref[...]) + _dot(lo, gmat_ref[...])
                v = v * lax.rsqrt(ms + RMS_EPS) * gn_ref[gain_row:gain_row + 1, :]
            if kind in ("rope", "norm") and use_rope:
                v = (v * cos_ref[...] + pltpu.roll(v, 16, 1) * sinp_ref[...]
                     + pltpu.roll(v, LANES - 16, 1) * sinm_ref[...])
            if factor != 1.0:
                v = v * factor
            o_refs[oi][0, :, oc:oc + LANES] = v.astype(o_refs[oi].dtype)


def _inproj_call(x, sc, sh, w, blocks, out_defs, rope_tabs, gains, gmat, *, use_rope, name, ones_cols=()):
    b, s, d = x.shape
    n = w.shape[1]
    tm = min(512, s)
    per_b = sc.shape[0] == b
    mod_map = (lambda bi, i: (bi, 0, 0)) if per_b else (lambda bi, i: (0, 0, 0))
    cos, sinp, sinm = rope_tabs
    tab_spec = pl.BlockSpec((tm, LANES), lambda bi, i: (i, 0))
    kern = functools.partial(_inproj_kernel, blocks=tuple(blocks), ones_cols=tuple(ones_cols), chunk=512,
                             use_rope=use_rope)
    return pl.pallas_call(
        kern,
        out_shape=[jax.ShapeDtypeStruct((b, s, nc), dt) for nc, dt in out_defs],
        grid=(b, s // tm),
        in_specs=[pl.BlockSpec((1, tm, d), lambda bi, i: (bi, i, 0)),
                  pl.BlockSpec((1, 1, d), mod_map),
                  pl.BlockSpec((1, 1, d), mod_map),
                  pl.BlockSpec((d, n), lambda bi, i: (0, 0)),
                  tab_spec, tab_spec, tab_spec,
                  pl.BlockSpec(gains.shape, lambda bi, i: (0, 0)),
                  pl.BlockSpec(gmat.shape, lambda bi, i: (0, 0))],
        out_specs=[pl.BlockSpec((1, tm, nc), lambda bi, i: (bi, i, 0)) for nc, _ in out_defs],
        compiler_params=_params(("arbitrary", "arbitrary")),
        name=name,
    )(x, sc, sh, w, cos, sinp, sinm, gains, gmat)


def _rope_tables(s):
    n_freq = HEAD_DIM // 4
    t = jnp.arange(s, dtype=jnp.int32)
    inv = ROPE_THETA ** (-jnp.arange(n_freq, dtype=F32) / n_freq)
    ang_r = (t // GRID_W).astype(F32)[:, None] * inv
    ang_c = (t % GRID_W).astype(F32)[:, None] * inv
    ang = jnp.concatenate([ang_r, ang_r, ang_c, ang_c] * (LANES // HEAD_DIM), axis=-1)
    first = (jnp.arange(LANES) % 32) < 16
    cos, sin = jnp.cos(ang), jnp.sin(ang)
    return cos, jnp.where(first, 0.0, sin), jnp.where(first, -sin, 0.0)


EV_AQ, EV_AK, EV_AV = 0, 512, 1024
EV_BQ = EV_AV + A_HEADS * 2 * LANES
EV_BK = EV_BQ + (B_HEADS // 2) * LANES
EV_BV = EV_BK + LANES
EV_WIDTH = EV_BV + 2 * LANES


def _attn_even_kernel(q_ref, lam_ref, sg_ref, *refs, n_kv, tq, lam_init):
    kv_refs, o_ref = refs[:n_kv], refs[n_kv]
    lo, hi = _lane_masks()
    lo_b, hi_b = lo.astype(BF16), hi.astype(BF16)
    lv = lam_ref[...]
    lam = (jnp.exp(jnp.sum(lv[0:1] * lv[1:2], axis=-1, keepdims=True))
           - jnp.exp(jnp.sum(lv[2:3] * lv[3:4], axis=-1, keepdims=True)) + lam_init)

    def attend(q, kcol, vcol):
        qq = jnp.concatenate([q * lo_b, q * hi_b], axis=0)
        ss = [_dot(qq, kv[0, :, kcol:kcol + LANES], _NT) for kv in kv_refs]
        m = ss[0].max(axis=-1, keepdims=True)
        for s in ss[1:]:
            m = jnp.maximum(m, s.max(axis=-1, keepdims=True))
        acc = None
        for s, kv in zip(ss, kv_refs):
            part = _dot(jnp.exp2(s - m).astype(BF16), kv[0, :, vcol:vcol + 2 * LANES])
            acc = part if acc is None else acc + part
        return acc[:, :LANES] * (1.0 / acc[:, LANES:])

    for h in range(A_HEADS):
        on = attend(q_ref[0, :, EV_AQ + h * LANES:EV_AQ + (h + 1) * LANES],
                    EV_AK + h * LANES, EV_AV + h * 2 * LANES)
        o = on[:tq] - lam * on[tq:]
        ms = jnp.mean(o * o, axis=-1, keepdims=True)
        o = o * lax.rsqrt(ms + RMS_EPS) * sg_ref[...] * (1.0 - lam_init)
        o_ref[0, :, h * LANES:(h + 1) * LANES] = o.astype(o_ref.dtype)

    for j in range(B_HEADS // 2):
        on = attend(q_ref[0, :, EV_BQ + j * LANES:EV_BQ + (j + 1) * LANES], EV_BK, EV_BV)
        o = on[:tq] * lo + on[tq:] * hi
        o_ref[0, :, 512 + j * LANES:512 + (j + 1) * LANES] = o.astype(o_ref.dtype)


def _attn_even_call(q_arr, kv_arrs, lamv, subln_g, lam_init, name):
    b, sq, n = q_arr.shape
    tq = min(256, sq)
    kern = functools.partial(_attn_even_kernel, n_kv=len(kv_arrs), tq=tq, lam_init=lam_init)
    return pl.pallas_call(
        kern,
        out_shape=jax.ShapeDtypeStruct((b, sq, 1024), BF16),
        grid=(b, sq // tq),
        in_specs=[pl.BlockSpec((1, tq, n), lambda bi, i: (bi, i, 0)),
                  pl.BlockSpec(lamv.shape, lambda bi, i: (0, 0)),
                  pl.BlockSpec(subln_g.shape, lambda bi, i: (0, 0))]
                 + [pl.BlockSpec((1,) + a.shape[1:], lambda bi, i: (bi, 0, 0)) for a in kv_arrs],
        out_specs=pl.BlockSpec((1, tq, 1024), lambda bi, i: (bi, i, 0)),
        compiler_params=_params(("arbitrary", "arbitrary")),
        name=name,
    )(q_arr, lamv, subln_g, *kv_arrs)


def _odd_mixer_kernel(qkv_ref, ckv_ref, bias_ref, du_ref, band_ref, edge_ref, pw_ref, ps_ref, o_ref, *,
                      tq, slab_rows, n_rows, n_blk, seq):
    i = pl.program_id(1)
    lo, hi = _lane_masks()
    lo_b, hi_b = lo.astype(BF16), hi.astype(BF16)
    t0 = pl.multiple_of(i * tq, tq)
    base = jnp.clip(i * NA_QROWS - NA_ROWS // 2, 0, n_rows - slab_rows)
    k0 = pl.multiple_of(base * GRID_W, GRID_W)
    nk = slab_rows * GRID_W
    cw = C_HEADS * HEAD_DIM

    for j in range(C_HEADS // 2):
        cs = slice(j * LANES, (j + 1) * LANES)
        q = qkv_ref[0, pl.ds(t0, tq), cs]
        qq = jnp.concatenate([q * lo_b, q * hi_b], axis=0)
        kl = qkv_ref[0, pl.ds(k0, nk), cw + j * LANES:cw + (j + 1) * LANES]
        vl = qkv_ref[0, pl.ds(k0, nk), 2 * cw + 2 * j * LANES:2 * cw + 2 * (j + 1) * LANES]
        kc = ckv_ref[0, :, cs]
        vc = ckv_ref[0, :, cw + 2 * j * LANES:cw + 2 * (j + 1) * LANES]
        bias = jnp.concatenate([bias_ref[2 * j], bias_ref[2 * j + 1]], axis=0)
        s_l = _dot(qq, kl, _NT) + bias
        s_c = _dot(qq, kc, _NT)
        m = jnp.maximum(s_l.max(axis=-1, keepdims=True), s_c.max(axis=-1, keepdims=True))
        acc = _dot(jnp.exp2(s_l - m).astype(BF16), vl) + _dot(jnp.exp2(s_c - m).astype(BF16), vc)
        on = acc[:, :LANES] * (1.0 / acc[:, LANES:])
        o = on[:tq] * lo + on[tq:] * hi
        o_ref[0, :, cs] = o.astype(o_ref.dtype)

    tprev = pl.multiple_of(jnp.maximum(i - 1, 0) * tq, tq)
    tnext = pl.multiple_of(jnp.minimum(i + 1, n_blk - 1) * tq, tq)
    has_prev = jnp.where(i > 0, 1.0, 0.0).astype(F32)
    has_next = jnp.where(i < n_blk - 1, 1.0, 0.0).astype(F32)
    tpos = t0 + lax.broadcasted_iota(jnp.int32, (tq, 1), 0)
    for g in range(D_GROUPS):
        half = POOL_WINDOWS[g] // 2
        gs = slice(g * D_GROUP_DIM, (g + 1) * D_GROUP_DIM)
        cur = du_ref[0, pl.ds(t0, tq), gs]
        prv = du_ref[0, pl.ds(tprev + (tq - POOL_EDGE), POOL_EDGE), gs]
        nxt = du_ref[0, pl.ds(tnext, POOL_EDGE), gs]

        def band_sum(band, u):
            u_hi, u_lo = _split_bf16(u)
            return _dot(band, u_hi) + _dot(band, u_lo)

        wsum = band_sum(band_ref[g], cur)
        wsum = jnp.concatenate([wsum[:POOL_EDGE] + has_prev * band_sum(edge_ref[g, 0], prv),
                                wsum[POOL_EDGE:tq - POOL_EDGE],
                                wsum[tq - POOL_EDGE:] + has_next * band_sum(edge_ref[g, 1], nxt)], axis=0)
        cnt = (jnp.minimum(tpos + half, seq) - jnp.maximum(tpos - half, 0)).astype(F32)
        pooled = wsum * (1.0 / cnt) - cur
        od = _dot(pooled.astype(BF16), pw_ref[g]) * ps_ref[:, gs]
        o_ref[0, :, cw + g * D_GROUP_DIM:cw + (g + 1) * D_GROUP_DIM] = od.astype(o_ref.dtype)


def _na_geometry(n_rows):
    kh = min(NA_ROWS, n_rows)
    slab = min(n_rows, NA_QROWS + kh)
    n_blk = n_rows // NA_QROWS
    bases = [min(max(i * NA_QROWS - NA_ROWS // 2, 0), n_rows - slab) for i in range(n_blk)]
    sigs, type_of = [], []
    for i in range(n_blk):
        sig = tuple((min(max(i * NA_QROWS + r - kh // 2, 0), n_rows - kh) - bases[i],
                     i * NA_QROWS + r - bases[i]) for r in range(NA_QROWS))
        if sig not in sigs:
            sigs.append(sig)
        type_of.append(sigs.index(sig))
    return kh, slab, n_blk, sigs, type_of


def _na_bias_table(rpb, n_rows):
    kh, slab, _, sigs, _ = _na_geometry(n_rows)
    kw = NA_COLS
    n_dc = 2 * NA_COLS - 1
    w = np.arange(GRID_W)
    cstart = np.clip(w - kw // 2, 0, GRID_W - kw)
    col_ok = (w[None, :] >= cstart[:, None]) & (w[None, :] < cstart[:, None] + kw)
    dc = np.clip(w[None, :] - w[:, None] + (NA_COLS - 1), 0, n_dc - 1)
    n_dr = 2 * NA_ROWS - 1
    heads = rpb.shape[0]
    onehot = ((dc[None] == np.arange(n_dc)[:, None, None]) & col_ok[None]).astype(np.float32)
    toep = jnp.einsum("hrd,dwj->hrwj", rpb, jnp.asarray(onehot), precision=lax.Precision.HIGHEST)
    toep = jnp.where(jnp.asarray(col_ok), toep, NEG_BIG)
    toep = jnp.concatenate([toep, jnp.full((heads, 1, GRID_W, GRID_W), NEG_BIG, F32)], axis=1)
    dr_idx = np.full((len(sigs), NA_QROWS, slab), n_dr, np.int32)
    for t, sig in enumerate(sigs):
        for r, (rs_rel, qr_rel) in enumerate(sig):
            for m in range(slab):
                if rs_rel <= m < rs_rel + kh:
                    dr_idx[t, r, m] = m - qr_rel + (NA_ROWS - 1)
    big = jnp.take(toep, jnp.asarray(dr_idx.reshape(-1)), axis=1)
    big = big.reshape(heads, len(sigs), NA_QROWS, slab, GRID_W, GRID_W).transpose(0, 1, 2, 4, 3, 5)
    return big.reshape(heads, len(sigs), NA_QROWS * GRID_W, slab * GRID_W)


POOL_EDGE = 16


def _pool_bands(tq):
    t = jnp.arange(tq, dtype=jnp.int32)[:, None]
    sidx = jnp.arange(tq, dtype=jnp.int32)[None, :]
    main, edge = [], []
    for wdw in POOL_WINDOWS:
        half = wdw // 2
        full = [((sidx + (m - 1) * tq >= t - half) & (sidx + (m - 1) * tq < t + half)).astype(BF16)
                for m in range(3)]
        main.append(full[1])
        edge.append(jnp.stack([full[0][:POOL_EDGE, tq - POOL_EDGE:], full[2][tq - POOL_EDGE:, :POOL_EDGE]]))
    return jnp.stack(main), jnp.stack(edge)


def _odd_mixer_call(qkv, ckv, rpb, du, pool_w, pool_scale):
    b, s, n = qkv.shape
    n_rows = s // GRID_W
    _, slab, n_blk, _, type_of = _na_geometry(n_rows)
    tq = NA_QROWS * GRID_W
    bias = _na_bias_table(rpb, n_rows)
    bands, edges = _pool_bands(tq)

    def bias_map(bi, i):
        t = jnp.int32(type_of[-1])
        for blk in range(n_blk - 2, -1, -1):
            t = jnp.where(i == blk, jnp.int32(type_of[blk]), t)
        return (0, t, 0, 0)

    kern = functools.partial(_odd_mixer_kernel, tq=tq, slab_rows=slab, n_rows=n_rows, n_blk=n_blk, seq=s)
    return pl.pallas_call(
        kern,
        out_shape=jax.ShapeDtypeStruct((b, s, 1024), BF16),
        grid=(b, n_blk),
        in_specs=[pl.BlockSpec((1, s, n), lambda bi, i: (bi, 0, 0)),
                  pl.BlockSpec((1,) + ckv.shape[1:], lambda bi, i: (bi, 0, 0)),
                  pl.BlockSpec((bias.shape[0], None) + bias.shape[2:], bias_map),
                  pl.BlockSpec((1, s, du.shape[2]), lambda bi, i: (bi, 0, 0)),
                  pl.BlockSpec(bands.shape, lambda bi, i: (0, 0, 0)),
                  pl.BlockSpec(edges.shape, lambda bi, i: (0, 0, 0, 0)),
                  pl.BlockSpec(pool_w.shape, lambda bi, i: (0, 0, 0)),
                  pl.BlockSpec(pool_scale.shape, lambda bi, i: (0, 0))],
        out_specs=pl.BlockSpec((1, tq, 1024), lambda bi, i: (bi, i, 0)),
        compiler_params=_params(("arbitrary", "arbitrary")),
        name="odd_mixer",
    )(qkv, ckv, bias, du, bands, edges, pool_w, pool_scale)


OUTPROJ_SUB = 256


def _outproj_kernel(mix_ref, w_ref, x_ref, g1_ref, lng_ref, lnb_ref, sc_ref, sh_ref, r2_ref, rh_ref,
                    x1_ref, hm_ref, aff_ref, *, n_exp):
    tm = x_ref.shape[1]
    for r0 in range(0, tm, OUTPROJ_SUB):
        rows = slice(r0, r0 + OUTPROJ_SUB)
        y = _dot(mix_ref[0, rows, :], w_ref[...])
        x1 = _layer_norm(DEEPNORM_ALPHA * x_ref[0, rows, :] + g1_ref[0] * y, lng_ref[...], lnb_ref[...])
        x1_ref[0, rows, :] = x1
        hm = x1 * (1.0 + sc_ref[0]) + sh_ref[0]
        hm_hi = hm.astype(BF16)
        hm_ref[0, rows, :] = hm_hi
        hm_lo = (hm - hm_hi.astype(F32)).astype(BF16)
        part = _dot(hm_hi, r2_ref[...])
        logits = part[:, :n_exp] + part[:, n_exp:] + _dot(hm_lo, rh_ref[...])
        ex = jnp.exp(logits - logits.max(axis=-1, keepdims=True))
        aff_ref[0, rows, :] = ex * (1.0 / ex.sum(axis=-1, keepdims=True))


def _outproj_call(mix, w, x, g1, lng, lnb, sc2, sh2, router, name):
    b, s, d = x.shape
    dm = mix.shape[2]
    e = router.shape[1]
    tm = min(2 * OUTPROJ_SUB, s)
    r_hi = router.astype(BF16)
    r_lo = (router - r_hi.astype(F32)).astype(BF16)
    r2 = jnp.concatenate([r_hi, r_lo], axis=1)
    per_b = g1.shape[0] == b
    mod_map = (lambda bi, i: (bi, 0, 0)) if per_b else (lambda bi, i: (0, 0, 0))
    vec = pl.BlockSpec((1, 1, d), mod_map)
    row = pl.BlockSpec((1, d), lambda bi, i: (0, 0))
    tile = lambda n: pl.BlockSpec((1, tm, n), lambda bi, i: (bi, i, 0))
    return pl.pallas_call(
        functools.partial(_outproj_kernel, n_exp=e),
        out_shape=[jax.ShapeDtypeStruct((b, s, d), F32),
                   jax.ShapeDtypeStruct((b, s, d), BF16),
                   jax.ShapeDtypeStruct((b, s, e), F32)],
        grid=(b, s // tm),
        in_specs=[tile(dm),
                  pl.BlockSpec((dm, d), lambda bi, i: (0, 0)),
                  tile(d),
                  vec, row, row, vec, vec,
                  pl.BlockSpec((d, 2 * e), lambda bi, i: (0, 0)),
                  pl.BlockSpec((d, e), lambda bi, i: (0, 0))],
        out_specs=[tile(d), tile(d), tile(e)],
        compiler_params=_params(("arbitrary", "arbitrary")),
        name=name,
    )(mix, w, x, g1, lng, lnb, sc2, sh2, r2, r_hi)


def _lane_cumsum(m):
    rows, s = m.shape
    r_i = lax.broadcasted_iota(jnp.int32, (LANES, LANES), 0)
    c_i = lax.broadcasted_iota(jnp.int32, (LANES, LANES), 1)
    tri = jnp.where(r_i <= c_i, 1.0, 0.0).astype(BF16)
    carry = jnp.zeros((rows, 1), F32)
    out = []
    for c in range(s // LANES):
        blk = m[:, c * LANES:(c + 1) * LANES]
        out.append(_dot(blk.astype(BF16), tri) + carry)
        carry = carry + blk.sum(axis=-1, keepdims=True)
    return jnp.concatenate(out, axis=-1)


ROUTE_TILE = 256


def _route_kernel(aff_ref, pos_ref, cnt_ref, *, cap):
    a = aff_ref[...]
    bits = pltpu.bitcast(a, jnp.int32)
    thr = jnp.zeros((a.shape[0], 1), jnp.int32)
    for bit in range(30, -1, -1):
        cand = thr | jnp.int32(1 << bit)
        cnt = jnp.where(bits >= cand, 1.0, 0.0).sum(axis=-1, keepdims=True)
        thr = jnp.where(cnt >= cap, cand, thr)
    gt = jnp.where(bits > thr, 1.0, 0.0)
    eq = jnp.where(bits == thr, 1.0, 0.0)
    need = cap - gt.sum(axis=-1, keepdims=True)
    sel = gt + eq * jnp.where(_lane_cumsum(eq) <= need, 1.0, 0.0)
    pos_ref[...] = jnp.where(sel > 0.5, _lane_cumsum(sel) - 1.0, -1.0)
    tok = lax.broadcasted_iota(jnp.int32, (a.shape[1], LANES), 0)
    tile = lax.broadcasted_iota(jnp.int32, (a.shape[1], LANES), 1)
    before = jnp.where(tok < tile * ROUTE_TILE, 1.0, 0.0).astype(BF16)
    cnt_ref[...] = _dot(sel.astype(BF16), before)


def _route_call(aff, cap, name):
    b, e, s = aff.shape
    n_tiles = s // min(ROUTE_TILE, s)
    full = pl.BlockSpec((b * e, s), lambda i: (0, 0))
    pos, cnt = pl.pallas_call(
        functools.partial(_route_kernel, cap=cap),
        out_shape=[jax.ShapeDtypeStruct((b * e, s), F32), jax.ShapeDtypeStruct((b * e, LANES), F32)],
        grid=(1,),
        in_specs=[full],
        out_specs=[full, pl.BlockSpec((b * e, LANES), lambda i: (0, 0))],
        compiler_params=_params(("arbitrary",)),
        name=name,
    )(aff.reshape(b * e, s))
    return pos.reshape(b, e, s), cnt[:, :n_tiles + 1].astype(jnp.int32).reshape(-1)


def _gather_kernel(cnt_ref, pos_ref, aff_ref, hm_ref, xg_ref, gs_ref, *, cap, n_exp, win, n_tiles):
    bi, i = pl.program_id(0), pl.program_id(1)

    @pl.when(i == 0)
    def _():
        xg_ref[...] = jnp.zeros(xg_ref.shape, xg_ref.dtype)
        gs_ref[...] = jnp.zeros(gs_ref.shape, gs_ref.dtype)

    starts, short = [], None
    for ei in range(n_exp):
        base = (bi * n_exp + ei) * (n_tiles + 1) + i
        st = jnp.minimum((cnt_ref[base] // SLOT_ALIGN) * SLOT_ALIGN, cap - win)
        starts.append(pl.multiple_of(st, SLOT_ALIGN))
        miss = cnt_ref[base + 1] > st + win
        short = miss if short is None else jnp.logical_or(short, miss)

    def scatter_rows(n_rows, first, stacked):
        row = lax.broadcasted_iota(jnp.int32, (n_rows, 1), 0).astype(F32)
        blocks = []
        for ei in range(n_exp):
            rel = pos_ref[0, ei:ei + 1, :] if first is None else pos_ref[0, ei:ei + 1, :] - first[ei].astype(F32)
            hit = rel == row
            blocks.append(jnp.where(hit, 1.0, 0.0).astype(BF16))
            gate = jnp.where(hit, aff_ref[0, ei:ei + 1, :], 0.0).sum(axis=-1, keepdims=True)
            rows = slice(None) if first is None else pl.ds(first[ei], n_rows)
            gs_ref[0, ei, rows, :] += gate
        if stacked:
            part = _dot(jnp.concatenate(blocks, axis=0), hm_ref[0])
        for ei in range(n_exp):
            rows = slice(None) if first is None else pl.ds(first[ei], n_rows)
            sub = part[ei * n_rows:(ei + 1) * n_rows] if stacked else _dot(blocks[ei], hm_ref[0])
            xg_ref[0, ei, rows, :] += sub.astype(xg_ref.dtype)

    @pl.when(jnp.logical_not(short))
    def _():
        scatter_rows(win, starts, True)

    @pl.when(short)
    def _():
        scatter_rows(cap, None, False)


def _gather_call(cnt, pos, aff, hm, cap, name):
    b, e, s = pos.shape
    d = hm.shape[2]
    tm = min(ROUTE_TILE, s)
    n_tiles = s // tm
    win = min(64, cap)
    tok = pl.BlockSpec((1, e, tm), lambda bi, i, c: (bi, 0, i))
    return pl.pallas_call(
        functools.partial(_gather_kernel, cap=cap, n_exp=e, win=win, n_tiles=n_tiles),
        out_shape=[jax.ShapeDtypeStruct((b, e, cap, d), BF16),
                   jax.ShapeDtypeStruct((b, e, cap, 1), F32)],
        grid_spec=pltpu.PrefetchScalarGridSpec(
            num_scalar_prefetch=1,
            grid=(b, n_tiles),
            in_specs=[tok, tok, pl.BlockSpec((1, tm, d), lambda bi, i, c: (bi, i, 0))],
            out_specs=[pl.BlockSpec((1, e, cap, d), lambda bi, i, c: (bi, 0, 0, 0)),
                       pl.BlockSpec((1, e, cap, 1), lambda bi, i, c: (bi, 0, 0, 0))]),
        compiler_params=_params(("arbitrary", "arbitrary")),
        name=name,
    )(cnt, pos, aff, hm)


def _ffn_kernel(*refs, n_grp, row_chunks):
    xg_refs = refs[:n_grp]
    gs_refs = refs[n_grp:2 * n_grp]
    wg_ref, wu_ref, wd_ref = refs[2 * n_grp:2 * n_grp + 3]
    y_refs = refs[2 * n_grp + 3:3 * n_grp + 3]
    acc_refs = refs[3 * n_grp + 3:]
    f = pl.program_id(1)
    last = pl.num_programs(1) - 1
    wg = wg_ref[0].astype(BF16)
    wu = wu_ref[0].astype(BF16)
    wd = wd_ref[0].astype(BF16)

    @pl.when(f == 0)
    def _():
        for acc_ref in acc_refs:
            acc_ref[...] = jnp.zeros(acc_ref.shape, F32)

    for xg_ref, acc_ref, nb in zip(xg_refs, acc_refs, row_chunks):
        bt, _, cap, d = xg_ref.shape
        for b0 in range(0, bt, nb):
            rows = nb * cap
            r0 = b0 * cap
            x = xg_ref[b0:b0 + nb, 0].reshape(rows, d)
            hg = _dot(x, wg)
            hu = _dot(x, wu)
            hid = (hg * (1.0 / (1.0 + jnp.exp(-hg))) * hu).astype(BF16)
            acc_ref[r0:r0 + rows, :] += _dot(hid, wd)

    @pl.when(f == last)
    def _():
        for xg_ref, gs_ref, y_ref, acc_ref in zip(xg_refs, gs_refs, y_refs, acc_refs):
            bt, _, cap, d = xg_ref.shape
            gate = gs_ref[:, 0].reshape(bt * cap, 1)
            y_ref[:, 0] = (acc_ref[...] * gate).reshape(bt, cap, d).astype(y_ref.dtype)


def _ffn_call(xgs, gss, w_gate, w_up, w_down, layer, name):
    _, e, d, ff = w_gate.shape
    tf = 512 if ff % 512 == 0 else ff
    n_grp = len(xgs)
    row_chunks = []
    for xg in xgs:
        bt, _, cap, _ = xg.shape
        nb = max(1, min(bt, 512 // cap))
        while bt % nb:
            nb -= 1
        row_chunks.append(nb)
    tok = lambda a: pl.BlockSpec((a.shape[0], 1) + a.shape[2:], lambda ei, fi: (0, ei, 0, 0))
    kern = functools.partial(_ffn_kernel, n_grp=n_grp, row_chunks=tuple(row_chunks))
    return pl.pallas_call(
        kern,
        out_shape=[jax.ShapeDtypeStruct(xg.shape, BF16) for xg in xgs],
        grid=(e, ff // tf),
        in_specs=[tok(a) for a in xgs] + [tok(a) for a in gss]
                 + [pl.BlockSpec((None, 1, d, tf), lambda ei, fi: (layer, ei, 0, fi)),
                    pl.BlockSpec((None, 1, d, tf), lambda ei, fi: (layer, ei, 0, fi)),
                    pl.BlockSpec((None, 1, tf, d), lambda ei, fi: (layer, ei, fi, 0))],
        out_specs=[tok(a) for a in xgs],
        scratch_shapes=[pltpu.VMEM((xg.shape[0] * xg.shape[2], d), F32) for xg in xgs],
        compiler_params=_params(("arbitrary", "arbitrary")),
        name=name,
    )(*xgs, *gss, w_gate, w_up, w_down)


MXU_DEPTH = 256
SLOT_ALIGN = 16


def _combine_kernel(cnt_ref, pos_ref, y_ref, x_ref, g_ref, lng_ref, lnb_ref, o_ref, *, cap, n_exp, win, n_tiles):
    bi, i = pl.program_id(0), pl.program_id(1)
    pos = pos_ref[0]
    tm = pos.shape[0]
    grp = MXU_DEPTH // win
    starts, short = [], None
    for ei in range(n_exp):
        base = (bi * n_exp + ei) * (n_tiles + 1) + i
        st = jnp.minimum((cnt_ref[base] // SLOT_ALIGN) * SLOT_ALIGN, cap - win)
        starts.append(st)
        miss = cnt_ref[base + 1] > st + win
        short = miss if short is None else jnp.logical_or(short, miss)

    def windowed():
        lane = lax.broadcasted_iota(jnp.int32, (1, grp * win), 1)
        lane_f = lane.astype(F32)
        acc = None
        for k in range(n_exp // grp):
            tgt, rows = None, []
            for u in range(grp - 1, -1, -1):
                ei = k * grp + u
                st = starts[ei]
                rel = pos[:, ei:ei + 1] - st.astype(F32)
                rel = jnp.where(rel >= 0.0, jnp.where(rel < win, rel + float(u * win), -1.0), -1.0)
                tgt = rel if tgt is None else jnp.where(lane < (u + 1) * win, rel, tgt)
                rows.insert(0, y_ref[0, pl.ds(pl.multiple_of(ei * cap + st, SLOT_ALIGN), win), :])
            onehot = jnp.where(tgt == lane_f, 1.0, 0.0).astype(BF16)
            part = _dot(onehot, jnp.concatenate(rows, axis=0))
            acc = part if acc is None else acc + part
        return acc

    def dense():
        slot = lax.broadcasted_iota(jnp.int32, (1, cap), 1).astype(F32)
        acc = None
        for ei in range(n_exp):
            onehot = jnp.where(pos[:, ei:ei + 1] == slot, 1.0, 0.0).astype(BF16)
            part = _dot(onehot, y_ref[0, ei * cap:(ei + 1) * cap, :])
            acc = part if acc is None else acc + part
        return acc

    acc = lax.cond(short, dense, windowed)
    o_ref[0] = _layer_norm(DEEPNORM_ALPHA * x_ref[0] + g_ref[0] * acc, lng_ref[...], lnb_ref[...])


def _combine_call(cnt, pos_t, y, x, g2, lng, lnb, cap, name):
    b, s, d = x.shape
    e = pos_t.shape[2]
    tm = min(ROUTE_TILE, s)
    n_tiles = s // tm
    win = min(64, cap)
    per_b = g2.shape[0] == b
    mod_map = (lambda bi, i, c: (bi, 0, 0)) if per_b else (lambda bi, i, c: (0, 0, 0))
    row = pl.BlockSpec((1, d), lambda bi, i, c: (0, 0))
    return pl.pallas_call(
        functools.partial(_combine_kernel, cap=cap, n_exp=e, win=win, n_tiles=n_tiles),
        out_shape=jax.ShapeDtypeStruct((b, s, d), F32),
        grid_spec=pltpu.PrefetchScalarGridSpec(
            num_scalar_prefetch=1,
            grid=(b, n_tiles),
            in_specs=[pl.BlockSpec((1, tm, e), lambda bi, i, c: (bi, i, 0)),
                      pl.BlockSpec((1, e * cap, d), lambda bi, i, c: (bi, 0, 0)),
                      pl.BlockSpec((1, tm, d), lambda bi, i, c: (bi, i, 0)),
                      pl.BlockSpec((1, 1, d), mod_map), row, row],
            out_specs=pl.BlockSpec((1, tm, d), lambda bi, i, c: (bi, i, 0))),
        compiler_params=_params(("arbitrary", "arbitrary")),
        name=name,
    )(cnt, pos_t, y.reshape(b, e * cap, d), x, g2, lng, lnb)


def _ffn_sublayer(streams, mods, lng, lnb, router, w_gate, w_up, w_down, layer):
    staged = []
    for si, (mix, w_out, x, g1, sc2, sh2, g2) in enumerate(streams):
        tag = f"l{layer}s{si}"
        n_tok = x.shape[1]
        cap = EC_FACTOR * n_tok // N_EXPERTS
        x1, hm, aff_t = _outproj_call(mix, w_out, x, g1, lng[0:1], lnb[0:1], sc2, sh2, router, "outproj_" + tag)
        aff = jnp.swapaxes(aff_t, 1, 2)
        pos, cnt = _route_call(aff, cap, "route_" + tag)
        xg, gs = _gather_call(cnt, pos, aff, hm, cap, "gather_" + tag)
        staged.append((x1, pos, cnt, xg, gs, g2, cap, tag))
    ys = _ffn_call([st[3] for st in staged], [st[4] for st in staged], w_gate, w_up, w_down, layer,
                   f"ffn_l{layer}")
    outs = []
    for (x1, pos, cnt, _, _, g2, cap, tag), y in zip(staged, ys):
        outs.append(_combine_call(cnt, jnp.swapaxes(pos, 1, 2), y, x1, g2, lng[1:2], lnb[1:2], cap,
                                  "combine_" + tag))
    return outs


def kernel(x, c, ctx, c_ctx, ada_w, ada_b, ln_g, ln_b, l0_w_in, l0_w_out, l0_lam_q1, l0_lam_k1, l0_lam_q2,
           l0_lam_k2, l0_subln_g, l0_qnorm_g, l0_knorm_g, l1_w_in, l1_w_out, l1_rpb, l1_pool_w, l1_pool_scale,
           moe_router, moe_w_gate, moe_w_up, moe_w_down):
    b, s, d = x.shape
    n_ctx = ctx.shape[1]

    rows = -(-(b + 1) // 8) * 8
    cc = jnp.zeros((rows, d), F32).at[:b].set(c).at[b].set(c_ctx)
    mod = _ada_call(cc, ada_w, ada_b)

    def mods(i):
        lat = [mod[i, :b, k * d:(k + 1) * d].reshape(b, 1, d) for k in range(6)]
        cx = [mod[i, b:b + 1, k * d:(k + 1) * d].reshape(1, 1, d) for k in range(6)]
        return lat, cx

    rope = _rope_tables(s)
    rope_id = (jnp.ones((n_ctx, LANES), F32), jnp.zeros((n_ctx, LANES), F32), jnp.zeros((n_ctx, LANES), F32))
    gmat = jnp.where((jnp.arange(LANES)[:, None] // HEAD_DIM) == (jnp.arange(LANES)[None, :] // HEAD_DIM),
                     1.0 / HEAD_DIM, 0.0).astype(BF16)
    tile2 = lambda g: jnp.concatenate([g, g]).reshape(1, LANES)

    (sh1, sc1, g1, sh2, sc2, g2), (csh1, csc1, cg1, csh2, csc2, cg2) = mods(0)
    bq0 = 3 * 512
    pair_cols = jnp.concatenate([jnp.arange(HEAD_DIM) + bq0 + hh * HEAD_DIM
                                 for j in range(B_HEADS // 2) for hh in (j, j + B_HEADS // 2)])
    in_perm = jnp.concatenate([jnp.arange(bq0), pair_cols, jnp.arange(bq0 + 512, l0_w_in.shape[1])])
    w_in0 = l0_w_in[:, in_perm].astype(BF16)
    w_out0 = l0_w_out[jnp.concatenate([jnp.arange(512), pair_cols - bq0 + 512])].astype(BF16)
    gains0 = jnp.concatenate([tile2(l0_qnorm_g), tile2(l0_knorm_g)], axis=0)
    q_exp2 = Q_SCALE * math.log2(math.e)
    blocks0 = ([("rope", 0, q_exp2, 0, EV_AQ + k * LANES) for k in range(4)]
               + [("rope", 0, 1.0, 0, EV_AK + k * LANES) for k in range(4)]
               + [("plain", 0, 1.0, 0, EV_AV + k * 2 * LANES) for k in range(4)]
               + [("norm", 0, q_exp2, 0, EV_BQ + k * LANES) for k in range(4)]
               + [("norm", 1, 1.0, 0, EV_BK), ("plain", 0, 1.0, 0, EV_BV)])
    ones0 = [(0, EV_AV + (2 * k + 1) * LANES) for k in range(4)] + [(0, EV_BV + LANES)]
    qkv = _inproj_call(x, sc1, sh1, w_in0, blocks0, [(EV_WIDTH, BF16)], rope, gains0, gmat,
                       use_rope=True, name="inproj_l0", ones_cols=ones0)[0]
    qkv_c = _inproj_call(ctx, csc1, csh1, w_in0, blocks0, [(EV_WIDTH, BF16)], rope_id, gains0, gmat,
                         use_rope=False, name="inproj_l0c", ones_cols=ones0)[0]
    lam_init = 0.8 - 0.6 * math.exp(-0.3 * 0)
    lamv = jnp.stack([l0_lam_q1, l0_lam_k1, l0_lam_q2, l0_lam_k2], axis=0)
    sub_g = l0_subln_g.reshape(1, LANES)
    mix = _attn_even_call(qkv, [qkv, qkv_c], lamv, sub_g, lam_init, "attn_l0")
    mix_c = _attn_even_call(qkv_c, [qkv_c], lamv, sub_g, lam_init, "attn_l0c")
    x, ctx = _ffn_sublayer(
        [(mix, w_out0, x, g1, sc2, sh2, g2), (mix_c, w_out0, ctx, cg1, csc2, csh2, cg2)],
        None, ln_g[0], ln_b[0], moe_router[0], moe_w_gate, moe_w_up, moe_w_down, 0)

    (sh1, sc1, g1, sh2, sc2, g2), (csh1, csc1, _, _, _, _) = mods(1)
    cw = C_HEADS * HEAD_DIM
    w_in1 = l1_w_in.astype(BF16)
    blocks1 = ([("plain", 0, q_exp2, 0, k * LANES) for k in range(4)]
               + [("plain", 0, 1.0, 0, cw + k * LANES) for k in range(4)]
               + [("plain", 0, 1.0, 0, 2 * cw + 2 * k * LANES) for k in range(4)]
               + [("plain", 0, 1.0, 1, k * LANES) for k in range(4)])
    ones1 = [(0, 2 * cw + (2 * k + 1) * LANES) for k in range(4)]
    qkv1, du = _inproj_call(x, sc1, sh1, w_in1, blocks1, [(4 * cw, BF16), (D_GROUPS * D_GROUP_DIM, F32)],
                            rope, gains0, gmat, use_rope=False, name="inproj_l1", ones_cols=ones1)
    blocks1c = ([("plain", 0, 1.0, 0, k * LANES) for k in range(4)]
                + [("plain", 0, 1.0, 0, cw + 2 * k * LANES) for k in range(4)])
    ones1c = [(0, cw + (2 * k + 1) * LANES) for k in range(4)]
    ckv = _inproj_call(ctx, csc1, csh1, w_in1[:, cw:3 * cw], blocks1c, [(3 * cw, BF16)], rope_id, gains0, gmat,
                       use_rope=False, name="inproj_l1c", ones_cols=ones1c)[0]
    mix1 = _odd_mixer_call(qkv1, ckv, l1_rpb * math.log2(math.e), du, l1_pool_w.astype(BF16),
                           l1_pool_scale.reshape(1, -1))
    (x,) = _ffn_sublayer([(mix1, l1_w_out.astype(BF16), x, g1, sc2, sh2, g2)],
                         None, ln_g[1], ln_b[1], moe_router[1], moe_w_gate, moe_w_up, moe_w_down, 1)
    return x
```

```python
import functools
import math

import numpy as np

import jax
import jax.numpy as jnp
from jax import lax
from jax.experimental import pallas as pl
from jax.experimental.pallas import tpu as pltpu

F32 = jnp.float32
BF16 = jnp.bfloat16

DEPTH = 2
GRID_W = 64
HEAD_DIM = 64
A_HEADS = 4
B_HEADS = 8
B_KV_HEADS = 2
C_HEADS = 8
D_GROUPS = 4
D_GROUP_DIM = 128
POOL_WINDOWS = (2, 4, 8, 16)
NA_ROWS = 8
NA_COLS = 16
N_EXPERTS = 16
EC_FACTOR = 2
ROPE_THETA = 10000.0
LN_EPS = 1e-5
RMS_EPS = 1e-6
DEEPNORM_ALPHA = (2 * DEPTH) ** 0.25
Q_SCALE = HEAD_DIM ** -0.5

LANES = 128
VMEM_LIMIT = 56 * 1024 * 1024

NA_QROWS = 4
NEG_BIG = -1e30

_NN = (((1,), (0,)), ((), ()))
_NT = (((1,), (1,)), ((), ()))


def _dot(a, b, dims=_NN):
    return lax.dot_general(a, b, dims, preferred_element_type=F32)


def _split_bf16(a):
    hi = a.astype(BF16)
    lo = (a - hi.astype(F32)).astype(BF16)
    return hi, lo


def _dot3(a, b, dims=_NN):
    a_hi, a_lo = _split_bf16(a)
    b_hi, b_lo = _split_bf16(b)
    return _dot(a_hi, b_hi, dims) + (_dot(a_hi, b_lo, dims) + _dot(a_lo, b_hi, dims))


def _params(sem):
    return pltpu.CompilerParams(dimension_semantics=sem, vmem_limit_bytes=VMEM_LIMIT)


def _layer_norm(z, g, b):
    mu = jnp.mean(z, axis=-1, keepdims=True)
    zc = z - mu
    var = jnp.mean(zc * zc, axis=-1, keepdims=True)
    return zc * lax.rsqrt(var + LN_EPS) * g + b


def _lane_masks():
    lane = lax.broadcasted_iota(jnp.int32, (1, LANES), 1)
    lo = jnp.where(lane < HEAD_DIM, 1.0, 0.0).astype(F32)
    return lo, 1.0 - lo


def _ada_kernel(c_ref, w_ref, b_ref, o_ref):
    c = c_ref[...]
    s = c * (1.0 / (1.0 + jnp.exp(-c)))
    o_ref[0] = _dot3(s, w_ref[0]) + b_ref[0]


def _ada_call(cc, ada_w, ada_b):
    depth, d, n = ada_w.shape
    rows = cc.shape[0]
    tn = 1536 if n % 1536 == 0 else n
    return pl.pallas_call(
        _ada_kernel,
        out_shape=jax.ShapeDtypeStruct((depth, rows, n), F32),
        grid=(depth, n // tn),
        in_specs=[pl.BlockSpec((rows, d), lambda l, j: (0, 0)),
                  pl.BlockSpec((1, d, tn), lambda l, j: (l, 0, j)),
                  pl.BlockSpec((1, 1, tn), lambda l, j: (l, 0, j))],
        out_specs=pl.BlockSpec((1, rows, tn), lambda l, j: (l, 0, j)),
        compiler_params=_params(("arbitrary", "arbitrary")),
        name="ada_mod",
    )(cc, ada_w, ada_b.reshape(depth, 1, n))


def _inproj_kernel(x_ref, sc_ref, sh_ref, w_ref, cos_ref, sinp_ref, sinm_ref, gn_ref, gmat_ref, *o_refs,
                   blocks, ones_cols, chunk, use_rope):
    h = (x_ref[0] * (1.0 + sc_ref[0]) + sh_ref[0]).astype(BF16)
    n = len(blocks) * LANES
    for oi, oc in ones_cols:
        o_refs[oi][0, :, oc:oc + LANES] = jnp.ones((h.shape[0], LANES), o_refs[oi].dtype)
    starts = list(range(0, n, chunk))
    accs = {starts[0]: _dot(h, w_ref[:, starts[0]:min(starts[0] + chunk, n)])}
    for ci, c0 in enumerate(starts):
        cw = min(chunk, n - c0)
        if ci + 1 < len(starts):
            nxt = starts[ci + 1]
            accs[nxt] = _dot(h, w_ref[:, nxt:min(nxt + chunk, n)])
        acc = accs.pop(c0)
        for j in range(cw // LANES):
            kind, gain_row, factor, oi, oc = blocks[(c0 // LANES) + j]
            v = acc[:, j * LANES:(j + 1) * LANES]
            if kind == "norm":
                v2 = v * v
                hi, lo = _split_bf16(v2)
                ms = _dot(hi, gmat_ref[...]) + _dot(lo, gmat_ref[...])
                v = v * lax.rsqrt(ms + RMS_EPS) * gn_ref[gain_row:gain_row + 1, :]
            if kind in ("rope", "norm") and use_rope:
                v = (v * cos_ref[...] + pltpu.roll(v, 16, 1) * sinp_ref[...]
                     + pltpu.roll(v, LANES - 16, 1) * sinm_ref[...])
            if factor != 1.0:
                v = v * factor
            o_refs[oi][0, :, oc:oc + LANES] = v.astype(o_refs[oi].dtype)


def _inproj_call(x, sc, sh, w, blocks, out_defs, rope_tabs, gains, gmat, *, use_rope, name, ones_cols=()):
    b, s, d = x.shape
    n = w.shape[1]
    tm = min(512, s)
    per_b = sc.shape[0] == b
    mod_map = (lambda bi, i: (bi, 0, 0)) if per_b else (lambda bi, i: (0, 0, 0))
    cos, sinp, sinm = rope_tabs
    tab_spec = pl.BlockSpec((tm, LANES), lambda bi, i: (i, 0))
    kern = functools.partial(_inproj_kernel, blocks=tuple(blocks), ones_cols=tuple(ones_cols), chunk=512,
                             use_rope=use_rope)
    return pl.pallas_call(
        kern,
        out_shape=[jax.ShapeDtypeStruct((b, s, nc), dt) for nc, dt in out_defs],
        grid=(b, s // tm),
        in_specs=[pl.BlockSpec((1, tm, d), lambda bi, i: (bi, i, 0)),
                  pl.BlockSpec((1, 1, d), mod_map),
                  pl.BlockSpec((1, 1, d), mod_map),
                  pl.BlockSpec((d, n), lambda bi, i: (0, 0)),
                  tab_spec, tab_spec, tab_spec,
                  pl.BlockSpec(gains.shape, lambda bi, i: (0, 0)),
                  pl.BlockSpec(gmat.shape, lambda bi, i: (0, 0))],
        out_specs=[pl.BlockSpec((1, tm, nc), lambda bi, i: (bi, i, 0)) for nc, _ in out_defs],
        compiler_params=_params(("arbitrary", "arbitrary")),
        name=name,
    )(x, sc, sh, w, cos, sinp, sinm, gains, gmat)


def _rope_tables(s):
    n_freq = HEAD_DIM // 4
    t = jnp.arange(s, dtype=jnp.int32)
    inv = ROPE_THETA ** (-jnp.arange(n_freq, dtype=F32) / n_freq)
    ang_r = (t // GRID_W).astype(F32)[:, None] * inv
    ang_c = (t % GRID_W).astype(F32)[:, None] * inv
    ang = jnp.concatenate([ang_r, ang_r, ang_c, ang_c] * (LANES // HEAD_DIM), axis=-1)
    first = (jnp.arange(LANES) % 32) < 16
    cos, sin = jnp.cos(ang), jnp.sin(ang)
    return cos, jnp.where(first, 0.0, sin), jnp.where(first, -sin, 0.0)


EV_AQ, EV_AK, EV_AV = 0, 512, 1024
EV_BQ = EV_AV + A_HEADS * 2 * LANES
EV_BK = EV_BQ + (B_HEADS // 2) * LANES
EV_BV = EV_BK + LANES
EV_WIDTH = EV_BV + 2 * LANES


def _attn_even_kernel(q_ref, lam_ref, sg_ref, *refs, n_kv, tq, lam_init):
    kv_refs, o_ref = refs[:n_kv], refs[n_kv]
    lo, hi = _lane_masks()
    lo_b, hi_b = lo.astype(BF16), hi.astype(BF16)
    lv = lam_ref[...]
    lam = (jnp.exp(jnp.sum(lv[0:1] * lv[1:2], axis=-1, keepdims=True))
           - jnp.exp(jnp.sum(lv[2:3] * lv[3:4], axis=-1, keepdims=True)) + lam_init)

    def scores(q, kcol):
        qq = jnp.concatenate([q * lo_b, q * hi_b], axis=0)
        ss = [_dot(qq, kv[0, :, kcol:kcol + LANES], _NT) for kv in kv_refs]
        m = ss[0].max(axis=-1, keepdims=True)
        for s in ss[1:]:
            m = jnp.maximum(m, s.max(axis=-1, keepdims=True))
        return ss, m

    def weighted(ss, m, vcol):
        acc = None
        for s, kv in zip(ss, kv_refs):
            part = _dot(jnp.exp2(s - m).astype(BF16), kv[0, :, vcol:vcol + 2 * LANES])
            acc = part if acc is None else acc + part
        return acc[:, :LANES] * (1.0 / acc[:, LANES:])

    units = ([("a", h, EV_AQ + h * LANES, EV_AK + h * LANES, EV_AV + h * 2 * LANES) for h in range(A_HEADS)]
             + [("b", j, EV_BQ + j * LANES, EV_BK, EV_BV) for j in range(B_HEADS // 2)])

    def finish(unit, ss, m):
        kind, idx, _, _, vcol = unit
        on = weighted(ss, m, vcol)
        if kind == "a":
            o = on[:tq] - lam * on[tq:]
            ms = jnp.mean(o * o, axis=-1, keepdims=True)
            o = o * lax.rsqrt(ms + RMS_EPS) * sg_ref[...] * (1.0 - lam_init)
            o_ref[0, :, idx * LANES:(idx + 1) * LANES] = o.astype(o_ref.dtype)
        else:
            o = on[:tq] * lo + on[tq:] * hi
            o_ref[0, :, 512 + idx * LANES:512 + (idx + 1) * LANES] = o.astype(o_ref.dtype)

    pending = None
    for unit in units:
        cur = scores(q_ref[0, :, unit[2]:unit[2] + LANES], unit[3])
        if pending is not None:
            finish(*pending)
        pending = (unit,) + cur
    finish(*pending)


def _attn_even_call(q_arr, kv_arrs, lamv, subln_g, lam_init, name):
    b, sq, n = q_arr.shape
    tq = min(256, sq)
    kern = functools.partial(_attn_even_kernel, n_kv=len(kv_arrs), tq=tq, lam_init=lam_init)
    return pl.pallas_call(
        kern,
        out_shape=jax.ShapeDtypeStruct((b, sq, 1024), BF16),
        grid=(b, sq // tq),
        in_specs=[pl.BlockSpec((1, tq, n), lambda bi, i: (bi, i, 0)),
                  pl.BlockSpec(lamv.shape, lambda bi, i: (0, 0)),
                  pl.BlockSpec(subln_g.shape, lambda bi, i: (0, 0))]
                 + [pl.BlockSpec((1,) + a.shape[1:], lambda bi, i: (bi, 0, 0)) for a in kv_arrs],
        out_specs=pl.BlockSpec((1, tq, 1024), lambda bi, i: (bi, i, 0)),
        compiler_params=_params(("arbitrary", "arbitrary")),
        name=name,
    )(q_arr, lamv, subln_g, *kv_arrs)


def _odd_mixer_kernel(qkv_ref, ckv_ref, bias_ref, du_ref, band_ref, edge_ref, pw_ref, ps_ref, o_ref, *,
                      tq, slab_rows, n_rows, n_blk, seq):
    i = pl.program_id(1)
    lo, hi = _lane_masks()
    lo_b, hi_b = lo.astype(BF16), hi.astype(BF16)
    t0 = pl.multiple_of(i * tq, tq)
    base = jnp.clip(i * NA_QROWS - NA_ROWS // 2, 0, n_rows - slab_rows)
    k0 = pl.multiple_of(base * GRID_W, GRID_W)
    nk = slab_rows * GRID_W
    cw = C_HEADS * HEAD_DIM

    def scores(j):
        cs = slice(j * LANES, (j + 1) * LANES)
        q = qkv_ref[0, pl.ds(t0, tq), cs]
        qq = jnp.concatenate([q * lo_b, q * hi_b], axis=0)
        kl = qkv_ref[0, pl.ds(k0, nk), cw + j * LANES:cw + (j + 1) * LANES]
        bias = jnp.concatenate([bias_ref[2 * j], bias_ref[2 * j + 1]], axis=0)
        s_l = _dot(qq, kl, _NT) + bias
        s_c = _dot(qq, ckv_ref[0, :, cs], _NT)
        m = jnp.maximum(s_l.max(axis=-1, keepdims=True), s_c.max(axis=-1, keepdims=True))
        return j, s_l, s_c, m

    def finish(j, s_l, s_c, m):
        vl = qkv_ref[0, pl.ds(k0, nk), 2 * cw + 2 * j * LANES:2 * cw + 2 * (j + 1) * LANES]
        vc = ckv_ref[0, :, cw + 2 * j * LANES:cw + 2 * (j + 1) * LANES]
        acc = _dot(jnp.exp2(s_l - m).astype(BF16), vl) + _dot(jnp.exp2(s_c - m).astype(BF16), vc)
        on = acc[:, :LANES] * (1.0 / acc[:, LANES:])
        o = on[:tq] * lo + on[tq:] * hi
        o_ref[0, :, j * LANES:(j + 1) * LANES] = o.astype(o_ref.dtype)

    pending = None
    for j in range(C_HEADS // 2):
        cur = scores(j)
        if pending is not None:
            finish(*pending)
        pending = cur
    finish(*pending)

    tprev = pl.multiple_of(jnp.maximum(i - 1, 0) * tq, tq)
    tnext = pl.multiple_of(jnp.minimum(i + 1, n_blk - 1) * tq, tq)
    has_prev = jnp.where(i > 0, 1.0, 0.0).astype(F32)
    has_next = jnp.where(i < n_blk - 1, 1.0, 0.0).astype(F32)
    tpos = t0 + lax.broadcasted_iota(jnp.int32, (tq, 1), 0)
    for g in range(D_GROUPS):
        half = POOL_WINDOWS[g] // 2
        gs = slice(g * D_GROUP_DIM, (g + 1) * D_GROUP_DIM)
        cur = du_ref[0, pl.ds(t0, tq), gs]
        prv = du_ref[0, pl.ds(tprev + (tq - POOL_EDGE), POOL_EDGE), gs]
        nxt = du_ref[0, pl.ds(tnext, POOL_EDGE), gs]

        def band_sum(band, u):
            u_hi, u_lo = _split_bf16(u)
            return _dot(band, u_hi) + _dot(band, u_lo)

        wsum = band_sum(band_ref[g], cur)
        wsum = jnp.concatenate([wsum[:POOL_EDGE] + has_prev * band_sum(edge_ref[g, 0], prv),
                                wsum[POOL_EDGE:tq - POOL_EDGE],
                                wsum[tq - POOL_EDGE:] + has_next * band_sum(edge_ref[g, 1], nxt)], axis=0)
        cnt = (jnp.minimum(tpos + half, seq) - jnp.maximum(tpos - half, 0)).astype(F32)
        pooled = wsum * (1.0 / cnt) - cur
        od = _dot(pooled.astype(BF16), pw_ref[g]) * ps_ref[:, gs]
        o_ref[0, :, cw + g * D_GROUP_DIM:cw + (g + 1) * D_GROUP_DIM] = od.astype(o_ref.dtype)


def _na_geometry(n_rows):
    kh = min(NA_ROWS, n_rows)
    slab = min(n_rows, NA_QROWS + kh)
    n_blk = n_rows // NA_QROWS
    bases = [min(max(i * NA_QROWS - NA_ROWS // 2, 0), n_rows - slab) for i in range(n_blk)]
    sigs, type_of = [], []
    for i in range(n_blk):
        sig = tuple((min(max(i * NA_QROWS + r - kh // 2, 0), n_rows - kh) - bases[i],
                     i * NA_QROWS + r - bases[i]) for r in range(NA_QROWS))
        if sig not in sigs:
            sigs.append(sig)
        type_of.append(sigs.index(sig))
    return kh, slab, n_blk, sigs, type_of


def _na_bias_table(rpb, n_rows):
    kh, slab, _, sigs, _ = _na_geometry(n_rows)
    kw = NA_COLS
    n_dc = 2 * NA_COLS - 1
    w = np.arange(GRID_W)
    cstart = np.clip(w - kw // 2, 0, GRID_W - kw)
    col_ok = (w[None, :] >= cstart[:, None]) & (w[None, :] < cstart[:, None] + kw)
    dc = np.clip(w[None, :] - w[:, None] + (NA_COLS - 1), 0, n_dc - 1)
    n_dr = 2 * NA_ROWS - 1
    heads = rpb.shape[0]
    onehot = ((dc[None] == np.arange(n_dc)[:, None, None]) & col_ok[None]).astype(np.float32)
    toep = jnp.einsum("hrd,dwj->hwrj", rpb, jnp.asarray(onehot), precision=lax.Precision.HIGHEST)
    toep = jnp.where(jnp.asarray(col_ok)[:, None, :], toep, NEG_BIG)
    pad = slab + NA_QROWS
    toep = jnp.pad(toep, ((0, 0), (0, 0), (pad, pad), (0, 0)), constant_values=NEG_BIG)
    toep = toep.reshape(heads, GRID_W, -1)
    blocks = []
    for sig in sigs:
        for rs_rel, qr_rel in sig:
            d0 = pad - qr_rel + (NA_ROWS - 1)
            in_win = np.repeat(np.array([rs_rel <= m < rs_rel + kh for m in range(slab)]), GRID_W)
            blk = lax.slice_in_dim(toep, d0 * GRID_W, (d0 + slab) * GRID_W, axis=2)
            blocks.append(jnp.where(jnp.asarray(in_win), blk, NEG_BIG))
    big = jnp.stack(blocks, axis=1)
    return big.reshape(heads, len(sigs), NA_QROWS * GRID_W, slab * GRID_W)


POOL_EDGE = 16


def _pool_bands(tq):
    t = jnp.arange(tq, dtype=jnp.int32)[:, None]
    sidx = jnp.arange(tq, dtype=jnp.int32)[None, :]
    main, edge = [], []
    for wdw in POOL_WINDOWS:
        half = wdw // 2
        full = [((sidx + (m - 1) * tq >= t - half) & (sidx + (m - 1) * tq < t + half)).astype(BF16)
                for m in range(3)]
        main.append(full[1])
        edge.append(jnp.stack([full[0][:POOL_EDGE, tq - POOL_EDGE:], full[2][tq - POOL_EDGE:, :POOL_EDGE]]))
    return jnp.stack(main), jnp.stack(edge)


def _odd_mixer_call(qkv, ckv, rpb, du, pool_w, pool_scale):
    b, s, n = qkv.shape
    n_rows = s // GRID_W
    _, slab, n_blk, _, type_of = _na_geometry(n_rows)
    tq = NA_QROWS * GRID_W
    bias = _na_bias_table(rpb, n_rows)
    bands, edges = _pool_bands(tq)

    def bias_map(bi, i):
        t = jnp.int32(type_of[-1])
        for blk in range(n_blk - 2, -1, -1):
            t = jnp.where(i == blk, jnp.int32(type_of[blk]), t)
        return (0, t, 0, 0)

    kern = functools.partial(_odd_mixer_kernel, tq=tq, slab_rows=slab, n_rows=n_rows, n_blk=n_blk, seq=s)
    return pl.pallas_call(
        kern,
        out_shape=jax.ShapeDtypeStruct((b, s, 1024), BF16),
        grid=(b, n_blk),
        in_specs=[pl.BlockSpec((1, s, n), lambda bi, i: (bi, 0, 0)),
                  pl.BlockSpec((1,) + ckv.shape[1:], lambda bi, i: (bi, 0, 0)),
                  pl.BlockSpec((bias.shape[0], None) + bias.shape[2:], bias_map),
                  pl.BlockSpec((1, s, du.shape[2]), lambda bi, i: (bi, 0, 0)),
                  pl.BlockSpec(bands.shape, lambda bi, i: (0, 0, 0)),
                  pl.BlockSpec(edges.shape, lambda bi, i: (0, 0, 0, 0)),
                  pl.BlockSpec(pool_w.shape, lambda bi, i: (0, 0, 0)),
                  pl.BlockSpec(pool_scale.shape, lambda bi, i: (0, 0))],
        out_specs=pl.BlockSpec((1, tq, 1024), lambda bi, i: (bi, i, 0)),
        compiler_params=_params(("arbitrary", "arbitrary")),
        name="odd_mixer",
    )(qkv, ckv, bias, du, bands, edges, pool_w, pool_scale)


OUTPROJ_SUB = 256


def _outproj_kernel(mix_ref, w_ref, x_ref, g1_ref, lng_ref, lnb_ref, sc_ref, sh_ref, r2_ref, rh_ref,
                    x1_ref, hm_ref, aff_ref, *, n_exp):
    tm = x_ref.shape[1]
    subs = [slice(r0, r0 + OUTPROJ_SUB) for r0 in range(0, tm, OUTPROJ_SUB)]
    ys = [_dot(mix_ref[0, rows, :], w_ref[...]) for rows in subs]
    for rows, y in zip(subs, ys):
        x1 = _layer_norm(DEEPNORM_ALPHA * x_ref[0, rows, :] + g1_ref[0] * y, lng_ref[...], lnb_ref[...])
        x1_ref[0, rows, :] = x1
        hm = x1 * (1.0 + sc_ref[0]) + sh_ref[0]
        hm_hi = hm.astype(BF16)
        hm_ref[0, rows, :] = hm_hi
        hm_lo = (hm - hm_hi.astype(F32)).astype(BF16)
        part = _dot(hm_hi, r2_ref[...])
        logits = part[:, :n_exp] + part[:, n_exp:] + _dot(hm_lo, rh_ref[...])
        ex = jnp.exp(logits - logits.max(axis=-1, keepdims=True))
        aff_ref[0, rows, :] = ex * (1.0 / ex.sum(axis=-1, keepdims=True))


def _outproj_call(mix, w, x, g1, lng, lnb, sc2, sh2, router, name):
    b, s, d = x.shape
    dm = mix.shape[2]
    e = router.shape[1]
    tm = min(2 * OUTPROJ_SUB, s)
    r_hi = router.astype(BF16)
    r_lo = (router - r_hi.astype(F32)).astype(BF16)
    r2 = jnp.concatenate([r_hi, r_lo], axis=1)
    per_b = g1.shape[0] == b
    mod_map = (lambda bi, i: (bi, 0, 0)) if per_b else (lambda bi, i: (0, 0, 0))
    vec = pl.BlockSpec((1, 1, d), mod_map)
    row = pl.BlockSpec((1, d), lambda bi, i: (0, 0))
    tile = lambda n: pl.BlockSpec((1, tm, n), lambda bi, i: (bi, i, 0))
    return pl.pallas_call(
        functools.partial(_outproj_kernel, n_exp=e),
        out_shape=[jax.ShapeDtypeStruct((b, s, d), F32),
                   jax.ShapeDtypeStruct((b, s, d), BF16),
                   jax.ShapeDtypeStruct((b, s, e), F32)],
        grid=(b, s // tm),
        in_specs=[tile(dm),
                  pl.BlockSpec((dm, d), lambda bi, i: (0, 0)),
                  tile(d),
                  vec, row, row, vec, vec,
                  pl.BlockSpec((d, 2 * e), lambda bi, i: (0, 0)),
                  pl.BlockSpec((d, e), lambda bi, i: (0, 0))],
        out_specs=[tile(d), tile(d), tile(e)],
        compiler_params=_params(("arbitrary", "arbitrary")),
        name=name,
    )(mix, w, x, g1, lng, lnb, sc2, sh2, r2, r_hi)


def _lane_cumsum(m):
    rows, s = m.shape
    r_i = lax.broadcasted_iota(jnp.int32, (LANES, LANES), 0)
    c_i = lax.broadcasted_iota(jnp.int32, (LANES, LANES), 1)
    tri = jnp.where(r_i <= c_i, 1.0, 0.0).astype(BF16)
    carry = jnp.zeros((rows, 1), F32)
    out = []
    for c in range(s // LANES):
        blk = m[:, c * LANES:(c + 1) * LANES]
        out.append(_dot(blk.astype(BF16), tri) + carry)
        carry = carry + blk.sum(axis=-1, keepdims=True)
    return jnp.concatenate(out, axis=-1)


ROUTE_TILE = 256


def _route_kernel(aff_ref, pos_ref, cnt_ref, *, cap):
    a = aff_ref[...]
    thr = jnp.zeros((a.shape[0], 1), jnp.int32)
    for bit in range(30, -1, -1):
        cand = thr | jnp.int32(1 << bit)
        cnt = jnp.where(a >= pltpu.bitcast(cand, F32), 1.0, 0.0).sum(axis=-1, keepdims=True)
        thr = jnp.where(cnt >= cap, cand, thr)
    gt = jnp.where(a >= pltpu.bitcast(thr + 1, F32), 1.0, 0.0)
    eq = jnp.where(a >= pltpu.bitcast(thr, F32), 1.0, 0.0) - gt
    need = cap - gt.sum(axis=-1, keepdims=True)
    sel = gt + eq * jnp.where(_lane_cumsum(eq) <= need, 1.0, 0.0)
    pos_ref[...] = jnp.where(sel > 0.5, _lane_cumsum(sel) - 1.0, -1.0)
    tok = lax.broadcasted_iota(jnp.int32, (a.shape[1], LANES), 0)
    tile = lax.broadcasted_iota(jnp.int32, (a.shape[1], LANES), 1)
    before = jnp.where(tok < tile * ROUTE_TILE, 1.0, 0.0).astype(BF16)
    cnt_ref[...] = _dot(sel.astype(BF16), before)


def _route_call(aff, cap, name):
    b, e, s = aff.shape
    n_tiles = s // min(ROUTE_TILE, s)
    full = pl.BlockSpec((b * e, s), lambda i: (0, 0))
    pos, cnt = pl.pallas_call(
        functools.partial(_route_kernel, cap=cap),
        out_shape=[jax.ShapeDtypeStruct((b * e, s), F32), jax.ShapeDtypeStruct((b * e, LANES), F32)],
        grid=(1,),
        in_specs=[full],
        out_specs=[full, pl.BlockSpec((b * e, LANES), lambda i: (0, 0))],
        compiler_params=_params(("arbitrary",)),
        name=name,
    )(aff.reshape(b * e, s))
    return pos.reshape(b, e, s), cnt[:, :n_tiles + 1].astype(jnp.int32).reshape(-1)


def _gather_kernel(cnt_ref, pos_ref, aff_ref, hm_ref, xg_ref, gs_ref, *, cap, n_exp, win, n_tiles):
    bi, i = pl.program_id(0), pl.program_id(1)

    @pl.when(i == 0)
    def _():
        xg_ref[...] = jnp.zeros(xg_ref.shape, xg_ref.dtype)
        gs_ref[...] = jnp.zeros(gs_ref.shape, gs_ref.dtype)

    starts, short = [], None
    for ei in range(n_exp):
        base = (bi * n_exp + ei) * (n_tiles + 1) + i
        st = jnp.minimum((cnt_ref[base] // SLOT_ALIGN) * SLOT_ALIGN, cap - win)
        starts.append(pl.multiple_of(st, SLOT_ALIGN))
        miss = cnt_ref[base + 1] > st + win
        short = miss if short is None else jnp.logical_or(short, miss)

    def scatter_rows(n_rows, first, stacked):
        row = lax.broadcasted_iota(jnp.int32, (n_rows, 1), 0).astype(F32)
        blocks = []
        for ei in range(n_exp):
            rel = pos_ref[0, ei:ei + 1, :] if first is None else pos_ref[0, ei:ei + 1, :] - first[ei].astype(F32)
            hit = rel == row
            blocks.append(jnp.where(hit, 1.0, 0.0).astype(BF16))
            gate = jnp.where(hit, aff_ref[0, ei:ei + 1, :], 0.0).sum(axis=-1, keepdims=True)
            rows = slice(None) if first is None else pl.ds(first[ei], n_rows)
            gs_ref[0, ei, rows, :] += gate
        if stacked:
            part = _dot(jnp.concatenate(blocks, axis=0), hm_ref[0])
        for ei in range(n_exp):
            rows = slice(None) if first is None else pl.ds(first[ei], n_rows)
            sub = part[ei * n_rows:(ei + 1) * n_rows] if stacked else _dot(blocks[ei], hm_ref[0])
            xg_ref[0, ei, rows, :] += sub.astype(xg_ref.dtype)

    @pl.when(jnp.logical_not(short))
    def _():
        scatter_rows(win, starts, True)

    @pl.when(short)
    def _():
        scatter_rows(cap, None, False)


def _gather_call(cnt, pos, aff, hm, cap, name):
    b, e, s = pos.shape
    d = hm.shape[2]
    tm = min(ROUTE_TILE, s)
    n_tiles = s // tm
    win = min(64, cap)
    tok = pl.BlockSpec((1, e, tm), lambda bi, i, c: (bi, 0, i))
    return pl.pallas_call(
        functools.partial(_gather_kernel, cap=cap, n_exp=e, win=win, n_tiles=n_tiles),
        out_shape=[jax.ShapeDtypeStruct((b, e, cap, d), BF16),
                   jax.ShapeDtypeStruct((b, e, cap, 1), F32)],
        grid_spec=pltpu.PrefetchScalarGridSpec(
            num_scalar_prefetch=1,
            grid=(b, n_tiles),
            in_specs=[tok, tok, pl.BlockSpec((1, tm, d), lambda bi, i, c: (bi, i, 0))],
            out_specs=[pl.BlockSpec((1, e, cap, d), lambda bi, i, c: (bi, 0, 0, 0)),
                       pl.BlockSpec((1, e, cap, 1), lambda bi, i, c: (bi, 0, 0, 0))]),
        compiler_params=_params(("arbitrary", "arbitrary")),
        name=name,
    )(cnt, pos, aff, hm)


def _ffn_kernel(*refs, n_grp, row_chunks):
    xg_refs = refs[:n_grp]
    gs_refs = refs[n_grp:2 * n_grp]
    wg_ref, wu_ref, wd_ref = refs[2 * n_grp:2 * n_grp + 3]
    y_refs = refs[2 * n_grp + 3:3 * n_grp + 3]
    acc_refs = refs[3 * n_grp + 3:]
    f = pl.program_id(1)
    last = pl.num_programs(1) - 1
    wg = wg_ref[0].astype(BF16)
    wu = wu_ref[0].astype(BF16)
    wd = wd_ref[0].astype(BF16)

    @pl.when(f == 0)
    def _():
        for acc_ref in acc_refs:
            acc_ref[...] = jnp.zeros(acc_ref.shape, F32)

    for xg_ref, acc_ref, nb in zip(xg_refs, acc_refs, row_chunks):
        bt, _, cap, d = xg_ref.shape
        for b0 in range(0, bt, nb):
            rows = nb * cap
            r0 = b0 * cap
            x = xg_ref[b0:b0 + nb, 0].reshape(rows, d)
            hg = _dot(x, wg)
            hu = _dot(x, wu)
            hid = (hg * (1.0 / (1.0 + jnp.exp(-hg))) * hu).astype(BF16)
            acc_ref[r0:r0 + rows, :] += _dot(hid, wd)

    @pl.when(f == last)
    def _():
        for xg_ref, gs_ref, y_ref, acc_ref in zip(xg_refs, gs_refs, y_refs, acc_refs):
            bt, _, cap, d = xg_ref.shape
            gate = gs_ref[:, 0].reshape(bt * cap, 1)
            y_ref[:, 0] = (acc_ref[...] * gate).reshape(bt, cap, d).astype(y_ref.dtype)


def _ffn_call(xgs, gss, w_gate, w_up, w_down, layer, name):
    _, e, d, ff = w_gate.shape
    tf = 512 if ff % 512 == 0 else ff
    n_grp = len(xgs)
    row_chunks = []
    for xg in xgs:
        bt, _, cap, _ = xg.shape
        nb = max(1, min(bt, 512 // cap))
        while bt % nb:
            nb -= 1
        row_chunks.append(nb)
    tok = lambda a: pl.BlockSpec((a.shape[0], 1) + a.shape[2:], lambda ei, fi: (0, ei, 0, 0))
    kern = functools.partial(_ffn_kernel, n_grp=n_grp, row_chunks=tuple(row_chunks))
    return pl.pallas_call(
        kern,
        out_shape=[jax.ShapeDtypeStruct(xg.shape, BF16) for xg in xgs],
        grid=(e, ff // tf),
        in_specs=[tok(a) for a in xgs] + [tok(a) for a in gss]
                 + [pl.BlockSpec((None, 1, d, tf), lambda ei, fi: (layer, ei, 0, fi)),
                    pl.BlockSpec((None, 1, d, tf), lambda ei, fi: (layer, ei, 0, fi)),
                    pl.BlockSpec((None, 1, tf, d), lambda ei, fi: (layer, ei, fi, 0))],
        out_specs=[tok(a) for a in xgs],
        scratch_shapes=[pltpu.VMEM((xg.shape[0] * xg.shape[2], d), F32) for xg in xgs],
        compiler_params=_params(("arbitrary", "arbitrary")),
        name=name,
    )(*xgs, *gss, w_gate, w_up, w_down)


MXU_DEPTH = 256
SLOT_ALIGN = 16


def _combine_kernel(cnt_ref, pos_ref, y_ref, x_ref, g_ref, lng_ref, lnb_ref, o_ref, *, cap, n_exp, win, n_tiles):
    bi, i = pl.program_id(0), pl.program_id(1)
    pos = pos_ref[0]
    tm = pos.shape[0]
    grp = MXU_DEPTH // win
    starts, short = [], None
    for ei in range(n_exp):
        base = (bi * n_exp + ei) * (n_tiles + 1) + i
        st = jnp.minimum((cnt_ref[base] // SLOT_ALIGN) * SLOT_ALIGN, cap - win)
        starts.append(st)
        miss = cnt_ref[base + 1] > st + win
        short = miss if short is None else jnp.logical_or(short, miss)

    def windowed():
        lane = lax.broadcasted_iota(jnp.int32, (1, grp * win), 1)
        lane_f = lane.astype(F32)
        acc = None
        for k in range(n_exp // grp):
            tgt, rows = None, []
            for u in range(grp - 1, -1, -1):
                ei = k * grp + u
                st = starts[ei]
                rel = pos[:, ei:ei + 1] - st.astype(F32)
                rel = jnp.where(rel >= 0.0, jnp.where(rel < win, rel + float(u * win), -1.0), -1.0)
                tgt = rel if tgt is None else jnp.where(lane < (u + 1) * win, rel, tgt)
                rows.insert(0, y_ref[0, pl.ds(pl.multiple_of(ei * cap + st, SLOT_ALIGN), win), :])
            onehot = jnp.where(tgt == lane_f, 1.0, 0.0).astype(BF16)
            part = _dot(onehot, jnp.concatenate(rows, axis=0))
            acc = part if acc is None else acc + part
        return acc

    def dense():
        slot = lax.broadcasted_iota(jnp.int32, (1, cap), 1).astype(F32)
        acc = None
        for ei in range(n_exp):
            onehot = jnp.where(pos[:, ei:ei + 1] == slot, 1.0, 0.0).astype(BF16)
            part = _dot(onehot, y_ref[0, ei * cap:(ei + 1) * cap, :])
            acc = part if acc is None else acc + part
        return acc

    acc = lax.cond(short, dense, windowed)
    o_ref[0] = _layer_norm(DEEPNORM_ALPHA * x_ref[0] + g_ref[0] * acc, lng_ref[...], lnb_ref[...])


def _combine_call(cnt, pos_t, y, x, g2, lng, lnb, cap, name):
    b, s, d = x.shape
    e = pos_t.shape[2]
    tm = min(ROUTE_TILE, s)
    n_tiles = s // tm
    win = min(64, cap)
    per_b = g2.shape[0] == b
    mod_map = (lambda bi, i, c: (bi, 0, 0)) if per_b else (lambda bi, i, c: (0, 0, 0))
    row = pl.BlockSpec((1, d), lambda bi, i, c: (0, 0))
    return pl.pallas_call(
        functools.partial(_combine_kernel, cap=cap, n_exp=e, win=win, n_tiles=n_tiles),
        out_shape=jax.ShapeDtypeStruct((b, s, d), F32),
        grid_spec=pltpu.PrefetchScalarGridSpec(
            num_scalar_prefetch=1,
            grid=(b, n_tiles),
            in_specs=[pl.BlockSpec((1, tm, e), lambda bi, i, c: (bi, i, 0)),
                      pl.BlockSpec((1, e * cap, d), lambda bi, i, c: (bi, 0, 0)),
                      pl.BlockSpec((1, tm, d), lambda bi, i, c: (bi, i, 0)),
                      pl.BlockSpec((1, 1, d), mod_map), row, row],
            out_specs=pl.BlockSpec((1, tm, d), lambda bi, i, c: (bi, i, 0))),
        compiler_params=_params(("arbitrary", "arbitrary")),
        name=name,
    )(cnt, pos_t, y.reshape(b, e * cap, d), x, g2, lng, lnb)


def _ffn_sublayer(streams, mods, lng, lnb, router, w_gate, w_up, w_down, layer):
    staged = []
    for si, (mix, w_out, x, g1, sc2, sh2, g2) in enumerate(streams):
        tag = f"l{layer}s{si}"
        n_tok = x.shape[1]
        cap = EC_FACTOR * n_tok // N_EXPERTS
        x1, hm, aff_t = _outproj_call(mix, w_out, x, g1, lng[0:1], lnb[0:1], sc2, sh2, router, "outproj_" + tag)
        aff = jnp.swapaxes(aff_t, 1, 2)
        pos, cnt = _route_call(aff, cap, "route_" + tag)
        xg, gs = _gather_call(cnt, pos, aff, hm, cap, "gather_" + tag)
        staged.append((x1, pos, cnt, xg, gs, g2, cap, tag))
    ys = _ffn_call([st[3] for st in staged], [st[4] for st in staged], w_gate, w_up, w_down, layer,
                   f"ffn_l{layer}")
    outs = []
    for (x1, pos, cnt, _, _, g2, cap, tag), y in zip(staged, ys):
        outs.append(_combine_call(cnt, jnp.swapaxes(pos, 1, 2), y, x1, g2, lng[1:2], lnb[1:2], cap,
                                  "combine_" + tag))
    return outs


def kernel(x, c, ctx, c_ctx, ada_w, ada_b, ln_g, ln_b, l0_w_in, l0_w_out, l0_lam_q1, l0_lam_k1, l0_lam_q2,
           l0_lam_k2, l0_subln_g, l0_qnorm_g, l0_knorm_g, l1_w_in, l1_w_out, l1_rpb, l1_pool_w, l1_pool_scale,
           moe_router, moe_w_gate, moe_w_up, moe_w_down):
    b, s, d = x.shape
    n_ctx = ctx.shape[1]

    rows = -(-(b + 1) // 8) * 8
    cc = jnp.zeros((rows, d), F32).at[:b].set(c).at[b].set(c_ctx)
    mod = _ada_call(cc, ada_w, ada_b)

    def mods(i):
        lat = [mod[i, :b, k * d:(k + 1) * d].reshape(b, 1, d) for k in range(6)]
        cx = [mod[i, b:b + 1, k * d:(k + 1) * d].reshape(1, 1, d) for k in range(6)]
        return lat, cx

    rope = _rope_tables(s)
    rope_id = (jnp.ones((n_ctx, LANES), F32), jnp.zeros((n_ctx, LANES), F32), jnp.zeros((n_ctx, LANES), F32))
    gmat = jnp.where((jnp.arange(LANES)[:, None] // HEAD_DIM) == (jnp.arange(LANES)[None, :] // HEAD_DIM),
                     1.0 / HEAD_DIM, 0.0).astype(BF16)
    tile2 = lambda g: jnp.concatenate([g, g]).reshape(1, LANES)

    (sh1, sc1, g1, sh2, sc2, g2), (csh1, csc1, cg1, csh2, csc2, cg2) = mods(0)
    bq0 = 3 * 512
    pair_heads = [hh for j in range(B_HEADS // 2) for hh in (j, j + B_HEADS // 2)]
    w_in_b, w_out_b = l0_w_in.astype(BF16), l0_w_out.astype(BF16)
    w_in0 = jnp.concatenate([w_in_b[:, :bq0]]
                            + [w_in_b[:, bq0 + hh * HEAD_DIM:bq0 + (hh + 1) * HEAD_DIM] for hh in pair_heads]
                            + [w_in_b[:, bq0 + 512:]], axis=1)
    w_out0 = jnp.concatenate([w_out_b[:512]]
                             + [w_out_b[512 + hh * HEAD_DIM:512 + (hh + 1) * HEAD_DIM] for hh in pair_heads], axis=0)
    gains0 = jnp.concatenate([tile2(l0_qnorm_g), tile2(l0_knorm_g)], axis=0)
    q_exp2 = Q_SCALE * math.log2(math.e)
    blocks0 = ([("rope", 0, q_exp2, 0, EV_AQ + k * LANES) for k in range(4)]
               + [("rope", 0, 1.0, 0, EV_AK + k * LANES) for k in range(4)]
               + [("plain", 0, 1.0, 0, EV_AV + k * 2 * LANES) for k in range(4)]
               + [("norm", 0, q_exp2, 0, EV_BQ + k * LANES) for k in range(4)]
               + [("norm", 1, 1.0, 0, EV_BK), ("plain", 0, 1.0, 0, EV_BV)])
    ones0 = [(0, EV_AV + (2 * k + 1) * LANES) for k in range(4)] + [(0, EV_BV + LANES)]
    qkv = _inproj_call(x, sc1, sh1, w_in0, blocks0, [(EV_WIDTH, BF16)], rope, gains0, gmat,
                       use_rope=True, name="inproj_l0", ones_cols=ones0)[0]
    qkv_c = _inproj_call(ctx, csc1, csh1, w_in0, blocks0, [(EV_WIDTH, BF16)], rope_id, gains0, gmat,
                         use_rope=False, name="inproj_l0c", ones_cols=ones0)[0]
    lam_init = 0.8 - 0.6 * math.exp(-0.3 * 0)
    lamv = jnp.stack([l0_lam_q1, l0_lam_k1, l0_lam_q2, l0_lam_k2], axis=0)
    sub_g = l0_subln_g.reshape(1, LANES)
    mix = _attn_even_call(qkv, [qkv, qkv_c], lamv, sub_g, lam_init, "attn_l0")
    mix_c = _attn_even_call(qkv_c, [qkv_c], lamv, sub_g, lam_init, "attn_l0c")
    x, ctx = _ffn_sublayer(
        [(mix, w_out0, x, g1, sc2, sh2, g2), (mix_c, w_out0, ctx, cg1, csc2, csh2, cg2)],
        None, ln_g[0], ln_b[0], moe_router[0], moe_w_gate, moe_w_up, moe_w_down, 0)

    (sh1, sc1, g1, sh2, sc2, g2), (csh1, csc1, _, _, _, _) = mods(1)
    cw = C_HEADS * HEAD_DIM
    w_in1 = l1_w_in.astype(BF16)
    blocks1 = ([("plain", 0, q_exp2, 0, k * LANES) for k in range(4)]
               + [("plain", 0, 1.0, 0, cw + k * LANES) for k in range(4)]
               + [("plain", 0, 1.0, 0, 2 * cw + 2 * k * LANES) for k in range(4)]
               + [("plain", 0, 1.0, 1, k * LANES) for k in range(4)])
    ones1 = [(0, 2 * cw + (2 * k + 1) * LANES) for k in range(4)]
    qkv1, du = _inproj_call(x, sc1, sh1, w_in1, blocks1, [(4 * cw, BF16), (D_GROUPS * D_GROUP_DIM, F32)],
                            rope, gains0, gmat, use_rope=False, name="inproj_l1", ones_cols=ones1)
    blocks1c = ([("plain", 0, 1.0, 0, k * LANES) for k in range(4)]
                + [("plain", 0, 1.0, 0, cw + 2 * k * LANES) for k in range(4)])
    ones1c = [(0, cw + (2 * k + 1) * LANES) for k in range(4)]
    ckv = _inproj_call(ctx, csc1, csh1, w_in1[:, cw:3 * cw], blocks1c, [(3 * cw, BF16)], rope_id, gains0, gmat,
                       use_rope=False, name="inproj_l1c", ones_cols=ones1c)[0]
    mix1 = _odd_mixer_call(qkv1, ckv, l1_rpb * math.log2(math.e), du, l1_pool_w.astype(BF16),
                           l1_pool_scale.reshape(1, -1))
    (x,) = _ffn_sublayer([(mix1, l1_w_out.astype(BF16), x, g1, sc2, sh2, g2)],
                         None, ln_g[1], ln_b[1], moe_router[1], moe_w_gate, moe_w_up, moe_w_down, 1)
    return x
```

```python
import collections
import functools
import math

import numpy as np

import jax
import jax.numpy as jnp
from jax import lax
from jax.experimental import pallas as pl
from jax.experimental.pallas import tpu as pltpu

F32 = jnp.float32
BF16 = jnp.bfloat16

DEPTH = 2
GRID_W = 64
HEAD_DIM = 64
A_HEADS = 4
B_HEADS = 8
B_KV_HEADS = 2
C_HEADS = 8
D_GROUPS = 4
D_GROUP_DIM = 128
POOL_WINDOWS = (2, 4, 8, 16)
NA_ROWS = 8
NA_COLS = 16
N_EXPERTS = 16
EC_FACTOR = 2
ROPE_THETA = 10000.0
LN_EPS = 1e-5
RMS_EPS = 1e-6
DEEPNORM_ALPHA = (2 * DEPTH) ** 0.25
Q_SCALE = HEAD_DIM ** -0.5

LANES = 128
VMEM_LIMIT = 56 * 1024 * 1024

NA_QROWS = 4
NEG_BIG = -1e30

_NN = (((1,), (0,)), ((), ()))
_NT = (((1,), (1,)), ((), ()))


def _dot(a, b, dims=_NN):
    return lax.dot_general(a, b, dims, preferred_element_type=F32)


def _split_bf16(a):
    hi = a.astype(BF16)
    lo = (a - hi.astype(F32)).astype(BF16)
    return hi, lo


def _dot3(a, b, dims=_NN):
    a_hi, a_lo = _split_bf16(a)
    b_hi, b_lo = _split_bf16(b)
    return _dot(a_hi, b_hi, dims) + (_dot(a_hi, b_lo, dims) + _dot(a_lo, b_hi, dims))


class _Mod(collections.namedtuple("_Mod", "arr layer k row")):
    def spec(self):
        d = self.arr.shape[-1]

        def index_map(*grid):
            return (self.layer, grid[0] if self.row is None else self.row, self.k, 0, 0)

        return pl.BlockSpec((None, None, 1, 1, d), index_map)


def _params(sem):
    return pltpu.CompilerParams(dimension_semantics=sem, vmem_limit_bytes=VMEM_LIMIT)


def _layer_norm(z, g, b):
    mu = jnp.mean(z, axis=-1, keepdims=True)
    zc = z - mu
    var = jnp.mean(zc * zc, axis=-1, keepdims=True)
    return zc * lax.rsqrt(var + LN_EPS) * g + b


def _lane_masks():
    lane = lax.broadcasted_iota(jnp.int32, (1, LANES), 1)
    lo = jnp.where(lane < HEAD_DIM, 1.0, 0.0).astype(F32)
    return lo, 1.0 - lo


def _ada_kernel(c_ref, w_ref, b_ref, o_ref):
    c = c_ref[...]
    s = c * (1.0 / (1.0 + jnp.exp(-c)))
    o_ref[0] = _dot3(s, w_ref[0]) + b_ref[0]


def _ada_call(cc, ada_w, ada_b):
    depth, d, n = ada_w.shape
    rows = cc.shape[0]
    tn = 1536 if n % 1536 == 0 else n
    return pl.pallas_call(
        _ada_kernel,
        out_shape=jax.ShapeDtypeStruct((depth, rows, n), F32),
        grid=(depth, n // tn),
        in_specs=[pl.BlockSpec((rows, d), lambda l, j: (0, 0)),
                  pl.BlockSpec((1, d, tn), lambda l, j: (l, 0, j)),
                  pl.BlockSpec((1, 1, tn), lambda l, j: (l, 0, j))],
        out_specs=pl.BlockSpec((1, rows, tn), lambda l, j: (l, 0, j)),
        compiler_params=_params(("arbitrary", "arbitrary")),
        name="ada_mod",
    )(cc, ada_w, ada_b.reshape(depth, 1, n))


def _inproj_body(h, w_ref, rope_refs, gn_ref, gmat_ref, o_refs, *, blocks, ones_cols, chunk, use_rope):
    n = len(blocks) * LANES
    for oi, oc in ones_cols:
        o_refs[oi][0, :, oc:oc + LANES] = jnp.ones((h.shape[0], LANES), o_refs[oi].dtype)
    starts = list(range(0, n, chunk))
    accs = {starts[0]: _dot(h, w_ref[:, starts[0]:min(starts[0] + chunk, n)])}
    for ci, c0 in enumerate(starts):
        cw = min(chunk, n - c0)
        if ci + 1 < len(starts):
            nxt = starts[ci + 1]
            accs[nxt] = _dot(h, w_ref[:, nxt:min(nxt + chunk, n)])
        acc = accs.pop(c0)
        for j in range(cw // LANES):
            kind, gain_row, factor, oi, oc = blocks[(c0 // LANES) + j]
            v = acc[:, j * LANES:(j + 1) * LANES]
            if kind == "norm":
                v2 = v * v
                hi, lo = _split_bf16(v2)
                ms = _dot(hi, gmat_ref[...]) + _dot(lo, gmat_ref[...])
                v = v * lax.rsqrt(ms + RMS_EPS) * gn_ref[gain_row:gain_row + 1, :]
            if kind in ("rope", "norm") and use_rope:
                cos_ref, sinp_ref, sinm_ref = rope_refs
                v = (v * cos_ref[...] + pltpu.roll(v, 16, 1) * sinp_ref[...]
                     + pltpu.roll(v, LANES - 16, 1) * sinm_ref[...])
            if factor != 1.0:
                v = v * factor
            o_refs[oi][0, :, oc:oc + LANES] = v.astype(o_refs[oi].dtype)


def _inproj_kernel(x_ref, sc_ref, sh_ref, w_ref, cos_ref, sinp_ref, sinm_ref, gn_ref, gmat_ref, *o_refs, **cfg):
    h = (x_ref[0] * (1.0 + sc_ref[0]) + sh_ref[0]).astype(BF16)
    _inproj_body(h, w_ref, (cos_ref, sinp_ref, sinm_ref), gn_ref, gmat_ref, o_refs, **cfg)


def _inproj_call(x, sc, sh, w, blocks, out_defs, rope_tabs, gains, gmat, *, use_rope, name, ones_cols=()):
    b, s, d = x.shape
    n = w.shape[1]
    tm = min(512, s)
    cos, sinp, sinm = rope_tabs
    tab_spec = pl.BlockSpec((tm, LANES), lambda bi, i: (i, 0))
    kern = functools.partial(_inproj_kernel, blocks=tuple(blocks), ones_cols=tuple(ones_cols), chunk=512,
                             use_rope=use_rope)
    return pl.pallas_call(
        kern,
        out_shape=[jax.ShapeDtypeStruct((b, s, nc), dt) for nc, dt in out_defs],
        grid=(b, s // tm),
        in_specs=[pl.BlockSpec((1, tm, d), lambda bi, i: (bi, i, 0)),
                  sc.spec(), sh.spec(),
                  pl.BlockSpec((d, n), lambda bi, i: (0, 0)),
                  tab_spec, tab_spec, tab_spec,
                  pl.BlockSpec(gains.shape, lambda bi, i: (0, 0)),
                  pl.BlockSpec(gmat.shape, lambda bi, i: (0, 0))],
        out_specs=[pl.BlockSpec((1, tm, nc), lambda bi, i: (bi, i, 0)) for nc, _ in out_defs],
        compiler_params=_params(("arbitrary", "arbitrary")),
        name=name,
    )(x, sc.arr, sh.arr, w, cos, sinp, sinm, gains, gmat)


def _rope_tables(s):
    n_freq = HEAD_DIM // 4
    t = jnp.arange(s, dtype=jnp.int32)
    inv = ROPE_THETA ** (-jnp.arange(n_freq, dtype=F32) / n_freq)
    ang_r = (t // GRID_W).astype(F32)[:, None] * inv
    ang_c = (t % GRID_W).astype(F32)[:, None] * inv
    ang = jnp.concatenate([ang_r, ang_r, ang_c, ang_c] * (LANES // HEAD_DIM), axis=-1)
    first = (jnp.arange(LANES) % 32) < 16
    cos, sin = jnp.cos(ang), jnp.sin(ang)
    return cos, jnp.where(first, 0.0, sin), jnp.where(first, -sin, 0.0)


EV_AQ, EV_AK, EV_AV = 0, 512, 1024
EV_BQ = EV_AV + A_HEADS * 2 * LANES
EV_BK = EV_BQ + (B_HEADS // 2) * LANES
EV_BV = EV_BK + LANES
EV_WIDTH = EV_BV + 2 * LANES


def _attn_even_kernel(q_ref, lam_ref, sg_ref, *refs, n_kv, tq, lam_init):
    kv_refs, o_ref = refs[:n_kv], refs[n_kv]
    lo, hi = _lane_masks()
    lo_b, hi_b = lo.astype(BF16), hi.astype(BF16)
    lv = lam_ref[...]
    lam = (jnp.exp(jnp.sum(lv[0:1] * lv[1:2], axis=-1, keepdims=True))
           - jnp.exp(jnp.sum(lv[2:3] * lv[3:4], axis=-1, keepdims=True)) + lam_init)

    def scores(q, kcol):
        qq = jnp.concatenate([q * lo_b, q * hi_b], axis=0)
        ss = [_dot(qq, kv[0, :, kcol:kcol + LANES], _NT) for kv in kv_refs]
        m = ss[0].max(axis=-1, keepdims=True)
        for s in ss[1:]:
            m = jnp.maximum(m, s.max(axis=-1, keepdims=True))
        return ss, m

    def weighted(ss, m, vcol):
        acc = None
        for s, kv in zip(ss, kv_refs):
            part = _dot(jnp.exp2(s - m).astype(BF16), kv[0, :, vcol:vcol + 2 * LANES])
            acc = part if acc is None else acc + part
        return acc[:, :LANES] * (1.0 / acc[:, LANES:])

    units = ([("a", h, EV_AQ + h * LANES, EV_AK + h * LANES, EV_AV + h * 2 * LANES) for h in range(A_HEADS)]
             + [("b", j, EV_BQ + j * LANES, EV_BK, EV_BV) for j in range(B_HEADS // 2)])

    def finish(unit, ss, m):
        kind, idx, _, _, vcol = unit
        on = weighted(ss, m, vcol)
        if kind == "a":
            o = on[:tq] - lam * on[tq:]
            ms = jnp.mean(o * o, axis=-1, keepdims=True)
            o = o * lax.rsqrt(ms + RMS_EPS) * sg_ref[...] * (1.0 - lam_init)
            o_ref[0, :, idx * LANES:(idx + 1) * LANES] = o.astype(o_ref.dtype)
        else:
            o = on[:tq] * lo + on[tq:] * hi
            o_ref[0, :, 512 + idx * LANES:512 + (idx + 1) * LANES] = o.astype(o_ref.dtype)

    pending = None
    for unit in units:
        cur = scores(q_ref[0, :, unit[2]:unit[2] + LANES], unit[3])
        if pending is not None:
            finish(*pending)
        pending = (unit,) + cur
    finish(*pending)


def _attn_even_call(q_arr, kv_arrs, lamv, subln_g, lam_init, name):
    b, sq, n = q_arr.shape
    tq = min(256, sq)
    kern = functools.partial(_attn_even_kernel, n_kv=len(kv_arrs), tq=tq, lam_init=lam_init)
    return pl.pallas_call(
        kern,
        out_shape=jax.ShapeDtypeStruct((b, sq, 1024), BF16),
        grid=(b, sq // tq),
        in_specs=[pl.BlockSpec((1, tq, n), lambda bi, i: (bi, i, 0)),
                  pl.BlockSpec(lamv.shape, lambda bi, i: (0, 0)),
                  pl.BlockSpec(subln_g.shape, lambda bi, i: (0, 0))]
                 + [pl.BlockSpec((1,) + a.shape[1:], lambda bi, i: (bi, 0, 0)) for a in kv_arrs],
        out_specs=pl.BlockSpec((1, tq, 1024), lambda bi, i: (bi, i, 0)),
        compiler_params=_params(("arbitrary", "arbitrary")),
        name=name,
    )(q_arr, lamv, subln_g, *kv_arrs)


def _odd_mixer_kernel(qkv_ref, ckv_ref, bias_ref, du_ref, band_ref, edge_ref, pw_ref, ps_ref, o_ref, *,
                      tq, slab_rows, n_rows, n_blk, seq):
    i = pl.program_id(1)
    lo, hi = _lane_masks()
    lo_b, hi_b = lo.astype(BF16), hi.astype(BF16)
    t0 = pl.multiple_of(i * tq, tq)
    base = jnp.clip(i * NA_QROWS - NA_ROWS // 2, 0, n_rows - slab_rows)
    k0 = pl.multiple_of(base * GRID_W, GRID_W)
    nk = slab_rows * GRID_W
    cw = C_HEADS * HEAD_DIM

    def scores(j):
        cs = slice(j * LANES, (j + 1) * LANES)
        q = qkv_ref[0, pl.ds(t0, tq), cs]
        qq = jnp.concatenate([q * lo_b, q * hi_b], axis=0)
        kl = qkv_ref[0, pl.ds(k0, nk), cw + j * LANES:cw + (j + 1) * LANES]
        bias = jnp.concatenate([bias_ref[2 * j], bias_ref[2 * j + 1]], axis=0)
        s_l = _dot(qq, kl, _NT) + bias
        s_c = _dot(qq, ckv_ref[0, :, cs], _NT)
        m = jnp.maximum(s_l.max(axis=-1, keepdims=True), s_c.max(axis=-1, keepdims=True))
        return j, s_l, s_c, m

    def finish(j, s_l, s_c, m):
        vl = qkv_ref[0, pl.ds(k0, nk), 2 * cw + 2 * j * LANES:2 * cw + 2 * (j + 1) * LANES]
        vc = ckv_ref[0, :, cw + 2 * j * LANES:cw + 2 * (j + 1) * LANES]
        acc = _dot(jnp.exp2(s_l - m).astype(BF16), vl) + _dot(jnp.exp2(s_c - m).astype(BF16), vc)
        on = acc[:, :LANES] * (1.0 / acc[:, LANES:])
        o = on[:tq] * lo + on[tq:] * hi
        o_ref[0, :, j * LANES:(j + 1) * LANES] = o.astype(o_ref.dtype)

    pending = None
    for j in range(C_HEADS // 2):
        cur = scores(j)
        if pending is not None:
            finish(*pending)
        pending = cur
    finish(*pending)

    tprev = pl.multiple_of(jnp.maximum(i - 1, 0) * tq, tq)
    tnext = pl.multiple_of(jnp.minimum(i + 1, n_blk - 1) * tq, tq)
    has_prev = jnp.where(i > 0, 1.0, 0.0).astype(F32)
    has_next = jnp.where(i < n_blk - 1, 1.0, 0.0).astype(F32)
    tpos = t0 + lax.broadcasted_iota(jnp.int32, (tq, 1), 0)
    for g in range(D_GROUPS):
        half = POOL_WINDOWS[g] // 2
        gs = slice(g * D_GROUP_DIM, (g + 1) * D_GROUP_DIM)
        cur = du_ref[0, pl.ds(t0, tq), gs]
        prv = du_ref[0, pl.ds(tprev + (tq - POOL_EDGE), POOL_EDGE), gs]
        nxt = du_ref[0, pl.ds(tnext, POOL_EDGE), gs]

        def band_sum(band, u):
            u_hi, u_lo = _split_bf16(u)
            return _dot(band, u_hi) + _dot(band, u_lo)

        wsum = band_sum(band_ref[g], cur)
        wsum = jnp.concatenate([wsum[:POOL_EDGE] + has_prev * band_sum(edge_ref[g, 0], prv),
                                wsum[POOL_EDGE:tq - POOL_EDGE],
                                wsum[tq - POOL_EDGE:] + has_next * band_sum(edge_ref[g, 1], nxt)], axis=0)
        cnt = (jnp.minimum(tpos + half, seq) - jnp.maximum(tpos - half, 0)).astype(F32)
        pooled = wsum * (1.0 / cnt) - cur
        od = _dot(pooled.astype(BF16), pw_ref[g]) * ps_ref[:, gs]
        o_ref[0, :, cw + g * D_GROUP_DIM:cw + (g + 1) * D_GROUP_DIM] = od.astype(o_ref.dtype)


def _na_geometry(n_rows):
    kh = min(NA_ROWS, n_rows)
    slab = min(n_rows, NA_QROWS + kh)
    n_blk = n_rows // NA_QROWS
    bases = [min(max(i * NA_QROWS - NA_ROWS // 2, 0), n_rows - slab) for i in range(n_blk)]
    sigs, type_of = [], []
    for i in range(n_blk):
        sig = tuple((min(max(i * NA_QROWS + r - kh // 2, 0), n_rows - kh) - bases[i],
                     i * NA_QROWS + r - bases[i]) for r in range(NA_QROWS))
        if sig not in sigs:
            sigs.append(sig)
        type_of.append(sigs.index(sig))
    return kh, slab, n_blk, sigs, type_of


def _na_bias_table(rpb, n_rows):
    kh, slab, _, sigs, _ = _na_geometry(n_rows)
    kw = NA_COLS
    n_dc = 2 * NA_COLS - 1
    w = np.arange(GRID_W)
    cstart = np.clip(w - kw // 2, 0, GRID_W - kw)
    col_ok = (w[None, :] >= cstart[:, None]) & (w[None, :] < cstart[:, None] + kw)
    dc = np.clip(w[None, :] - w[:, None] + (NA_COLS - 1), 0, n_dc - 1)
    heads = rpb.shape[0]
    onehot = ((dc[None] == np.arange(n_dc)[:, None, None]) & col_ok[None]).astype(np.float32)
    toep = jnp.einsum("hrd,dwj->hwrj", rpb, jnp.asarray(onehot), precision=lax.Precision.HIGHEST)
    toep = jnp.where(jnp.asarray(col_ok)[:, None, :], toep, NEG_BIG)
    pad = slab + NA_QROWS
    toep = jnp.pad(toep, ((0, 0), (0, 0), (pad, pad), (0, 0)), constant_values=NEG_BIG)
    toep = toep.reshape(heads, GRID_W, -1)
    blocks = []
    for sig in sigs:
        for rs_rel, qr_rel in sig:
            d0 = pad - qr_rel + (NA_ROWS - 1)
            in_win = np.repeat(np.array([rs_rel <= m < rs_rel + kh for m in range(slab)]), GRID_W)
            blk = lax.slice_in_dim(toep, d0 * GRID_W, (d0 + slab) * GRID_W, axis=2)
            blocks.append(jnp.where(jnp.asarray(in_win), blk, NEG_BIG))
    big = jnp.stack(blocks, axis=1)
    return big.reshape(heads, len(sigs), NA_QROWS * GRID_W, slab * GRID_W)


POOL_EDGE = 16


def _pool_bands(tq):
    t = jnp.arange(tq, dtype=jnp.int32)[:, None]
    sidx = jnp.arange(tq, dtype=jnp.int32)[None, :]
    main, edge = [], []
    for wdw in POOL_WINDOWS:
        half = wdw // 2
        full = [((sidx + (m - 1) * tq >= t - half) & (sidx + (m - 1) * tq < t + half)).astype(BF16)
                for m in range(3)]
        main.append(full[1])
        edge.append(jnp.stack([full[0][:POOL_EDGE, tq - POOL_EDGE:], full[2][tq - POOL_EDGE:, :POOL_EDGE]]))
    return jnp.stack(main), jnp.stack(edge)


def _odd_mixer_call(qkv, ckv, rpb, du, pool_w, pool_scale):
    b, s, n = qkv.shape
    n_rows = s // GRID_W
    _, slab, n_blk, _, type_of = _na_geometry(n_rows)
    tq = NA_QROWS * GRID_W
    bias = _na_bias_table(rpb, n_rows)
    bands, edges = _pool_bands(tq)

    def bias_map(bi, i):
        t = jnp.int32(type_of[-1])
        for blk in range(n_blk - 2, -1, -1):
            t = jnp.where(i == blk, jnp.int32(type_of[blk]), t)
        return (0, t, 0, 0)

    kern = functools.partial(_odd_mixer_kernel, tq=tq, slab_rows=slab, n_rows=n_rows, n_blk=n_blk, seq=s)
    return pl.pallas_call(
        kern,
        out_shape=jax.ShapeDtypeStruct((b, s, 1024), BF16),
        grid=(b, n_blk),
        in_specs=[pl.BlockSpec((1, s, n), lambda bi, i: (bi, 0, 0)),
                  pl.BlockSpec((1,) + ckv.shape[1:], lambda bi, i: (bi, 0, 0)),
                  pl.BlockSpec((bias.shape[0], None) + bias.shape[2:], bias_map),
                  pl.BlockSpec((1, s, du.shape[2]), lambda bi, i: (bi, 0, 0)),
                  pl.BlockSpec(bands.shape, lambda bi, i: (0, 0, 0)),
                  pl.BlockSpec(edges.shape, lambda bi, i: (0, 0, 0, 0)),
                  pl.BlockSpec(pool_w.shape, lambda bi, i: (0, 0, 0)),
                  pl.BlockSpec(pool_scale.shape, lambda bi, i: (0, 0))],
        out_specs=pl.BlockSpec((1, tq, 1024), lambda bi, i: (bi, i, 0)),
        compiler_params=_params(("arbitrary", "arbitrary")),
        name="odd_mixer",
    )(qkv, ckv, bias, du, bands, edges, pool_w, pool_scale)


OUTPROJ_SUB = 256


def _outproj_kernel(mix_ref, w_ref, x_ref, g1_ref, lng_ref, lnb_ref, sc_ref, sh_ref, r2_ref, rh_ref,
                    x1_ref, hm_ref, aff_ref, *, n_exp):
    tm = x_ref.shape[1]
    subs = [slice(r0, r0 + OUTPROJ_SUB) for r0 in range(0, tm, OUTPROJ_SUB)]
    ys = [_dot(mix_ref[0, rows, :], w_ref[...]) for rows in subs]
    for rows, y in zip(subs, ys):
        x1 = _layer_norm(DEEPNORM_ALPHA * x_ref[0, rows, :] + g1_ref[0] * y, lng_ref[...], lnb_ref[...])
        x1_ref[0, rows, :] = x1
        hm = x1 * (1.0 + sc_ref[0]) + sh_ref[0]
        hm_hi = hm.astype(BF16)
        hm_ref[0, rows, :] = hm_hi
        hm_lo = (hm - hm_hi.astype(F32)).astype(BF16)
        part = _dot(hm_hi, r2_ref[...])
        logits = part[:, :n_exp] + part[:, n_exp:] + _dot(hm_lo, rh_ref[...])
        ex = jnp.exp(logits - logits.max(axis=-1, keepdims=True))
        aff_ref[0, rows, :] = ex * (1.0 / ex.sum(axis=-1, keepdims=True))


def _outproj_call(mix, w, x, g1, lng, lnb, sc2, sh2, router, name):
    b, s, d = x.shape
    dm = mix.shape[2]
    e = router.shape[1]
    tm = min(2 * OUTPROJ_SUB, s)
    r_hi = router.astype(BF16)
    r_lo = (router - r_hi.astype(F32)).astype(BF16)
    r2 = jnp.concatenate([r_hi, r_lo], axis=1)
    row = pl.BlockSpec((1, d), lambda bi, i: (0, 0))
    tile = lambda n: pl.BlockSpec((1, tm, n), lambda bi, i: (bi, i, 0))
    return pl.pallas_call(
        functools.partial(_outproj_kernel, n_exp=e),
        out_shape=[jax.ShapeDtypeStruct((b, s, d), F32),
                   jax.ShapeDtypeStruct((b, s, d), BF16),
                   jax.ShapeDtypeStruct((b, s, e), F32)],
        grid=(b, s // tm),
        in_specs=[tile(dm),
                  pl.BlockSpec((dm, d), lambda bi, i: (0, 0)),
                  tile(d),
                  g1.spec(), row, row, sc2.spec(), sh2.spec(),
                  pl.BlockSpec((d, 2 * e), lambda bi, i: (0, 0)),
                  pl.BlockSpec((d, e), lambda bi, i: (0, 0))],
        out_specs=[tile(d), tile(d), tile(e)],
        compiler_params=_params(("arbitrary", "arbitrary")),
        name=name,
    )(mix, w, x, g1.arr, lng, lnb, sc2.arr, sh2.arr, r2, r_hi)


def _lane_cumsum(m):
    rows, s = m.shape
    r_i = lax.broadcasted_iota(jnp.int32, (LANES, LANES), 0)
    c_i = lax.broadcasted_iota(jnp.int32, (LANES, LANES), 1)
    tri = jnp.where(r_i <= c_i, 1.0, 0.0).astype(BF16)
    carry = jnp.zeros((rows, 1), F32)
    out = []
    for c in range(s // LANES):
        blk = m[:, c * LANES:(c + 1) * LANES]
        out.append(_dot(blk.astype(BF16), tri) + carry)
        carry = carry + blk.sum(axis=-1, keepdims=True)
    return jnp.concatenate(out, axis=-1)


ROUTE_TILE = 256


def _route_kernel(aff_ref, pos_ref, cnt_ref, *, cap):
    a = aff_ref[...]
    thr = jnp.zeros((a.shape[0], 1), jnp.int32)
    for bit in range(30, -1, -1):
        cand = thr | jnp.int32(1 << bit)
        cnt = jnp.where(a >= pltpu.bitcast(cand, F32), 1.0, 0.0).sum(axis=-1, keepdims=True)
        thr = jnp.where(cnt >= cap, cand, thr)
    gt = jnp.where(a >= pltpu.bitcast(thr + 1, F32), 1.0, 0.0)
    eq = jnp.where(a >= pltpu.bitcast(thr, F32), 1.0, 0.0) - gt
    need = cap - gt.sum(axis=-1, keepdims=True)
    sel = gt + eq * jnp.where(_lane_cumsum(eq) <= need, 1.0, 0.0)
    pos_ref[...] = jnp.where(sel > 0.5, _lane_cumsum(sel) - 1.0, -1.0)
    tok = lax.broadcasted_iota(jnp.int32, (a.shape[1], LANES), 0)
    tile = lax.broadcasted_iota(jnp.int32, (a.shape[1], LANES), 1)
    before = jnp.where(tok < tile * ROUTE_TILE, 1.0, 0.0).astype(BF16)
    cnt_ref[...] = _dot(sel.astype(BF16), before)


def _route_call(aff, cap, name):
    b, e, s = aff.shape
    n_tiles = s // min(ROUTE_TILE, s)
    full = pl.BlockSpec((b * e, s), lambda i: (0, 0))
    pos, cnt = pl.pallas_call(
        functools.partial(_route_kernel, cap=cap),
        out_shape=[jax.ShapeDtypeStruct((b * e, s), F32), jax.ShapeDtypeStruct((b * e, LANES), F32)],
        grid=(1,),
        in_specs=[full],
        out_specs=[full, pl.BlockSpec((b * e, LANES), lambda i: (0, 0))],
        compiler_params=_params(("arbitrary",)),
        name=name,
    )(aff.reshape(b * e, s))
    return pos.reshape(b, e, s), cnt[:, :n_tiles + 1].astype(jnp.int32).reshape(-1)


def _gather_kernel(cnt_ref, pos_ref, aff_ref, hm_ref, xg_ref, gs_ref, *, cap, n_exp, win, n_tiles):
    bi, i = pl.program_id(0), pl.program_id(1)

    @pl.when(i == 0)
    def _():
        xg_ref[...] = jnp.zeros(xg_ref.shape, xg_ref.dtype)
        gs_ref[...] = jnp.zeros(gs_ref.shape, gs_ref.dtype)

    starts, short = [], None
    for ei in range(n_exp):
        base = (bi * n_exp + ei) * (n_tiles + 1) + i
        st = jnp.minimum((cnt_ref[base] // SLOT_ALIGN) * SLOT_ALIGN, cap - win)
        starts.append(pl.multiple_of(st, SLOT_ALIGN))
        miss = cnt_ref[base + 1] > st + win
        short = miss if short is None else jnp.logical_or(short, miss)

    def scatter_rows(n_rows, first, stacked):
        row = lax.broadcasted_iota(jnp.int32, (n_rows, 1), 0).astype(F32)
        blocks = []
        for ei in range(n_exp):
            rel = pos_ref[0, ei:ei + 1, :] if first is None else pos_ref[0, ei:ei + 1, :] - first[ei].astype(F32)
            hit = rel == row
            blocks.append(jnp.where(hit, 1.0, 0.0).astype(BF16))
            gate = jnp.where(hit, aff_ref[0, ei:ei + 1, :], 0.0).sum(axis=-1, keepdims=True)
            rows = slice(None) if first is None else pl.ds(first[ei], n_rows)
            gs_ref[0, ei, rows, :] += gate
        if stacked:
            part = _dot(jnp.concatenate(blocks, axis=0), hm_ref[0])
        for ei in range(n_exp):
            rows = slice(None) if first is None else pl.ds(first[ei], n_rows)
            sub = part[ei * n_rows:(ei + 1) * n_rows] if stacked else _dot(blocks[ei], hm_ref[0])
            xg_ref[0, ei, rows, :] += sub.astype(xg_ref.dtype)

    @pl.when(jnp.logical_not(short))
    def _():
        scatter_rows(win, starts, True)

    @pl.when(short)
    def _():
        scatter_rows(cap, None, False)


def _gather_call(cnt, pos, aff, hm, cap, name):
    b, e, s = pos.shape
    d = hm.shape[2]
    tm = min(ROUTE_TILE, s)
    n_tiles = s // tm
    win = min(64, cap)
    tok = pl.BlockSpec((1, e, tm), lambda bi, i, c: (bi, 0, i))
    return pl.pallas_call(
        functools.partial(_gather_kernel, cap=cap, n_exp=e, win=win, n_tiles=n_tiles),
        out_shape=[jax.ShapeDtypeStruct((b, e, cap, d), BF16),
                   jax.ShapeDtypeStruct((b, e, cap, 1), F32)],
        grid_spec=pltpu.PrefetchScalarGridSpec(
            num_scalar_prefetch=1,
            grid=(b, n_tiles),
            in_specs=[tok, tok, pl.BlockSpec((1, tm, d), lambda bi, i, c: (bi, i, 0))],
            out_specs=[pl.BlockSpec((1, e, cap, d), lambda bi, i, c: (bi, 0, 0, 0)),
                       pl.BlockSpec((1, e, cap, 1), lambda bi, i, c: (bi, 0, 0, 0))]),
        compiler_params=_params(("arbitrary", "arbitrary")),
        name=name,
    )(cnt, pos, aff, hm)


def _ffn_kernel(*refs, n_grp, row_chunks):
    xg_refs = refs[:n_grp]
    gs_refs = refs[n_grp:2 * n_grp]
    wg_ref, wu_ref, wd_ref = refs[2 * n_grp:2 * n_grp + 3]
    y_refs = refs[2 * n_grp + 3:3 * n_grp + 3]
    acc_refs = refs[3 * n_grp + 3:]
    f = pl.program_id(1)
    last = pl.num_programs(1) - 1
    wg = wg_ref[0].astype(BF16)
    wu = wu_ref[0].astype(BF16)
    wd = wd_ref[0].astype(BF16)

    @pl.when(f == 0)
    def _():
        for acc_ref in acc_refs:
            acc_ref[...] = jnp.zeros(acc_ref.shape, F32)

    for xg_ref, acc_ref, nb in zip(xg_refs, acc_refs, row_chunks):
        bt, _, cap, d = xg_ref.shape
        for b0 in range(0, bt, nb):
            rows = nb * cap
            r0 = b0 * cap
            x = xg_ref[b0:b0 + nb, 0].reshape(rows, d)
            hg = _dot(x, wg)
            hu = _dot(x, wu)
            hid = (hg * (1.0 / (1.0 + jnp.exp(-hg))) * hu).astype(BF16)
            acc_ref[r0:r0 + rows, :] += _dot(hid, wd)

    @pl.when(f == last)
    def _():
        for xg_ref, gs_ref, y_ref, acc_ref in zip(xg_refs, gs_refs, y_refs, acc_refs):
            bt, _, cap, d = xg_ref.shape
            gate = gs_ref[:, 0].reshape(bt * cap, 1)
            y_ref[:, 0] = (acc_ref[...] * gate).reshape(bt, cap, d).astype(y_ref.dtype)


def _ffn_call(xgs, gss, w_gate, w_up, w_down, layer, name):
    _, e, d, ff = w_gate.shape
    tf = 512 if ff % 512 == 0 else ff
    n_grp = len(xgs)
    row_chunks = []
    for xg in xgs:
        bt, _, cap, _ = xg.shape
        nb = max(1, min(bt, 512 // cap))
        while bt % nb:
            nb -= 1
        row_chunks.append(nb)
    tok = lambda a: pl.BlockSpec((a.shape[0], 1) + a.shape[2:], lambda ei, fi: (0, ei, 0, 0))
    kern = functools.partial(_ffn_kernel, n_grp=n_grp, row_chunks=tuple(row_chunks))
    return pl.pallas_call(
        kern,
        out_shape=[jax.ShapeDtypeStruct(xg.shape, BF16) for xg in xgs],
        grid=(e, ff // tf),
        in_specs=[tok(a) for a in xgs] + [tok(a) for a in gss]
                 + [pl.BlockSpec((None, 1, d, tf), lambda ei, fi: (layer, ei, 0, fi)),
                    pl.BlockSpec((None, 1, d, tf), lambda ei, fi: (layer, ei, 0, fi)),
                    pl.BlockSpec((None, 1, tf, d), lambda ei, fi: (layer, ei, fi, 0))],
        out_specs=[tok(a) for a in xgs],
        scratch_shapes=[pltpu.VMEM((xg.shape[0] * xg.shape[2], d), F32) for xg in xgs],
        compiler_params=_params(("arbitrary", "arbitrary")),
        name=name,
    )(*xgs, *gss, w_gate, w_up, w_down)


MXU_DEPTH = 256
SLOT_ALIGN = 16


def _combine_kernel(cnt_ref, pos_ref, y_ref, x_ref, g_ref, lng_ref, lnb_ref, *rest, cap, n_exp, win, n_tiles,
                    proj_cfg=None):
    if proj_cfg is None:
        (o_ref,), proj_refs = rest, ()
    else:
        proj_refs, o_ref = rest[:3] + rest[4:], rest[3]
    bi, i = pl.program_id(0), pl.program_id(1)
    pos = pos_ref[0]
    tm = pos.shape[0]
    grp = MXU_DEPTH // win
    starts, short = [], None
    for ei in range(n_exp):
        base = (bi * n_exp + ei) * (n_tiles + 1) + i
        st = jnp.minimum((cnt_ref[base] // SLOT_ALIGN) * SLOT_ALIGN, cap - win)
        starts.append(st)
        miss = cnt_ref[base + 1] > st + win
        short = miss if short is None else jnp.logical_or(short, miss)

    def windowed():
        lane = lax.broadcasted_iota(jnp.int32, (1, grp * win), 1)
        lane_f = lane.astype(F32)
        acc = None
        for k in range(n_exp // grp):
            tgt, rows = None, []
            for u in range(grp - 1, -1, -1):
                ei = k * grp + u
                st = starts[ei]
                rel = pos[:, ei:ei + 1] - st.astype(F32)
                rel = jnp.where(rel >= 0.0, jnp.where(rel < win, rel + float(u * win), -1.0), -1.0)
                tgt = rel if tgt is None else jnp.where(lane < (u + 1) * win, rel, tgt)
                rows.insert(0, y_ref[0, pl.ds(pl.multiple_of(ei * cap + st, SLOT_ALIGN), win), :])
            onehot = jnp.where(tgt == lane_f, 1.0, 0.0).astype(BF16)
            part = _dot(onehot, jnp.concatenate(rows, axis=0))
            acc = part if acc is None else acc + part
        return acc

    def dense():
        slot = lax.broadcasted_iota(jnp.int32, (1, cap), 1).astype(F32)
        acc = None
        for ei in range(n_exp):
            onehot = jnp.where(pos[:, ei:ei + 1] == slot, 1.0, 0.0).astype(BF16)
            part = _dot(onehot, y_ref[0, ei * cap:(ei + 1) * cap, :])
            acc = part if acc is None else acc + part
        return acc

    acc = lax.cond(short, dense, windowed)
    x_new = _layer_norm(DEEPNORM_ALPHA * x_ref[0] + g_ref[0] * acc, lng_ref[...], lnb_ref[...])
    o_ref[0] = x_new
    if proj_cfg is not None:
        sc_ref, sh_ref, w_ref = proj_refs[:3]
        h = (x_new * (1.0 + sc_ref[0]) + sh_ref[0]).astype(BF16)
        _inproj_body(h, w_ref, None, None, None, proj_refs[3:], **proj_cfg)


def _combine_call(cnt, pos_t, y, x, g2, lng, lnb, cap, name, proj=None):
    b, s, d = x.shape
    e = pos_t.shape[2]
    tm = min(ROUTE_TILE, s)
    n_tiles = s // tm
    win = min(64, cap)
    row = pl.BlockSpec((1, d), lambda bi, i, c: (0, 0))
    tile = lambda n: pl.BlockSpec((1, tm, n), lambda bi, i, c: (bi, i, 0))
    in_specs = [tile(e), pl.BlockSpec((1, e * cap, d), lambda bi, i, c: (bi, 0, 0)), tile(d), g2.spec(), row, row]
    operands = [cnt, pos_t, y.reshape(b, e * cap, d), x, g2.arr, lng, lnb]
    out_shape = [jax.ShapeDtypeStruct((b, s, d), F32)]
    out_specs = [tile(d)]
    proj_cfg = None
    if proj is not None:
        sc, sh, w, blocks, out_defs, ones_cols = proj
        in_specs += [sc.spec(), sh.spec(), pl.BlockSpec(w.shape, lambda bi, i, c: (0, 0))]
        operands += [sc.arr, sh.arr, w]
        out_shape += [jax.ShapeDtypeStruct((b, s, nc), dt) for nc, dt in out_defs]
        out_specs += [tile(nc) for nc, _ in out_defs]
        proj_cfg = dict(blocks=tuple(blocks), ones_cols=tuple(ones_cols), chunk=512, use_rope=False)
    res = pl.pallas_call(
        functools.partial(_combine_kernel, cap=cap, n_exp=e, win=win, n_tiles=n_tiles, proj_cfg=proj_cfg),
        out_shape=out_shape,
        grid_spec=pltpu.PrefetchScalarGridSpec(
            num_scalar_prefetch=1, grid=(b, n_tiles), in_specs=in_specs, out_specs=out_specs),
        compiler_params=_params(("arbitrary", "arbitrary")),
        name=name,
    )(*operands)
    return res[0] if proj is None else res


def _ffn_sublayer(streams, projs, lng, lnb, router, w_gate, w_up, w_down, layer):
    staged = []
    for si, (mix, w_out, x, g1, sc2, sh2, g2) in enumerate(streams):
        tag = f"l{layer}s{si}"
        n_tok = x.shape[1]
        cap = EC_FACTOR * n_tok // N_EXPERTS
        x1, hm, aff_t = _outproj_call(mix, w_out, x, g1, lng[0:1], lnb[0:1], sc2, sh2, router, "outproj_" + tag)
        aff = jnp.swapaxes(aff_t, 1, 2)
        pos, cnt = _route_call(aff, cap, "route_" + tag)
        xg, gs = _gather_call(cnt, pos, aff, hm, cap, "gather_" + tag)
        staged.append((x1, pos, cnt, xg, gs, g2, cap, tag))
    ys = _ffn_call([st[3] for st in staged], [st[4] for st in staged], w_gate, w_up, w_down, layer,
                   f"ffn_l{layer}")
    outs = []
    for (x1, pos, cnt, _, _, g2, cap, tag), y, proj in zip(staged, ys, projs):
        outs.append(_combine_call(cnt, jnp.swapaxes(pos, 1, 2), y, x1, g2, lng[1:2], lnb[1:2], cap,
                                  "combine_" + tag, proj))
    return outs


def kernel(x, c, ctx, c_ctx, ada_w, ada_b, ln_g, ln_b, l0_w_in, l0_w_out, l0_lam_q1, l0_lam_k1, l0_lam_q2,
           l0_lam_k2, l0_subln_g, l0_qnorm_g, l0_knorm_g, l1_w_in, l1_w_out, l1_rpb, l1_pool_w, l1_pool_scale,
           moe_router, moe_w_gate, moe_w_up, moe_w_down):
    b, s, d = x.shape
    n_ctx = ctx.shape[1]

    rows = -(-(b + 1) // 8) * 8
    cc = jnp.zeros((rows, d), F32).at[:b].set(c).at[b].set(c_ctx)
    mod = _ada_call(cc, ada_w, ada_b)

    mod5 = mod.reshape(DEPTH, rows, 6, 1, d)

    def mods(i):
        return ([_Mod(mod5, i, k, None) for k in range(6)], [_Mod(mod5, i, k, b) for k in range(6)])

    rope = _rope_tables(s)
    rope_id = (jnp.ones((n_ctx, LANES), F32), jnp.zeros((n_ctx, LANES), F32), jnp.zeros((n_ctx, LANES), F32))
    gmat = jnp.where((jnp.arange(LANES)[:, None] // HEAD_DIM) == (jnp.arange(LANES)[None, :] // HEAD_DIM),
                     1.0 / HEAD_DIM, 0.0).astype(BF16)
    tile2 = lambda g: jnp.concatenate([g, g]).reshape(1, LANES)

    (sh1, sc1, g1, sh2, sc2, g2), (csh1, csc1, cg1, csh2, csc2, cg2) = mods(0)
    bq0 = 3 * 512
    pair_heads = [hh for j in range(B_HEADS // 2) for hh in (j, j + B_HEADS // 2)]
    w_in_b, w_out_b = l0_w_in.astype(BF16), l0_w_out.astype(BF16)
    w_in0 = jnp.concatenate([w_in_b[:, :bq0]]
                            + [w_in_b[:, bq0 + hh * HEAD_DIM:bq0 + (hh + 1) * HEAD_DIM] for hh in pair_heads]
                            + [w_in_b[:, bq0 + 512:]], axis=1)
    w_out0 = jnp.concatenate([w_out_b[:512]]
                             + [w_out_b[512 + hh * HEAD_DIM:512 + (hh + 1) * HEAD_DIM] for hh in pair_heads], axis=0)
    gains0 = jnp.concatenate([tile2(l0_qnorm_g), tile2(l0_knorm_g)], axis=0)
    q_exp2 = Q_SCALE * math.log2(math.e)
    blocks0 = ([("rope", 0, q_exp2, 0, EV_AQ + k * LANES) for k in range(4)]
               + [("rope", 0, 1.0, 0, EV_AK + k * LANES) for k in range(4)]
               + [("plain", 0, 1.0, 0, EV_AV + k * 2 * LANES) for k in range(4)]
               + [("norm", 0, q_exp2, 0, EV_BQ + k * LANES) for k in range(4)]
               + [("norm", 1, 1.0, 0, EV_BK), ("plain", 0, 1.0, 0, EV_BV)])
    ones0 = [(0, EV_AV + (2 * k + 1) * LANES) for k in range(4)] + [(0, EV_BV + LANES)]
    qkv = _inproj_call(x, sc1, sh1, w_in0, blocks0, [(EV_WIDTH, BF16)], rope, gains0, gmat,
                       use_rope=True, name="inproj_l0", ones_cols=ones0)[0]
    qkv_c = _inproj_call(ctx, csc1, csh1, w_in0, blocks0, [(EV_WIDTH, BF16)], rope_id, gains0, gmat,
                         use_rope=False, name="inproj_l0c", ones_cols=ones0)[0]
    lam_init = 0.8 - 0.6 * math.exp(-0.3 * 0)
    lamv = jnp.stack([l0_lam_q1, l0_lam_k1, l0_lam_q2, l0_lam_k2], axis=0)
    sub_g = l0_subln_g.reshape(1, LANES)
    mix = _attn_even_call(qkv, [qkv, qkv_c], lamv, sub_g, lam_init, "attn_l0")
    mix_c = _attn_even_call(qkv_c, [qkv_c], lamv, sub_g, lam_init, "attn_l0c")
    (sh1n, sc1n, g1n, sh2n, sc2n, g2n), (csh1n, csc1n, _, _, _, _) = mods(1)
    cw = C_HEADS * HEAD_DIM
    w_in1 = l1_w_in.astype(BF16)
    blocks1 = ([("plain", 0, q_exp2, 0, k * LANES) for k in range(4)]
               + [("plain", 0, 1.0, 0, cw + k * LANES) for k in range(4)]
               + [("plain", 0, 1.0, 0, 2 * cw + 2 * k * LANES) for k in range(4)]
               + [("plain", 0, 1.0, 1, k * LANES) for k in range(4)])
    ones1 = [(0, 2 * cw + (2 * k + 1) * LANES) for k in range(4)]
    blocks1c = ([("plain", 0, 1.0, 0, k * LANES) for k in range(4)]
                + [("plain", 0, 1.0, 0, cw + 2 * k * LANES) for k in range(4)])
    ones1c = [(0, cw + (2 * k + 1) * LANES) for k in range(4)]
    proj_lat = (sc1n, sh1n, w_in1, blocks1, [(4 * cw, BF16), (D_GROUPS * D_GROUP_DIM, F32)], ones1)
    proj_ctx = (csc1n, csh1n, w_in1[:, cw:3 * cw], blocks1c, [(3 * cw, BF16)], ones1c)
    (x, qkv1, du), (ctx, ckv) = _ffn_sublayer(
        [(mix, w_out0, x, g1, sc2, sh2, g2), (mix_c, w_out0, ctx, cg1, csc2, csh2, cg2)],
        [proj_lat, proj_ctx], ln_g[0], ln_b[0], moe_router[0], moe_w_gate, moe_w_up, moe_w_down, 0)

    mix1 = _odd_mixer_call(qkv1, ckv, l1_rpb * math.log2(math.e), du, l1_pool_w.astype(BF16),
                           l1_pool_scale.reshape(1, -1))
    (x,) = _ffn_sublayer([(mix1, l1_w_out.astype(BF16), x, g1n, sc2n, sh2n, g2n)],
                         [None], ln_g[1], ln_b[1], moe_router[1], moe_w_gate, moe_w_up, moe_w_down, 1)
    return x
```

```python
import collections
import functools
import math

import numpy as np

import jax
import jax.numpy as jnp
from jax import lax
from jax.experimental import pallas as pl
from jax.experimental.pallas import tpu as pltpu

F32 = jnp.float32
BF16 = jnp.bfloat16

DEPTH = 2
GRID_W = 64
HEAD_DIM = 64
A_HEADS = 4
B_HEADS = 8
B_KV_HEADS = 2
C_HEADS = 8
D_GROUPS = 4
D_GROUP_DIM = 128
POOL_WINDOWS = (2, 4, 8, 16)
NA_ROWS = 8
NA_COLS = 16
N_EXPERTS = 16
EC_FACTOR = 2
ROPE_THETA = 10000.0
LN_EPS = 1e-5
RMS_EPS = 1e-6
DEEPNORM_ALPHA = (2 * DEPTH) ** 0.25
Q_SCALE = HEAD_DIM ** -0.5

LANES = 128
VMEM_LIMIT = 56 * 1024 * 1024

NA_QROWS = 4
NEG_BIG = -1e30

_NN = (((1,), (0,)), ((), ()))
_NT = (((1,), (1,)), ((), ()))


def _dot(a, b, dims=_NN):
    return lax.dot_general(a, b, dims, preferred_element_type=F32)


def _split_bf16(a):
    hi = a.astype(BF16)
    lo = (a - hi.astype(F32)).astype(BF16)
    return hi, lo


def _dot3(a, b, dims=_NN):
    a_hi, a_lo = _split_bf16(a)
    b_hi, b_lo = _split_bf16(b)
    return _dot(a_hi, b_hi, dims) + (_dot(a_hi, b_lo, dims) + _dot(a_lo, b_hi, dims))


class _Mod(collections.namedtuple("_Mod", "arr layer k row")):
    def spec(self):
        d = self.arr.shape[-1]

        def index_map(*grid):
            return (self.layer, grid[0] if self.row is None else self.row, self.k, 0, 0)

        return pl.BlockSpec((None, None, 1, 1, d), index_map)


def _params(sem):
    return pltpu.CompilerParams(dimension_semantics=sem, vmem_limit_bytes=VMEM_LIMIT)


def _layer_norm(z, g, b):
    mu = jnp.mean(z, axis=-1, keepdims=True)
    zc = z - mu
    var = jnp.mean(zc * zc, axis=-1, keepdims=True)
    return zc * lax.rsqrt(var + LN_EPS) * g + b


def _lane_masks():
    lane = lax.broadcasted_iota(jnp.int32, (1, LANES), 1)
    lo = jnp.where(lane < HEAD_DIM, 1.0, 0.0).astype(F32)
    return lo, 1.0 - lo


def _ada_kernel(c_ref, w_ref, b_ref, o_ref):
    c = c_ref[...]
    s = c * (1.0 / (1.0 + jnp.exp(-c)))
    o_ref[0] = _dot3(s, w_ref[0]) + b_ref[0]


def _ada_call(cc, ada_w, ada_b):
    depth, d, n = ada_w.shape
    rows = cc.shape[0]
    tn = 1536 if n % 1536 == 0 else n
    return pl.pallas_call(
        _ada_kernel,
        out_shape=jax.ShapeDtypeStruct((depth, rows, n), F32),
        grid=(depth, n // tn),
        in_specs=[pl.BlockSpec((rows, d), lambda l, j: (0, 0)),
                  pl.BlockSpec((1, d, tn), lambda l, j: (l, 0, j)),
                  pl.BlockSpec((1, 1, tn), lambda l, j: (l, 0, j))],
        out_specs=pl.BlockSpec((1, rows, tn), lambda l, j: (l, 0, j)),
        compiler_params=_params(("arbitrary", "arbitrary")),
        name="ada_mod",
    )(cc, ada_w, ada_b.reshape(depth, 1, n))


def _inproj_body(h, w_ref, rope_refs, gn_ref, gmat_ref, o_refs, *, blocks, ones_cols, chunk, use_rope):
    n = len(blocks) * LANES
    for oi, oc in ones_cols:
        o_refs[oi][0, :, oc:oc + LANES] = jnp.ones((h.shape[0], LANES), o_refs[oi].dtype)
    starts = list(range(0, n, chunk))
    accs = {starts[0]: _dot(h, w_ref[:, starts[0]:min(starts[0] + chunk, n)])}
    for ci, c0 in enumerate(starts):
        cw = min(chunk, n - c0)
        if ci + 1 < len(starts):
            nxt = starts[ci + 1]
            accs[nxt] = _dot(h, w_ref[:, nxt:min(nxt + chunk, n)])
        acc = accs.pop(c0)
        for j in range(cw // LANES):
            kind, gain_row, factor, oi, oc = blocks[(c0 // LANES) + j]
            v = acc[:, j * LANES:(j + 1) * LANES]
            if kind == "norm":
                v2 = v * v
                hi, lo = _split_bf16(v2)
                ms = _dot(hi, gmat_ref[...]) + _dot(lo, gmat_ref[...])
                v = v * lax.rsqrt(ms + RMS_EPS) * gn_ref[gain_row:gain_row + 1, :]
            if kind in ("rope", "norm") and use_rope:
                cos_ref, sinp_ref, sinm_ref = rope_refs
                v = (v * cos_ref[...] + pltpu.roll(v, 16, 1) * sinp_ref[...]
                     + pltpu.roll(v, LANES - 16, 1) * sinm_ref[...])
            if factor != 1.0:
                v = v * factor
            o_refs[oi][0, :, oc:oc + LANES] = v.astype(o_refs[oi].dtype)


def _inproj_kernel(x_ref, sc_ref, sh_ref, w_ref, cos_ref, sinp_ref, sinm_ref, gn_ref, gmat_ref, *o_refs, **cfg):
    h = (x_ref[0] * (1.0 + sc_ref[0]) + sh_ref[0]).astype(BF16)
    _inproj_body(h, w_ref, (cos_ref, sinp_ref, sinm_ref), gn_ref, gmat_ref, o_refs, **cfg)


def _inproj_call(x, sc, sh, w, blocks, out_defs, rope_tabs, gains, gmat, *, use_rope, name, ones_cols=()):
    b, s, d = x.shape
    n = w.shape[1]
    tm = min(512, s)
    cos, sinp, sinm = rope_tabs
    tab_spec = pl.BlockSpec((tm, LANES), lambda bi, i: (i, 0))
    kern = functools.partial(_inproj_kernel, blocks=tuple(blocks), ones_cols=tuple(ones_cols), chunk=512,
                             use_rope=use_rope)
    return pl.pallas_call(
        kern,
        out_shape=[jax.ShapeDtypeStruct((b, s, nc), dt) for nc, dt in out_defs],
        grid=(b, s // tm),
        in_specs=[pl.BlockSpec((1, tm, d), lambda bi, i: (bi, i, 0)),
                  sc.spec(), sh.spec(),
                  pl.BlockSpec((d, n), lambda bi, i: (0, 0)),
                  tab_spec, tab_spec, tab_spec,
                  pl.BlockSpec(gains.shape, lambda bi, i: (0, 0)),
                  pl.BlockSpec(gmat.shape, lambda bi, i: (0, 0))],
        out_specs=[pl.BlockSpec((1, tm, nc), lambda bi, i: (bi, i, 0)) for nc, _ in out_defs],
        compiler_params=_params(("arbitrary", "arbitrary")),
        name=name,
    )(x, sc.arr, sh.arr, w, cos, sinp, sinm, gains, gmat)


def _rope_tables(s):
    n_freq = HEAD_DIM // 4
    t = jnp.arange(s, dtype=jnp.int32)
    inv = ROPE_THETA ** (-jnp.arange(n_freq, dtype=F32) / n_freq)
    ang_r = (t // GRID_W).astype(F32)[:, None] * inv
    ang_c = (t % GRID_W).astype(F32)[:, None] * inv
    ang = jnp.concatenate([ang_r, ang_r, ang_c, ang_c] * (LANES // HEAD_DIM), axis=-1)
    first = (jnp.arange(LANES) % 32) < 16
    cos, sin = jnp.cos(ang), jnp.sin(ang)
    return cos, jnp.where(first, 0.0, sin), jnp.where(first, -sin, 0.0)


EV_AQ, EV_AK, EV_AV = 0, 512, 1024
EV_BQ = EV_AV + A_HEADS * 2 * LANES
EV_BK = EV_BQ + (B_HEADS // 2) * LANES
EV_BV = EV_BK + LANES
EV_WIDTH = EV_BV + 2 * LANES


def _attn_even_kernel(q_ref, lam_ref, sg_ref, *refs, n_kv, tq, lam_init):
    kv_refs, o_ref = refs[:n_kv], refs[n_kv]
    lo, hi = _lane_masks()
    lo_b, hi_b = lo.astype(BF16), hi.astype(BF16)
    lv = lam_ref[...]
    lam = (jnp.exp(jnp.sum(lv[0:1] * lv[1:2], axis=-1, keepdims=True))
           - jnp.exp(jnp.sum(lv[2:3] * lv[3:4], axis=-1, keepdims=True)) + lam_init)

    def scores(q, kcol):
        qq = jnp.concatenate([q * lo_b, q * hi_b], axis=0)
        ss = [_dot(qq, kv[0, :, kcol:kcol + LANES], _NT) for kv in kv_refs]
        m = ss[0].max(axis=-1, keepdims=True)
        for s in ss[1:]:
            m = jnp.maximum(m, s.max(axis=-1, keepdims=True))
        return ss, m

    def weighted(ss, m, vcol):
        acc = None
        for s, kv in zip(ss, kv_refs):
            part = _dot(jnp.exp2(s - m).astype(BF16), kv[0, :, vcol:vcol + 2 * LANES])
            acc = part if acc is None else acc + part
        return acc[:, :LANES] * (1.0 / acc[:, LANES:])

    units = ([("a", h, EV_AQ + h * LANES, EV_AK + h * LANES, EV_AV + h * 2 * LANES) for h in range(A_HEADS)]
             + [("b", j, EV_BQ + j * LANES, EV_BK, EV_BV) for j in range(B_HEADS // 2)])

    def finish(unit, ss, m):
        kind, idx, _, _, vcol = unit
        on = weighted(ss, m, vcol)
        if kind == "a":
            o = on[:tq] - lam * on[tq:]
            ms = jnp.mean(o * o, axis=-1, keepdims=True)
            o = o * lax.rsqrt(ms + RMS_EPS) * sg_ref[...] * (1.0 - lam_init)
            o_ref[0, :, idx * LANES:(idx + 1) * LANES] = o.astype(o_ref.dtype)
        else:
            o = on[:tq] * lo + on[tq:] * hi
            o_ref[0, :, 512 + idx * LANES:512 + (idx + 1) * LANES] = o.astype(o_ref.dtype)

    pending = None
    for unit in units:
        cur = scores(q_ref[0, :, unit[2]:unit[2] + LANES], unit[3])
        if pending is not None:
            finish(*pending)
        pending = (unit,) + cur
    finish(*pending)


def _attn_even_call(q_arr, kv_arrs, lamv, subln_g, lam_init, name):
    b, sq, n = q_arr.shape
    tq = min(256, sq)
    kern = functools.partial(_attn_even_kernel, n_kv=len(kv_arrs), tq=tq, lam_init=lam_init)
    return pl.pallas_call(
        kern,
        out_shape=jax.ShapeDtypeStruct((b, sq, 1024), BF16),
        grid=(b, sq // tq),
        in_specs=[pl.BlockSpec((1, tq, n), lambda bi, i: (bi, i, 0)),
                  pl.BlockSpec(lamv.shape, lambda bi, i: (0, 0)),
                  pl.BlockSpec(subln_g.shape, lambda bi, i: (0, 0))]
                 + [pl.BlockSpec((1,) + a.shape[1:], lambda bi, i: (bi, 0, 0)) for a in kv_arrs],
        out_specs=pl.BlockSpec((1, tq, 1024), lambda bi, i: (bi, i, 0)),
        compiler_params=_params(("arbitrary", "arbitrary")),
        name=name,
    )(q_arr, lamv, subln_g, *kv_arrs)


def _odd_mixer_kernel(qkv_ref, ckv_ref, bias_ref, du_ref, band_ref, edge_ref, pw_ref, ps_ref, o_ref, *,
                      tq, slab_rows, n_rows, n_blk, seq):
    i = pl.program_id(1)
    lo, hi = _lane_masks()
    lo_b, hi_b = lo.astype(BF16), hi.astype(BF16)
    t0 = pl.multiple_of(i * tq, tq)
    base = jnp.clip(i * NA_QROWS - NA_ROWS // 2, 0, n_rows - slab_rows)
    k0 = pl.multiple_of(base * GRID_W, GRID_W)
    nk = slab_rows * GRID_W
    cw = C_HEADS * HEAD_DIM

    def scores(j):
        cs = slice(j * LANES, (j + 1) * LANES)
        q = qkv_ref[0, pl.ds(t0, tq), cs]
        qq = jnp.concatenate([q * lo_b, q * hi_b], axis=0)
        kl = qkv_ref[0, pl.ds(k0, nk), cw + j * LANES:cw + (j + 1) * LANES]
        bias = jnp.concatenate([bias_ref[2 * j], bias_ref[2 * j + 1]], axis=0)
        s_l = _dot(qq, kl, _NT) + bias
        s_c = _dot(qq, ckv_ref[0, :, cs], _NT)
        m = jnp.maximum(s_l.max(axis=-1, keepdims=True), s_c.max(axis=-1, keepdims=True))
        return j, s_l, s_c, m

    def finish(j, s_l, s_c, m):
        vl = qkv_ref[0, pl.ds(k0, nk), 2 * cw + 2 * j * LANES:2 * cw + 2 * (j + 1) * LANES]
        vc = ckv_ref[0, :, cw + 2 * j * LANES:cw + 2 * (j + 1) * LANES]
        acc = _dot(jnp.exp2(s_l - m).astype(BF16), vl) + _dot(jnp.exp2(s_c - m).astype(BF16), vc)
        on = acc[:, :LANES] * (1.0 / acc[:, LANES:])
        o = on[:tq] * lo + on[tq:] * hi
        o_ref[0, :, j * LANES:(j + 1) * LANES] = o.astype(o_ref.dtype)

    pending = None
    for j in range(C_HEADS // 2):
        cur = scores(j)
        if pending is not None:
            finish(*pending)
        pending = cur
    finish(*pending)

    tprev = pl.multiple_of(jnp.maximum(i - 1, 0) * tq, tq)
    tnext = pl.multiple_of(jnp.minimum(i + 1, n_blk - 1) * tq, tq)
    has_prev = jnp.where(i > 0, 1.0, 0.0).astype(F32)
    has_next = jnp.where(i < n_blk - 1, 1.0, 0.0).astype(F32)
    tpos = t0 + lax.broadcasted_iota(jnp.int32, (tq, 1), 0)
    for g in range(D_GROUPS):
        half = POOL_WINDOWS[g] // 2
        gs = slice(g * D_GROUP_DIM, (g + 1) * D_GROUP_DIM)
        cur = du_ref[0, pl.ds(t0, tq), gs]
        prv = du_ref[0, pl.ds(tprev + (tq - POOL_EDGE), POOL_EDGE), gs]
        nxt = du_ref[0, pl.ds(tnext, POOL_EDGE), gs]

        def band_sum(band, u):
            u_hi, u_lo = _split_bf16(u)
            return _dot(band, u_hi) + _dot(band, u_lo)

        wsum = band_sum(band_ref[g], cur)
        wsum = jnp.concatenate([wsum[:POOL_EDGE] + has_prev * band_sum(edge_ref[g, 0], prv),
                                wsum[POOL_EDGE:tq - POOL_EDGE],
                                wsum[tq - POOL_EDGE:] + has_next * band_sum(edge_ref[g, 1], nxt)], axis=0)
        cnt = (jnp.minimum(tpos + half, seq) - jnp.maximum(tpos - half, 0)).astype(F32)
        pooled = wsum * (1.0 / cnt) - cur
        od = _dot(pooled.astype(BF16), pw_ref[g]) * ps_ref[:, gs]
        o_ref[0, :, cw + g * D_GROUP_DIM:cw + (g + 1) * D_GROUP_DIM] = od.astype(o_ref.dtype)


def _na_geometry(n_rows):
    kh = min(NA_ROWS, n_rows)
    slab = min(n_rows, NA_QROWS + kh)
    n_blk = n_rows // NA_QROWS
    bases = [min(max(i * NA_QROWS - NA_ROWS // 2, 0), n_rows - slab) for i in range(n_blk)]
    sigs, type_of = [], []
    for i in range(n_blk):
        sig = tuple((min(max(i * NA_QROWS + r - kh // 2, 0), n_rows - kh) - bases[i],
                     i * NA_QROWS + r - bases[i]) for r in range(NA_QROWS))
        if sig not in sigs:
            sigs.append(sig)
        type_of.append(sigs.index(sig))
    return kh, slab, n_blk, sigs, type_of


def _na_bias_table(rpb, n_rows):
    kh, slab, _, sigs, _ = _na_geometry(n_rows)
    kw = NA_COLS
    n_dc = 2 * NA_COLS - 1
    w = np.arange(GRID_W)
    cstart = np.clip(w - kw // 2, 0, GRID_W - kw)
    col_ok = (w[None, :] >= cstart[:, None]) & (w[None, :] < cstart[:, None] + kw)
    dc = np.clip(w[None, :] - w[:, None] + (NA_COLS - 1), 0, n_dc - 1)
    heads = rpb.shape[0]
    onehot = ((dc[None] == np.arange(n_dc)[:, None, None]) & col_ok[None]).astype(np.float32)
    toep = jnp.einsum("hrd,dwj->hwrj", rpb, jnp.asarray(onehot), precision=lax.Precision.HIGHEST)
    toep = jnp.where(jnp.asarray(col_ok)[:, None, :], toep, NEG_BIG)
    pad = slab + NA_QROWS
    toep = jnp.pad(toep, ((0, 0), (0, 0), (pad, pad), (0, 0)), constant_values=NEG_BIG)
    toep = toep.reshape(heads, GRID_W, -1)
    blocks = []
    for sig in sigs:
        for rs_rel, qr_rel in sig:
            d0 = pad - qr_rel + (NA_ROWS - 1)
            in_win = np.repeat(np.array([rs_rel <= m < rs_rel + kh for m in range(slab)]), GRID_W)
            blk = lax.slice_in_dim(toep, d0 * GRID_W, (d0 + slab) * GRID_W, axis=2)
            blocks.append(jnp.where(jnp.asarray(in_win), blk, NEG_BIG))
    big = jnp.stack(blocks, axis=1)
    return big.reshape(heads, len(sigs), NA_QROWS * GRID_W, slab * GRID_W)


POOL_EDGE = 16


def _pool_bands(tq):
    t = jnp.arange(tq, dtype=jnp.int32)[:, None]
    sidx = jnp.arange(tq, dtype=jnp.int32)[None, :]
    main, edge = [], []
    for wdw in POOL_WINDOWS:
        half = wdw // 2
        full = [((sidx + (m - 1) * tq >= t - half) & (sidx + (m - 1) * tq < t + half)).astype(BF16)
                for m in range(3)]
        main.append(full[1])
        edge.append(jnp.stack([full[0][:POOL_EDGE, tq - POOL_EDGE:], full[2][tq - POOL_EDGE:, :POOL_EDGE]]))
    return jnp.stack(main), jnp.stack(edge)


def _odd_mixer_call(qkv, ckv, rpb, du, pool_w, pool_scale):
    b, s, n = qkv.shape
    n_rows = s // GRID_W
    _, slab, n_blk, _, type_of = _na_geometry(n_rows)
    tq = NA_QROWS * GRID_W
    bias = _na_bias_table(rpb, n_rows)
    bands, edges = _pool_bands(tq)

    def bias_map(bi, i):
        t = jnp.int32(type_of[-1])
        for blk in range(n_blk - 2, -1, -1):
            t = jnp.where(i == blk, jnp.int32(type_of[blk]), t)
        return (0, t, 0, 0)

    kern = functools.partial(_odd_mixer_kernel, tq=tq, slab_rows=slab, n_rows=n_rows, n_blk=n_blk, seq=s)
    return pl.pallas_call(
        kern,
        out_shape=jax.ShapeDtypeStruct((b, s, 1024), BF16),
        grid=(b, n_blk),
        in_specs=[pl.BlockSpec((1, s, n), lambda bi, i: (bi, 0, 0)),
                  pl.BlockSpec((1,) + ckv.shape[1:], lambda bi, i: (bi, 0, 0)),
                  pl.BlockSpec((bias.shape[0], None) + bias.shape[2:], bias_map),
                  pl.BlockSpec((1, s, du.shape[2]), lambda bi, i: (bi, 0, 0)),
                  pl.BlockSpec(bands.shape, lambda bi, i: (0, 0, 0)),
                  pl.BlockSpec(edges.shape, lambda bi, i: (0, 0, 0, 0)),
                  pl.BlockSpec(pool_w.shape, lambda bi, i: (0, 0, 0)),
                  pl.BlockSpec(pool_scale.shape, lambda bi, i: (0, 0))],
        out_specs=pl.BlockSpec((1, tq, 1024), lambda bi, i: (bi, i, 0)),
        compiler_params=_params(("arbitrary", "arbitrary")),
        name="odd_mixer",
    )(qkv, ckv, bias, du, bands, edges, pool_w, pool_scale)


OUTPROJ_SUB = 256


def _outproj_kernel(mix_ref, w_ref, x_ref, g1_ref, lng_ref, lnb_ref, sc_ref, sh_ref, r2_ref, rh_ref,
                    x1_ref, hm_ref, aff_ref, *, n_exp):
    tm = x_ref.shape[1]
    subs = [slice(r0, r0 + OUTPROJ_SUB) for r0 in range(0, tm, OUTPROJ_SUB)]
    ys = [_dot(mix_ref[0, rows, :], w_ref[...]) for rows in subs]
    for rows, y in zip(subs, ys):
        x1 = _layer_norm(DEEPNORM_ALPHA * x_ref[0, rows, :] + g1_ref[0] * y, lng_ref[...], lnb_ref[...])
        x1_ref[0, rows, :] = x1
        hm = x1 * (1.0 + sc_ref[0]) + sh_ref[0]
        hm_hi = hm.astype(BF16)
        hm_ref[0, rows, :] = hm_hi
        hm_lo = (hm - hm_hi.astype(F32)).astype(BF16)
        part = _dot(hm_hi, r2_ref[...])
        logits = part[:, :n_exp] + part[:, n_exp:] + _dot(hm_lo, rh_ref[...])
        ex = jnp.exp(logits - logits.max(axis=-1, keepdims=True))
        aff_ref[0, rows, :] = ex * (1.0 / ex.sum(axis=-1, keepdims=True))


def _outproj_call(mix, w, x, g1, lng, lnb, sc2, sh2, router, name):
    b, s, d = x.shape
    dm = mix.shape[2]
    e = router.shape[1]
    tm = min(2 * OUTPROJ_SUB, s)
    r_hi = router.astype(BF16)
    r_lo = (router - r_hi.astype(F32)).astype(BF16)
    r2 = jnp.concatenate([r_hi, r_lo], axis=1)
    row = pl.BlockSpec((1, d), lambda bi, i: (0, 0))
    tile = lambda n: pl.BlockSpec((1, tm, n), lambda bi, i: (bi, i, 0))
    return pl.pallas_call(
        functools.partial(_outproj_kernel, n_exp=e),
        out_shape=[jax.ShapeDtypeStruct((b, s, d), F32),
                   jax.ShapeDtypeStruct((b, s, d), BF16),
                   jax.ShapeDtypeStruct((b, s, e), F32)],
        grid=(b, s // tm),
        in_specs=[tile(dm),
                  pl.BlockSpec((dm, d), lambda bi, i: (0, 0)),
                  tile(d),
                  g1.spec(), row, row, sc2.spec(), sh2.spec(),
                  pl.BlockSpec((d, 2 * e), lambda bi, i: (0, 0)),
                  pl.BlockSpec((d, e), lambda bi, i: (0, 0))],
        out_specs=[tile(d), tile(d), tile(e)],
        compiler_params=_params(("arbitrary", "arbitrary")),
        name=name,
    )(mix, w, x, g1.arr, lng, lnb, sc2.arr, sh2.arr, r2, r_hi)


def _lane_cumsum(m):
    rows, s = m.shape
    r_i = lax.broadcasted_iota(jnp.int32, (LANES, LANES), 0)
    c_i = lax.broadcasted_iota(jnp.int32, (LANES, LANES), 1)
    tri = jnp.where(r_i <= c_i, 1.0, 0.0).astype(BF16)
    carry = jnp.zeros((rows, 1), F32)
    out = []
    for c in range(s // LANES):
        blk = m[:, c * LANES:(c + 1) * LANES]
        out.append(_dot(blk.astype(BF16), tri) + carry)
        carry = carry + blk.sum(axis=-1, keepdims=True)
    return jnp.concatenate(out, axis=-1)


ROUTE_TILE = 256


def _route_kernel(aff_ref, pos_ref, cnt_ref, *, cap):
    a = aff_ref[...]
    thr = jnp.zeros((a.shape[0], 1), jnp.int32)
    for bit in range(30, -1, -1):
        cand = thr | jnp.int32(1 << bit)
        cnt = jnp.where(a >= pltpu.bitcast(cand, F32), 1.0, 0.0).sum(axis=-1, keepdims=True)
        thr = jnp.where(cnt >= cap, cand, thr)
    gt = jnp.where(a >= pltpu.bitcast(thr + 1, F32), 1.0, 0.0)
    eq = jnp.where(a >= pltpu.bitcast(thr, F32), 1.0, 0.0) - gt
    need = cap - gt.sum(axis=-1, keepdims=True)
    sel = gt + eq * jnp.where(_lane_cumsum(eq) <= need, 1.0, 0.0)
    pos_ref[...] = jnp.where(sel > 0.5, _lane_cumsum(sel) - 1.0, -1.0)
    tok = lax.broadcasted_iota(jnp.int32, (a.shape[1], LANES), 0)
    tile = lax.broadcasted_iota(jnp.int32, (a.shape[1], LANES), 1)
    before = jnp.where(tok < tile * ROUTE_TILE, 1.0, 0.0).astype(BF16)
    cnt_ref[...] = _dot(sel.astype(BF16), before)


def _route_call(aff, cap, name):
    b, e, s = aff.shape
    n_tiles = s // min(ROUTE_TILE, s)
    full = pl.BlockSpec((b * e, s), lambda i: (0, 0))
    pos, cnt = pl.pallas_call(
        functools.partial(_route_kernel, cap=cap),
        out_shape=[jax.ShapeDtypeStruct((b * e, s), F32), jax.ShapeDtypeStruct((b * e, LANES), F32)],
        grid=(1,),
        in_specs=[full],
        out_specs=[full, pl.BlockSpec((b * e, LANES), lambda i: (0, 0))],
        compiler_params=_params(("arbitrary",)),
        name=name,
    )(aff.reshape(b * e, s))
    return pos.reshape(b, e, s), cnt[:, :n_tiles + 1].astype(jnp.int32).reshape(-1)


def _gather_kernel(cnt_ref, pos_ref, aff_ref, hm_ref, xg_ref, gs_ref, *, cap, n_exp, win, n_tiles):
    bi, i = pl.program_id(0), pl.program_id(1)

    @pl.when(i == 0)
    def _():
        xg_ref[...] = jnp.zeros(xg_ref.shape, xg_ref.dtype)
        gs_ref[...] = jnp.zeros(gs_ref.shape, gs_ref.dtype)

    starts, short = [], None
    for ei in range(n_exp):
        base = (bi * n_exp + ei) * (n_tiles + 1) + i
        st = jnp.minimum((cnt_ref[base] // SLOT_ALIGN) * SLOT_ALIGN, cap - win)
        starts.append(pl.multiple_of(st, SLOT_ALIGN))
        miss = cnt_ref[base + 1] > st + win
        short = miss if short is None else jnp.logical_or(short, miss)

    def scatter_rows(n_rows, first, stacked):
        row = lax.broadcasted_iota(jnp.int32, (n_rows, 1), 0).astype(F32)
        blocks = []
        for ei in range(n_exp):
            rel = pos_ref[0, ei:ei + 1, :] if first is None else pos_ref[0, ei:ei + 1, :] - first[ei].astype(F32)
            hit = rel == row
            blocks.append(jnp.where(hit, 1.0, 0.0).astype(BF16))
            gate = jnp.where(hit, aff_ref[0, ei:ei + 1, :], 0.0).sum(axis=-1, keepdims=True)
            rows = slice(None) if first is None else pl.ds(first[ei], n_rows)
            gs_ref[0, ei, rows, :] += gate
        if stacked:
            part = _dot(jnp.concatenate(blocks, axis=0), hm_ref[0])
        for ei in range(n_exp):
            rows = slice(None) if first is None else pl.ds(first[ei], n_rows)
            sub = part[ei * n_rows:(ei + 1) * n_rows] if stacked else _dot(blocks[ei], hm_ref[0])
            xg_ref[0, ei, rows, :] += sub.astype(xg_ref.dtype)

    @pl.when(jnp.logical_not(short))
    def _():
        scatter_rows(win, starts, True)

    @pl.when(short)
    def _():
        scatter_rows(cap, None, False)


def _gather_call(cnt, pos, aff, hm, cap, name):
    b, e, s = pos.shape
    d = hm.shape[2]
    tm = min(ROUTE_TILE, s)
    n_tiles = s // tm
    win = min(64, cap)
    tok = pl.BlockSpec((1, e, tm), lambda bi, i, c: (bi, 0, i))
    return pl.pallas_call(
        functools.partial(_gather_kernel, cap=cap, n_exp=e, win=win, n_tiles=n_tiles),
        out_shape=[jax.ShapeDtypeStruct((b, e, cap, d), BF16),
                   jax.ShapeDtypeStruct((b, e, cap, 1), F32)],
        grid_spec=pltpu.PrefetchScalarGridSpec(
            num_scalar_prefetch=1,
            grid=(b, n_tiles),
            in_specs=[tok, tok, pl.BlockSpec((1, tm, d), lambda bi, i, c: (bi, i, 0))],
            out_specs=[pl.BlockSpec((1, e, cap, d), lambda bi, i, c: (bi, 0, 0, 0)),
                       pl.BlockSpec((1, e, cap, 1), lambda bi, i, c: (bi, 0, 0, 0))]),
        compiler_params=_params(("arbitrary", "arbitrary")),
        name=name,
    )(cnt, pos, aff, hm)


def _ffn_kernel(*refs, n_grp, row_chunks, n_steps):
    xg_refs = refs[:n_grp]
    gs_refs = refs[n_grp:2 * n_grp]
    wg_ref, wu_ref, wd_ref = refs[2 * n_grp:2 * n_grp + 3]
    y_refs = refs[2 * n_grp + 3:3 * n_grp + 3]
    acc_refs = refs[3 * n_grp + 3:]
    f = pl.program_id(1)
    last = pl.num_programs(1) - 1
    wg = wg_ref[0].astype(BF16)
    wu = wu_ref[0].astype(BF16)
    wd = wd_ref[0].astype(BF16)

    def body(first, final):
        for xg_ref, gs_ref, y_ref, acc_ref, nb in zip(xg_refs, gs_refs, y_refs, acc_refs, row_chunks):
            bt, _, cap, d = xg_ref.shape
            for b0 in range(0, bt, nb):
                rows = nb * cap
                r0 = b0 * cap
                x = xg_ref[b0:b0 + nb, 0].reshape(rows, d)
                hg = _dot(x, wg)
                hu = _dot(x, wu)
                hid = (hg * (1.0 / (1.0 + jnp.exp(-hg))) * hu).astype(BF16)
                part = _dot(hid, wd)
                if not first:
                    part = acc_ref[r0:r0 + rows, :] + part
                if final:
                    gate = gs_ref[b0:b0 + nb, 0].reshape(rows, 1)
                    y_ref[b0:b0 + nb, 0] = (part * gate).reshape(nb, cap, d).astype(y_ref.dtype)
                else:
                    acc_ref[r0:r0 + rows, :] = part

    if n_steps == 1:
        body(True, True)
    else:
        pl.when(f == 0)(functools.partial(body, True, False))
        pl.when(jnp.logical_and(f > 0, f < last))(functools.partial(body, False, False))
        pl.when(f == last)(functools.partial(body, False, True))


def _ffn_call(xgs, gss, w_gate, w_up, w_down, layer, name):
    _, e, d, ff = w_gate.shape
    tf = 512 if ff % 512 == 0 else ff
    n_grp = len(xgs)
    row_chunks = []
    for xg in xgs:
        bt, _, cap, _ = xg.shape
        nb = max(1, min(bt, 512 // cap))
        while bt % nb:
            nb -= 1
        row_chunks.append(nb)
    tok = lambda a: pl.BlockSpec((a.shape[0], 1) + a.shape[2:], lambda ei, fi: (0, ei, 0, 0))
    kern = functools.partial(_ffn_kernel, n_grp=n_grp, row_chunks=tuple(row_chunks), n_steps=ff // tf)
    return pl.pallas_call(
        kern,
        out_shape=[jax.ShapeDtypeStruct(xg.shape, BF16) for xg in xgs],
        grid=(e, ff // tf),
        in_specs=[tok(a) for a in xgs] + [tok(a) for a in gss]
                 + [pl.BlockSpec((None, 1, d, tf), lambda ei, fi: (layer, ei, 0, fi)),
                    pl.BlockSpec((None, 1, d, tf), lambda ei, fi: (layer, ei, 0, fi)),
                    pl.BlockSpec((None, 1, tf, d), lambda ei, fi: (layer, ei, fi, 0))],
        out_specs=[tok(a) for a in xgs],
        scratch_shapes=[pltpu.VMEM((xg.shape[0] * xg.shape[2], d), F32) for xg in xgs],
        compiler_params=_params(("arbitrary", "arbitrary")),
        name=name,
    )(*xgs, *gss, w_gate, w_up, w_down)


MXU_DEPTH = 256
SLOT_ALIGN = 16


def _combine_kernel(cnt_ref, pos_ref, y_ref, x_ref, g_ref, lng_ref, lnb_ref, *rest, cap, n_exp, win, n_tiles,
                    proj_cfg=None):
    if proj_cfg is None:
        (o_ref,), proj_refs = rest, ()
    else:
        proj_refs, o_ref = rest[:3] + rest[4:], rest[3]
    bi, i = pl.program_id(0), pl.program_id(1)
    pos = pos_ref[0]
    tm = pos.shape[0]
    grp = MXU_DEPTH // win
    starts, short = [], None
    for ei in range(n_exp):
        base = (bi * n_exp + ei) * (n_tiles + 1) + i
        st = jnp.minimum((cnt_ref[base] // SLOT_ALIGN) * SLOT_ALIGN, cap - win)
        starts.append(st)
        miss = cnt_ref[base + 1] > st + win
        short = miss if short is None else jnp.logical_or(short, miss)

    def windowed():
        lane = lax.broadcasted_iota(jnp.int32, (1, grp * win), 1)
        lane_f = lane.astype(F32)
        acc = None
        for k in range(n_exp // grp):
            tgt, rows = None, []
            for u in range(grp - 1, -1, -1):
                ei = k * grp + u
                st = starts[ei]
                rel = pos[:, ei:ei + 1] - st.astype(F32)
                rel = jnp.where(rel >= 0.0, jnp.where(rel < win, rel + float(u * win), -1.0), -1.0)
                tgt = rel if tgt is None else jnp.where(lane < (u + 1) * win, rel, tgt)
                rows.insert(0, y_ref[0, pl.ds(pl.multiple_of(ei * cap + st, SLOT_ALIGN), win), :])
            onehot = jnp.where(tgt == lane_f, 1.0, 0.0).astype(BF16)
            part = _dot(onehot, jnp.concatenate(rows, axis=0))
            acc = part if acc is None else acc + part
        return acc

    def dense():
        slot = lax.broadcasted_iota(jnp.int32, (1, cap), 1).astype(F32)
        acc = None
        for ei in range(n_exp):
            onehot = jnp.where(pos[:, ei:ei + 1] == slot, 1.0, 0.0).astype(BF16)
            part = _dot(onehot, y_ref[0, ei * cap:(ei + 1) * cap, :])
            acc = part if acc is None else acc + part
        return acc

    acc = lax.cond(short, dense, windowed)
    x_new = _layer_norm(DEEPNORM_ALPHA * x_ref[0] + g_ref[0] * acc, lng_ref[...], lnb_ref[...])
    o_ref[0] = x_new
    if proj_cfg is not None:
        sc_ref, sh_ref, w_ref = proj_refs[:3]
        h = (x_new * (1.0 + sc_ref[0]) + sh_ref[0]).astype(BF16)
        _inproj_body(h, w_ref, None, None, None, proj_refs[3:], **proj_cfg)


def _combine_call(cnt, pos_t, y, x, g2, lng, lnb, cap, name, proj=None):
    b, s, d = x.shape
    e = pos_t.shape[2]
    tm = min(ROUTE_TILE, s)
    n_tiles = s // tm
    win = min(64, cap)
    row = pl.BlockSpec((1, d), lambda bi, i, c: (0, 0))
    tile = lambda n: pl.BlockSpec((1, tm, n), lambda bi, i, c: (bi, i, 0))
    in_specs = [tile(e), pl.BlockSpec((1, e * cap, d), lambda bi, i, c: (bi, 0, 0)), tile(d), g2.spec(), row, row]
    operands = [cnt, pos_t, y.reshape(b, e * cap, d), x, g2.arr, lng, lnb]
    out_shape = [jax.ShapeDtypeStruct((b, s, d), F32)]
    out_specs = [tile(d)]
    proj_cfg = None
    if proj is not None:
        sc, sh, w, blocks, out_defs, ones_cols = proj
        in_specs += [sc.spec(), sh.spec(), pl.BlockSpec(w.shape, lambda bi, i, c: (0, 0))]
        operands += [sc.arr, sh.arr, w]
        out_shape += [jax.ShapeDtypeStruct((b, s, nc), dt) for nc, dt in out_defs]
        out_specs += [tile(nc) for nc, _ in out_defs]
        proj_cfg = dict(blocks=tuple(blocks), ones_cols=tuple(ones_cols), chunk=512, use_rope=False)
    res = pl.pallas_call(
        functools.partial(_combine_kernel, cap=cap, n_exp=e, win=win, n_tiles=n_tiles, proj_cfg=proj_cfg),
        out_shape=out_shape,
        grid_spec=pltpu.PrefetchScalarGridSpec(
            num_scalar_prefetch=1, grid=(b, n_tiles), in_specs=in_specs, out_specs=out_specs),
        compiler_params=_params(("arbitrary", "arbitrary")),
        name=name,
    )(*operands)
    return res[0] if proj is None else res


def _ffn_sublayer(streams, projs, lng, lnb, router, w_gate, w_up, w_down, layer):
    staged = []
    for si, (mix, w_out, x, g1, sc2, sh2, g2) in enumerate(streams):
        tag = f"l{layer}s{si}"
        n_tok = x.shape[1]
        cap = EC_FACTOR * n_tok // N_EXPERTS
        x1, hm, aff_t = _outproj_call(mix, w_out, x, g1, lng[0:1], lnb[0:1], sc2, sh2, router, "outproj_" + tag)
        aff = jnp.swapaxes(aff_t, 1, 2)
        pos, cnt = _route_call(aff, cap, "route_" + tag)
        xg, gs = _gather_call(cnt, pos, aff, hm, cap, "gather_" + tag)
        staged.append((x1, pos, cnt, xg, gs, g2, cap, tag))
    ys = _ffn_call([st[3] for st in staged], [st[4] for st in staged], w_gate, w_up, w_down, layer,
                   f"ffn_l{layer}")
    outs = []
    for (x1, pos, cnt, _, _, g2, cap, tag), y, proj in zip(staged, ys, projs):
        outs.append(_combine_call(cnt, jnp.swapaxes(pos, 1, 2), y, x1, g2, lng[1:2], lnb[1:2], cap,
                                  "combine_" + tag, proj))
    return outs


def kernel(x, c, ctx, c_ctx, ada_w, ada_b, ln_g, ln_b, l0_w_in, l0_w_out, l0_lam_q1, l0_lam_k1, l0_lam_q2,
           l0_lam_k2, l0_subln_g, l0_qnorm_g, l0_knorm_g, l1_w_in, l1_w_out, l1_rpb, l1_pool_w, l1_pool_scale,
           moe_router, moe_w_gate, moe_w_up, moe_w_down):
    b, s, d = x.shape
    n_ctx = ctx.shape[1]

    rows = -(-(b + 1) // 8) * 8
    cc = jnp.zeros((rows, d), F32).at[:b].set(c).at[b].set(c_ctx)
    mod = _ada_call(cc, ada_w, ada_b)

    mod5 = mod.reshape(DEPTH, rows, 6, 1, d)

    def mods(i):
        return ([_Mod(mod5, i, k, None) for k in range(6)], [_Mod(mod5, i, k, b) for k in range(6)])

    rope = _rope_tables(s)
    rope_id = (jnp.ones((n_ctx, LANES), F32), jnp.zeros((n_ctx, LANES), F32), jnp.zeros((n_ctx, LANES), F32))
    gmat = jnp.where((jnp.arange(LANES)[:, None] // HEAD_DIM) == (jnp.arange(LANES)[None, :] // HEAD_DIM),
                     1.0 / HEAD_DIM, 0.0).astype(BF16)
    tile2 = lambda g: jnp.concatenate([g, g]).reshape(1, LANES)

    (sh1, sc1, g1, sh2, sc2, g2), (csh1, csc1, cg1, csh2, csc2, cg2) = mods(0)
    bq0 = 3 * 512
    pair_heads = [hh for j in range(B_HEADS // 2) for hh in (j, j + B_HEADS // 2)]
    w_in_b, w_out_b = l0_w_in.astype(BF16), l0_w_out.astype(BF16)
    w_in0 = jnp.concatenate([w_in_b[:, :bq0]]
                            + [w_in_b[:, bq0 + hh * HEAD_DIM:bq0 + (hh + 1) * HEAD_DIM] for hh in pair_heads]
                            + [w_in_b[:, bq0 + 512:]], axis=1)
    w_out0 = jnp.concatenate([w_out_b[:512]]
                             + [w_out_b[512 + hh * HEAD_DIM:512 + (hh + 1) * HEAD_DIM] for hh in pair_heads], axis=0)
    gains0 = jnp.concatenate([tile2(l0_qnorm_g), tile2(l0_knorm_g)], axis=0)
    q_exp2 = Q_SCALE * math.log2(math.e)
    blocks0 = ([("rope", 0, q_exp2, 0, EV_AQ + k * LANES) for k in range(4)]
               + [("rope", 0, 1.0, 0, EV_AK + k * LANES) for k in range(4)]
               + [("plain", 0, 1.0, 0, EV_AV + k * 2 * LANES) for k in range(4)]
               + [("norm", 0, q_exp2, 0, EV_BQ + k * LANES) for k in range(4)]
               + [("norm", 1, 1.0, 0, EV_BK), ("plain", 0, 1.0, 0, EV_BV)])
    ones0 = [(0, EV_AV + (2 * k + 1) * LANES) for k in range(4)] + [(0, EV_BV + LANES)]
    qkv = _inproj_call(x, sc1, sh1, w_in0, blocks0, [(EV_WIDTH, BF16)], rope, gains0, gmat,
                       use_rope=True, name="inproj_l0", ones_cols=ones0)[0]
    qkv_c = _inproj_call(ctx, csc1, csh1, w_in0, blocks0, [(EV_WIDTH, BF16)], rope_id, gains0, gmat,
                         use_rope=False, name="inproj_l0c", ones_cols=ones0)[0]
    lam_init = 0.8 - 0.6 * math.exp(-0.3 * 0)
    lamv = jnp.stack([l0_lam_q1, l0_lam_k1, l0_lam_q2, l0_lam_k2], axis=0)
    sub_g = l0_subln_g.reshape(1, LANES)
    mix = _attn_even_call(qkv, [qkv, qkv_c], lamv, sub_g, lam_init, "attn_l0")
    mix_c = _attn_even_call(qkv_c, [qkv_c], lamv, sub_g, lam_init, "attn_l0c")
    (sh1n, sc1n, g1n, sh2n, sc2n, g2n), (csh1n, csc1n, _, _, _, _) = mods(1)
    cw = C_HEADS * HEAD_DIM
    w_in1 = l1_w_in.astype(BF16)
    blocks1 = ([("plain", 0, q_exp2, 0, k * LANES) for k in range(4)]
               + [("plain", 0, 1.0, 0, cw + k * LANES) for k in range(4)]
               + [("plain", 0, 1.0, 0, 2 * cw + 2 * k * LANES) for k in range(4)]
               + [("plain", 0, 1.0, 1, k * LANES) for k in range(4)])
    ones1 = [(0, 2 * cw + (2 * k + 1) * LANES) for k in range(4)]
    blocks1c = ([("plain", 0, 1.0, 0, k * LANES) for k in range(4)]
                + [("plain", 0, 1.0, 0, cw + 2 * k * LANES) for k in range(4)])
    ones1c = [(0, cw + (2 * k + 1) * LANES) for k in range(4)]
    proj_lat = (sc1n, sh1n, w_in1, blocks1, [(4 * cw, BF16), (D_GROUPS * D_GROUP_DIM, F32)], ones1)
    proj_ctx = (csc1n, csh1n, w_in1[:, cw:3 * cw], blocks1c, [(3 * cw, BF16)], ones1c)
    (x, qkv1, du), (ctx, ckv) = _ffn_sublayer(
        [(mix, w_out0, x, g1, sc2, sh2, g2), (mix_c, w_out0, ctx, cg1, csc2, csh2, cg2)],
        [proj_lat, proj_ctx], ln_g[0], ln_b[0], moe_router[0], moe_w_gate, moe_w_up, moe_w_down, 0)

    mix1 = _odd_mixer_call(qkv1, ckv, l1_rpb * math.log2(math.e), du, l1_pool_w.astype(BF16),
                           l1_pool_scale.reshape(1, -1))
    (x,) = _ffn_sublayer([(mix1, l1_w_out.astype(BF16), x, g1n, sc2n, sh2n, g2n)],
                         [None], ln_g[1], ln_b[1], moe_router[1], moe_w_gate, moe_w_up, moe_w_down, 1)
    return x
```

```python
import collections
import functools
import math

import numpy as np

import jax
import jax.numpy as jnp
from jax import lax
from jax.experimental import pallas as pl
from jax.experimental.pallas import tpu as pltpu

F32 = jnp.float32
BF16 = jnp.bfloat16

DEPTH = 2
GRID_W = 64
HEAD_DIM = 64
A_HEADS = 4
B_HEADS = 8
B_KV_HEADS = 2
C_HEADS = 8
D_GROUPS = 4
D_GROUP_DIM = 128
POOL_WINDOWS = (2, 4, 8, 16)
NA_ROWS = 8
NA_COLS = 16
N_EXPERTS = 16
EC_FACTOR = 2
ROPE_THETA = 10000.0
LN_EPS = 1e-5
RMS_EPS = 1e-6
DEEPNORM_ALPHA = (2 * DEPTH) ** 0.25
Q_SCALE = HEAD_DIM ** -0.5

LANES = 128
VMEM_LIMIT = 56 * 1024 * 1024

NA_QROWS = 4
NEG_BIG = -1e30

_NN = (((1,), (0,)), ((), ()))
_NT = (((1,), (1,)), ((), ()))


def _dot(a, b, dims=_NN):
    return lax.dot_general(a, b, dims, preferred_element_type=F32)


def _split_bf16(a):
    hi = a.astype(BF16)
    lo = (a - hi.astype(F32)).astype(BF16)
    return hi, lo


def _dot3(a, b, dims=_NN):
    a_hi, a_lo = _split_bf16(a)
    b_hi, b_lo = _split_bf16(b)
    return _dot(a_hi, b_hi, dims) + (_dot(a_hi, b_lo, dims) + _dot(a_lo, b_hi, dims))


class _Mod(collections.namedtuple("_Mod", "arr layer k row")):
    def spec(self):
        d = self.arr.shape[-1]

        def index_map(*grid):
            return (self.layer, grid[0] if self.row is None else self.row, self.k, 0, 0)

        return pl.BlockSpec((None, None, 1, 1, d), index_map)


def _params(sem):
    return pltpu.CompilerParams(dimension_semantics=sem, vmem_limit_bytes=VMEM_LIMIT)


def _layer_norm(z, g, b):
    mu = jnp.mean(z, axis=-1, keepdims=True)
    zc = z - mu
    var = jnp.mean(zc * zc, axis=-1, keepdims=True)
    return zc * lax.rsqrt(var + LN_EPS) * g + b


def _lane_masks():
    lane = lax.broadcasted_iota(jnp.int32, (1, LANES), 1)
    lo = jnp.where(lane < HEAD_DIM, 1.0, 0.0).astype(F32)
    return lo, 1.0 - lo


def _ada_kernel(c_ref, w_ref, b_ref, o_ref):
    c = c_ref[...]
    s = c * (1.0 / (1.0 + jnp.exp(-c)))
    o_ref[0] = _dot3(s, w_ref[0]) + b_ref[0]


def _ada_call(cc, ada_w, ada_b):
    depth, d, n = ada_w.shape
    rows = cc.shape[0]
    tn = 1536 if n % 1536 == 0 else n
    return pl.pallas_call(
        _ada_kernel,
        out_shape=jax.ShapeDtypeStruct((depth, rows, n), F32),
        grid=(depth, n // tn),
        in_specs=[pl.BlockSpec((rows, d), lambda l, j: (0, 0)),
                  pl.BlockSpec((1, d, tn), lambda l, j: (l, 0, j)),
                  pl.BlockSpec((1, 1, tn), lambda l, j: (l, 0, j))],
        out_specs=pl.BlockSpec((1, rows, tn), lambda l, j: (l, 0, j)),
        compiler_params=_params(("arbitrary", "arbitrary")),
        name="ada_mod",
    )(cc, ada_w, ada_b.reshape(depth, 1, n))


def _inproj_body(h, w_ref, rope_refs, gn_ref, gmat_ref, o_refs, *, blocks, ones_cols, chunk, use_rope):
    n = len(blocks) * LANES
    for oi, oc in ones_cols:
        o_refs[oi][0, :, oc:oc + LANES] = jnp.ones((h.shape[0], LANES), o_refs[oi].dtype)
    starts = list(range(0, n, chunk))
    accs = {starts[0]: _dot(h, w_ref[:, starts[0]:min(starts[0] + chunk, n)])}
    for ci, c0 in enumerate(starts):
        cw = min(chunk, n - c0)
        if ci + 1 < len(starts):
            nxt = starts[ci + 1]
            accs[nxt] = _dot(h, w_ref[:, nxt:min(nxt + chunk, n)])
        acc = accs.pop(c0)
        for j in range(cw // LANES):
            kind, gain_row, factor, oi, oc = blocks[(c0 // LANES) + j]
            v = acc[:, j * LANES:(j + 1) * LANES]
            if kind == "norm":
                v2 = v * v
                hi, lo = _split_bf16(v2)
                ms = _dot(hi, gmat_ref[...]) + _dot(lo, gmat_ref[...])
                v = v * lax.rsqrt(ms + RMS_EPS) * gn_ref[gain_row:gain_row + 1, :]
            if kind in ("rope", "norm") and use_rope:
                cos_ref, sinp_ref, sinm_ref = rope_refs
                v = (v * cos_ref[...] + pltpu.roll(v, 16, 1) * sinp_ref[...]
                     + pltpu.roll(v, LANES - 16, 1) * sinm_ref[...])
            if factor != 1.0:
                v = v * factor
            o_refs[oi][0, :, oc:oc + LANES] = v.astype(o_refs[oi].dtype)


def _inproj_kernel(x_ref, sc_ref, sh_ref, w_ref, cos_ref, sinp_ref, sinm_ref, gn_ref, gmat_ref, *o_refs, **cfg):
    h = (x_ref[0] * (1.0 + sc_ref[0]) + sh_ref[0]).astype(BF16)
    _inproj_body(h, w_ref, (cos_ref, sinp_ref, sinm_ref), gn_ref, gmat_ref, o_refs, **cfg)


def _inproj_call(x, sc, sh, w, blocks, out_defs, rope_tabs, gains, gmat, *, use_rope, name, ones_cols=()):
    b, s, d = x.shape
    n = w.shape[1]
    tm = min(512, s)
    cos, sinp, sinm = rope_tabs
    tab_spec = pl.BlockSpec((tm, LANES), lambda bi, i: (i, 0))
    kern = functools.partial(_inproj_kernel, blocks=tuple(blocks), ones_cols=tuple(ones_cols), chunk=512,
                             use_rope=use_rope)
    return pl.pallas_call(
        kern,
        out_shape=[jax.ShapeDtypeStruct((b, s, nc), dt) for nc, dt in out_defs],
        grid=(b, s // tm),
        in_specs=[pl.BlockSpec((1, tm, d), lambda bi, i: (bi, i, 0)),
                  sc.spec(), sh.spec(),
                  pl.BlockSpec((d, n), lambda bi, i: (0, 0)),
                  tab_spec, tab_spec, tab_spec,
                  pl.BlockSpec(gains.shape, lambda bi, i: (0, 0)),
                  pl.BlockSpec(gmat.shape, lambda bi, i: (0, 0))],
        out_specs=[pl.BlockSpec((1, tm, nc), lambda bi, i: (bi, i, 0)) for nc, _ in out_defs],
        compiler_params=_params(("arbitrary", "arbitrary")),
        name=name,
    )(x, sc.arr, sh.arr, w, cos, sinp, sinm, gains, gmat)


def _rope_tables(s):
    n_freq = HEAD_DIM // 4
    t = jnp.arange(s, dtype=jnp.int32)
    inv = ROPE_THETA ** (-jnp.arange(n_freq, dtype=F32) / n_freq)
    ang_r = (t // GRID_W).astype(F32)[:, None] * inv
    ang_c = (t % GRID_W).astype(F32)[:, None] * inv
    ang = jnp.concatenate([ang_r, ang_r, ang_c, ang_c] * (LANES // HEAD_DIM), axis=-1)
    first = (jnp.arange(LANES) % 32) < 16
    cos, sin = jnp.cos(ang), jnp.sin(ang)
    return cos, jnp.where(first, 0.0, sin), jnp.where(first, -sin, 0.0)


EV_AQ, EV_AK, EV_AV = 0, 512, 1024
EV_BQ = EV_AV + A_HEADS * 2 * LANES
EV_BK = EV_BQ + (B_HEADS // 2) * LANES
EV_BV = EV_BK + LANES
EV_WIDTH = EV_BV + 2 * LANES


def _attn_even_kernel(q_ref, lam_ref, sg_ref, *refs, n_kv, tq, lam_init):
    kv_refs, o_ref = refs[:n_kv], refs[n_kv]
    lo, hi = _lane_masks()
    lo_b, hi_b = lo.astype(BF16), hi.astype(BF16)
    lv = lam_ref[...]
    lam = (jnp.exp(jnp.sum(lv[0:1] * lv[1:2], axis=-1, keepdims=True))
           - jnp.exp(jnp.sum(lv[2:3] * lv[3:4], axis=-1, keepdims=True)) + lam_init)

    def scores(q, kcol):
        qq = jnp.concatenate([q * lo_b, q * hi_b], axis=0)
        ss = [_dot(qq, kv[0, :, kcol:kcol + LANES], _NT) for kv in kv_refs]
        m = ss[0].max(axis=-1, keepdims=True)
        for s in ss[1:]:
            m = jnp.maximum(m, s.max(axis=-1, keepdims=True))
        return ss, m

    def weighted(ss, m, vcol):
        acc = None
        for s, kv in zip(ss, kv_refs):
            part = _dot(jnp.exp2(s - m).astype(BF16), kv[0, :, vcol:vcol + 2 * LANES])
            acc = part if acc is None else acc + part
        return acc[:, :LANES] * (1.0 / acc[:, LANES:])

    units = ([("a", h, EV_AQ + h * LANES, EV_AK + h * LANES, EV_AV + h * 2 * LANES) for h in range(A_HEADS)]
             + [("b", j, EV_BQ + j * LANES, EV_BK, EV_BV) for j in range(B_HEADS // 2)])

    def finish(unit, ss, m):
        kind, idx, _, _, vcol = unit
        on = weighted(ss, m, vcol)
        if kind == "a":
            o = on[:tq] - lam * on[tq:]
            ms = jnp.mean(o * o, axis=-1, keepdims=True)
            o = o * lax.rsqrt(ms + RMS_EPS) * sg_ref[...] * (1.0 - lam_init)
            o_ref[0, :, idx * LANES:(idx + 1) * LANES] = o.astype(o_ref.dtype)
        else:
            o = on[:tq] * lo + on[tq:] * hi
            o_ref[0, :, 512 + idx * LANES:512 + (idx + 1) * LANES] = o.astype(o_ref.dtype)

    pending = None
    for unit in units:
        cur = scores(q_ref[0, :, unit[2]:unit[2] + LANES], unit[3])
        if pending is not None:
            finish(*pending)
        pending = (unit,) + cur
    finish(*pending)


def _attn_even_call(q_arr, kv_arrs, lamv, subln_g, lam_init, name):
    b, sq, n = q_arr.shape
    tq = min(256, sq)
    kern = functools.partial(_attn_even_kernel, n_kv=len(kv_arrs), tq=tq, lam_init=lam_init)
    return pl.pallas_call(
        kern,
        out_shape=jax.ShapeDtypeStruct((b, sq, 1024), BF16),
        grid=(b, sq // tq),
        in_specs=[pl.BlockSpec((1, tq, n), lambda bi, i: (bi, i, 0)),
                  pl.BlockSpec(lamv.shape, lambda bi, i: (0, 0)),
                  pl.BlockSpec(subln_g.shape, lambda bi, i: (0, 0))]
                 + [pl.BlockSpec((1,) + a.shape[1:], lambda bi, i: (bi, 0, 0)) for a in kv_arrs],
        out_specs=pl.BlockSpec((1, tq, 1024), lambda bi, i: (bi, i, 0)),
        compiler_params=_params(("arbitrary", "arbitrary")),
        name=name,
    )(q_arr, lamv, subln_g, *kv_arrs)


def _odd_mixer_kernel(qkv_ref, ckv_ref, bias_ref, du_ref, band_ref, edge_ref, pw_ref, ps_ref, o_ref, *,
                      tq, slab_rows, n_rows, n_blk, seq):
    i = pl.program_id(1)
    lo, hi = _lane_masks()
    lo_b, hi_b = lo.astype(BF16), hi.astype(BF16)
    t0 = pl.multiple_of(i * tq, tq)
    base = jnp.clip(i * NA_QROWS - NA_ROWS // 2, 0, n_rows - slab_rows)
    k0 = pl.multiple_of(base * GRID_W, GRID_W)
    nk = slab_rows * GRID_W
    cw = C_HEADS * HEAD_DIM

    def scores(j):
        cs = slice(j * LANES, (j + 1) * LANES)
        q = qkv_ref[0, pl.ds(t0, tq), cs]
        qq = jnp.concatenate([q * lo_b, q * hi_b], axis=0)
        kl = qkv_ref[0, pl.ds(k0, nk), cw + j * LANES:cw + (j + 1) * LANES]
        bias = jnp.concatenate([bias_ref[2 * j], bias_ref[2 * j + 1]], axis=0)
        s_l = _dot(qq, kl, _NT) + bias
        s_c = _dot(qq, ckv_ref[0, :, cs], _NT)
        m = jnp.maximum(s_l.max(axis=-1, keepdims=True), s_c.max(axis=-1, keepdims=True))
        return j, s_l, s_c, m

    def finish(j, s_l, s_c, m):
        vl = qkv_ref[0, pl.ds(k0, nk), 2 * cw + 2 * j * LANES:2 * cw + 2 * (j + 1) * LANES]
        vc = ckv_ref[0, :, cw + 2 * j * LANES:cw + 2 * (j + 1) * LANES]
        acc = _dot(jnp.exp2(s_l - m).astype(BF16), vl) + _dot(jnp.exp2(s_c - m).astype(BF16), vc)
        on = acc[:, :LANES] * (1.0 / acc[:, LANES:])
        o = on[:tq] * lo + on[tq:] * hi
        o_ref[0, :, j * LANES:(j + 1) * LANES] = o.astype(o_ref.dtype)

    pending = None
    for j in range(C_HEADS // 2):
        cur = scores(j)
        if pending is not None:
            finish(*pending)
        pending = cur
    finish(*pending)

    tprev = pl.multiple_of(jnp.maximum(i - 1, 0) * tq, tq)
    tnext = pl.multiple_of(jnp.minimum(i + 1, n_blk - 1) * tq, tq)
    has_prev = jnp.where(i > 0, 1.0, 0.0).astype(F32)
    has_next = jnp.where(i < n_blk - 1, 1.0, 0.0).astype(F32)
    tpos = t0 + lax.broadcasted_iota(jnp.int32, (tq, 1), 0)
    for g in range(D_GROUPS):
        half = POOL_WINDOWS[g] // 2
        gs = slice(g * D_GROUP_DIM, (g + 1) * D_GROUP_DIM)
        cur = du_ref[0, pl.ds(t0, tq), gs]
        prv = du_ref[0, pl.ds(tprev + (tq - POOL_EDGE), POOL_EDGE), gs]
        nxt = du_ref[0, pl.ds(tnext, POOL_EDGE), gs]

        def band_sum(band, u):
            u_hi, u_lo = _split_bf16(u)
            return _dot(band, u_hi) + _dot(band, u_lo)

        wsum = band_sum(band_ref[g], cur)
        wsum = jnp.concatenate([wsum[:POOL_EDGE] + has_prev * band_sum(edge_ref[g, 0], prv),
                                wsum[POOL_EDGE:tq - POOL_EDGE],
                                wsum[tq - POOL_EDGE:] + has_next * band_sum(edge_ref[g, 1], nxt)], axis=0)
        cnt = (jnp.minimum(tpos + half, seq) - jnp.maximum(tpos - half, 0)).astype(F32)
        pooled = wsum * (1.0 / cnt) - cur
        od = _dot(pooled.astype(BF16), pw_ref[g]) * ps_ref[:, gs]
        o_ref[0, :, cw + g * D_GROUP_DIM:cw + (g + 1) * D_GROUP_DIM] = od.astype(o_ref.dtype)


def _na_geometry(n_rows):
    kh = min(NA_ROWS, n_rows)
    slab = min(n_rows, NA_QROWS + kh)
    n_blk = n_rows // NA_QROWS
    bases = [min(max(i * NA_QROWS - NA_ROWS // 2, 0), n_rows - slab) for i in range(n_blk)]
    sigs, type_of = [], []
    for i in range(n_blk):
        sig = tuple((min(max(i * NA_QROWS + r - kh // 2, 0), n_rows - kh) - bases[i],
                     i * NA_QROWS + r - bases[i]) for r in range(NA_QROWS))
        if sig not in sigs:
            sigs.append(sig)
        type_of.append(sigs.index(sig))
    return kh, slab, n_blk, sigs, type_of


def _na_bias_table(rpb, n_rows):
    kh, slab, _, sigs, _ = _na_geometry(n_rows)
    kw = NA_COLS
    n_dc = 2 * NA_COLS - 1
    w = np.arange(GRID_W)
    cstart = np.clip(w - kw // 2, 0, GRID_W - kw)
    col_ok = (w[None, :] >= cstart[:, None]) & (w[None, :] < cstart[:, None] + kw)
    dc = np.clip(w[None, :] - w[:, None] + (NA_COLS - 1), 0, n_dc - 1)
    heads = rpb.shape[0]
    onehot = ((dc[None] == np.arange(n_dc)[:, None, None]) & col_ok[None]).astype(np.float32)
    toep = jnp.einsum("hrd,dwj->hwrj", rpb, jnp.asarray(onehot), precision=lax.Precision.HIGHEST)
    toep = jnp.where(jnp.asarray(col_ok)[:, None, :], toep, NEG_BIG)
    pad = slab + NA_QROWS
    toep = jnp.pad(toep, ((0, 0), (0, 0), (pad, pad), (0, 0)), constant_values=NEG_BIG)
    toep = toep.reshape(heads, GRID_W, -1)
    blocks = []
    for sig in sigs:
        for rs_rel, qr_rel in sig:
            d0 = pad - qr_rel + (NA_ROWS - 1)
            in_win = np.repeat(np.array([rs_rel <= m < rs_rel + kh for m in range(slab)]), GRID_W)
            blk = lax.slice_in_dim(toep, d0 * GRID_W, (d0 + slab) * GRID_W, axis=2)
            blocks.append(jnp.where(jnp.asarray(in_win), blk, NEG_BIG))
    big = jnp.stack(blocks, axis=1)
    return big.reshape(heads, len(sigs), NA_QROWS * GRID_W, slab * GRID_W)


POOL_EDGE = 16


def _pool_bands(tq):
    t = jnp.arange(tq, dtype=jnp.int32)[:, None]
    sidx = jnp.arange(tq, dtype=jnp.int32)[None, :]
    main, edge = [], []
    for wdw in POOL_WINDOWS:
        half = wdw // 2
        full = [((sidx + (m - 1) * tq >= t - half) & (sidx + (m - 1) * tq < t + half)).astype(BF16)
                for m in range(3)]
        main.append(full[1])
        edge.append(jnp.stack([full[0][:POOL_EDGE, tq - POOL_EDGE:], full[2][tq - POOL_EDGE:, :POOL_EDGE]]))
    return jnp.stack(main), jnp.stack(edge)


def _odd_mixer_call(qkv, ckv, rpb, du, pool_w, pool_scale):
    b, s, n = qkv.shape
    n_rows = s // GRID_W
    _, slab, n_blk, _, type_of = _na_geometry(n_rows)
    tq = NA_QROWS * GRID_W
    bias = _na_bias_table(rpb, n_rows)
    bands, edges = _pool_bands(tq)

    def bias_map(bi, i):
        t = jnp.int32(type_of[-1])
        for blk in range(n_blk - 2, -1, -1):
            t = jnp.where(i == blk, jnp.int32(type_of[blk]), t)
        return (0, t, 0, 0)

    kern = functools.partial(_odd_mixer_kernel, tq=tq, slab_rows=slab, n_rows=n_rows, n_blk=n_blk, seq=s)
    return pl.pallas_call(
        kern,
        out_shape=jax.ShapeDtypeStruct((b, s, 1024), BF16),
        grid=(b, n_blk),
        in_specs=[pl.BlockSpec((1, s, n), lambda bi, i: (bi, 0, 0)),
                  pl.BlockSpec((1,) + ckv.shape[1:], lambda bi, i: (bi, 0, 0)),
                  pl.BlockSpec((bias.shape[0], None) + bias.shape[2:], bias_map),
                  pl.BlockSpec((1, s, du.shape[2]), lambda bi, i: (bi, 0, 0)),
                  pl.BlockSpec(bands.shape, lambda bi, i: (0, 0, 0)),
                  pl.BlockSpec(edges.shape, lambda bi, i: (0, 0, 0, 0)),
                  pl.BlockSpec(pool_w.shape, lambda bi, i: (0, 0, 0)),
                  pl.BlockSpec(pool_scale.shape, lambda bi, i: (0, 0))],
        out_specs=pl.BlockSpec((1, tq, 1024), lambda bi, i: (bi, i, 0)),
        compiler_params=_params(("arbitrary", "arbitrary")),
        name="odd_mixer",
    )(qkv, ckv, bias, du, bands, edges, pool_w, pool_scale)


OUTPROJ_SUB = 256


def _outproj_kernel(mix_ref, w_ref, x_ref, g1_ref, lng_ref, lnb_ref, sc_ref, sh_ref, r2_ref, rh_ref,
                    x1_ref, hm_ref, aff_ref, *, n_exp):
    tm = x_ref.shape[1]
    subs = [slice(r0, r0 + OUTPROJ_SUB) for r0 in range(0, tm, OUTPROJ_SUB)]
    ys = [_dot(mix_ref[0, rows, :], w_ref[...]) for rows in subs]
    for rows, y in zip(subs, ys):
        x1 = _layer_norm(DEEPNORM_ALPHA * x_ref[0, rows, :] + g1_ref[0] * y, lng_ref[...], lnb_ref[...])
        x1_ref[0, rows, :] = x1
        hm = x1 * (1.0 + sc_ref[0]) + sh_ref[0]
        hm_hi = hm.astype(BF16)
        hm_ref[0, rows, :] = hm_hi
        hm_lo = (hm - hm_hi.astype(F32)).astype(BF16)
        part = _dot(hm_hi, r2_ref[...])
        logits = part[:, :n_exp] + part[:, n_exp:] + _dot(hm_lo, rh_ref[...])
        ex = jnp.exp(logits - logits.max(axis=-1, keepdims=True))
        aff_ref[0, rows, :] = ex * (1.0 / ex.sum(axis=-1, keepdims=True))


def _outproj_call(mix, w, x, g1, lng, lnb, sc2, sh2, router, name):
    b, s, d = x.shape
    dm = mix.shape[2]
    e = router.shape[1]
    tm = min(4 * OUTPROJ_SUB, s)
    r_hi = router.astype(BF16)
    r_lo = (router - r_hi.astype(F32)).astype(BF16)
    r2 = jnp.concatenate([r_hi, r_lo], axis=1)
    row = pl.BlockSpec((1, d), lambda bi, i: (0, 0))
    tile = lambda n: pl.BlockSpec((1, tm, n), lambda bi, i: (bi, i, 0))
    return pl.pallas_call(
        functools.partial(_outproj_kernel, n_exp=e),
        out_shape=[jax.ShapeDtypeStruct((b, s, d), F32),
                   jax.ShapeDtypeStruct((b, s, d), BF16),
                   jax.ShapeDtypeStruct((b, s, e), F32)],
        grid=(b, s // tm),
        in_specs=[tile(dm),
                  pl.BlockSpec((dm, d), lambda bi, i: (0, 0)),
                  tile(d),
                  g1.spec(), row, row, sc2.spec(), sh2.spec(),
                  pl.BlockSpec((d, 2 * e), lambda bi, i: (0, 0)),
                  pl.BlockSpec((d, e), lambda bi, i: (0, 0))],
        out_specs=[tile(d), tile(d), tile(e)],
        compiler_params=_params(("arbitrary", "arbitrary")),
        name=name,
    )(mix, w, x, g1.arr, lng, lnb, sc2.arr, sh2.arr, r2, r_hi)


def _lane_cumsum(m):
    rows, s = m.shape
    r_i = lax.broadcasted_iota(jnp.int32, (LANES, LANES), 0)
    c_i = lax.broadcasted_iota(jnp.int32, (LANES, LANES), 1)
    tri = jnp.where(r_i <= c_i, 1.0, 0.0).astype(BF16)
    carry = jnp.zeros((rows, 1), F32)
    out = []
    for c in range(s // LANES):
        blk = m[:, c * LANES:(c + 1) * LANES]
        out.append(_dot(blk.astype(BF16), tri) + carry)
        carry = carry + blk.sum(axis=-1, keepdims=True)
    return jnp.concatenate(out, axis=-1)


ROUTE_TILE = 256


def _route_kernel(aff_ref, pos_ref, cnt_ref, *, cap):
    a = aff_ref[...]
    thr = jnp.zeros((a.shape[0], 1), jnp.int32)
    for bit in range(30, -1, -1):
        cand = thr | jnp.int32(1 << bit)
        cnt = jnp.where(a >= pltpu.bitcast(cand, F32), 1.0, 0.0).sum(axis=-1, keepdims=True)
        thr = jnp.where(cnt >= cap, cand, thr)
    gt = jnp.where(a >= pltpu.bitcast(thr + 1, F32), 1.0, 0.0)
    eq = jnp.where(a >= pltpu.bitcast(thr, F32), 1.0, 0.0) - gt
    need = cap - gt.sum(axis=-1, keepdims=True)
    sel = gt + eq * jnp.where(_lane_cumsum(eq) <= need, 1.0, 0.0)
    pos_ref[...] = jnp.where(sel > 0.5, _lane_cumsum(sel) - 1.0, -1.0)
    tok = lax.broadcasted_iota(jnp.int32, (a.shape[1], LANES), 0)
    tile = lax.broadcasted_iota(jnp.int32, (a.shape[1], LANES), 1)
    before = jnp.where(tok < tile * ROUTE_TILE, 1.0, 0.0).astype(BF16)
    cnt_ref[...] = _dot(sel.astype(BF16), before)


def _route_call(aff, cap, name):
    b, e, s = aff.shape
    n_tiles = s // min(ROUTE_TILE, s)
    full = pl.BlockSpec((b * e, s), lambda i: (0, 0))
    pos, cnt = pl.pallas_call(
        functools.partial(_route_kernel, cap=cap),
        out_shape=[jax.ShapeDtypeStruct((b * e, s), F32), jax.ShapeDtypeStruct((b * e, LANES), F32)],
        grid=(1,),
        in_specs=[full],
        out_specs=[full, pl.BlockSpec((b * e, LANES), lambda i: (0, 0))],
        compiler_params=_params(("arbitrary",)),
        name=name,
    )(aff.reshape(b * e, s))
    return pos.reshape(b, e, s), cnt[:, :n_tiles + 1].astype(jnp.int32).reshape(-1)


def _gather_kernel(cnt_ref, pos_ref, aff_ref, hm_ref, xg_ref, gs_ref, *, cap, n_exp, win, n_tiles):
    bi, i = pl.program_id(0), pl.program_id(1)

    @pl.when(i == 0)
    def _():
        xg_ref[...] = jnp.zeros(xg_ref.shape, xg_ref.dtype)
        gs_ref[...] = jnp.zeros(gs_ref.shape, gs_ref.dtype)

    starts, short = [], None
    for ei in range(n_exp):
        base = (bi * n_exp + ei) * (n_tiles + 1) + i
        st = jnp.minimum((cnt_ref[base] // SLOT_ALIGN) * SLOT_ALIGN, cap - win)
        starts.append(pl.multiple_of(st, SLOT_ALIGN))
        miss = cnt_ref[base + 1] > st + win
        short = miss if short is None else jnp.logical_or(short, miss)

    def scatter_rows(n_rows, first, stacked):
        row = lax.broadcasted_iota(jnp.int32, (n_rows, 1), 0).astype(F32)
        blocks = []
        for ei in range(n_exp):
            rel = pos_ref[0, ei:ei + 1, :] if first is None else pos_ref[0, ei:ei + 1, :] - first[ei].astype(F32)
            hit = rel == row
            blocks.append(jnp.where(hit, 1.0, 0.0).astype(BF16))
            gate = jnp.where(hit, aff_ref[0, ei:ei + 1, :], 0.0).sum(axis=-1, keepdims=True)
            rows = slice(None) if first is None else pl.ds(first[ei], n_rows)
            gs_ref[0, ei, rows, :] += gate
        if stacked:
            part = _dot(jnp.concatenate(blocks, axis=0), hm_ref[0])
        for ei in range(n_exp):
            rows = slice(None) if first is None else pl.ds(first[ei], n_rows)
            sub = part[ei * n_rows:(ei + 1) * n_rows] if stacked else _dot(blocks[ei], hm_ref[0])
            xg_ref[0, ei, rows, :] += sub.astype(xg_ref.dtype)

    @pl.when(jnp.logical_not(short))
    def _():
        scatter_rows(win, starts, True)

    @pl.when(short)
    def _():
        scatter_rows(cap, None, False)


def _gather_call(cnt, pos, aff, hm, cap, name):
    b, e, s = pos.shape
    d = hm.shape[2]
    tm = min(ROUTE_TILE, s)
    n_tiles = s // tm
    win = min(64, cap)
    tok = pl.BlockSpec((1, e, tm), lambda bi, i, c: (bi, 0, i))
    return pl.pallas_call(
        functools.partial(_gather_kernel, cap=cap, n_exp=e, win=win, n_tiles=n_tiles),
        out_shape=[jax.ShapeDtypeStruct((b, e, cap, d), BF16),
                   jax.ShapeDtypeStruct((b, e, cap, 1), F32)],
        grid_spec=pltpu.PrefetchScalarGridSpec(
            num_scalar_prefetch=1,
            grid=(b, n_tiles),
            in_specs=[tok, tok, pl.BlockSpec((1, tm, d), lambda bi, i, c: (bi, i, 0))],
            out_specs=[pl.BlockSpec((1, e, cap, d), lambda bi, i, c: (bi, 0, 0, 0)),
                       pl.BlockSpec((1, e, cap, 1), lambda bi, i, c: (bi, 0, 0, 0))]),
        compiler_params=_params(("arbitrary", "arbitrary")),
        name=name,
    )(cnt, pos, aff, hm)


def _ffn_kernel(*refs, n_grp, row_chunks, n_steps):
    xg_refs = refs[:n_grp]
    gs_refs = refs[n_grp:2 * n_grp]
    wg_ref, wu_ref, wd_ref = refs[2 * n_grp:2 * n_grp + 3]
    y_refs = refs[2 * n_grp + 3:3 * n_grp + 3]
    acc_refs = refs[3 * n_grp + 3:]
    f = pl.program_id(1)
    last = pl.num_programs(1) - 1
    wg = wg_ref[0].astype(BF16)

    def body(first, final):
        wu = wd = None
        for xg_ref, gs_ref, y_ref, acc_ref, nb in zip(xg_refs, gs_refs, y_refs, acc_refs, row_chunks):
            bt, _, cap, d = xg_ref.shape
            for b0 in range(0, bt, nb):
                rows = nb * cap
                r0 = b0 * cap
                x = xg_ref[b0:b0 + nb, 0].reshape(rows, d)
                hg = _dot(x, wg)
                if wu is None:
                    wu = wu_ref[0].astype(BF16)
                hu = _dot(x, wu)
                if wd is None:
                    wd = wd_ref[0].astype(BF16)
                hid = (hg * (1.0 / (1.0 + jnp.exp(-hg))) * hu).astype(BF16)
                part = _dot(hid, wd)
                if not first:
                    part = acc_ref[r0:r0 + rows, :] + part
                if final:
                    gate = gs_ref[b0:b0 + nb, 0].reshape(rows, 1)
                    y_ref[b0:b0 + nb, 0] = (part * gate).reshape(nb, cap, d).astype(y_ref.dtype)
                else:
                    acc_ref[r0:r0 + rows, :] = part

    if n_steps == 1:
        body(True, True)
    else:
        pl.when(f == 0)(functools.partial(body, True, False))
        pl.when(jnp.logical_and(f > 0, f < last))(functools.partial(body, False, False))
        pl.when(f == last)(functools.partial(body, False, True))


def _ffn_call(xgs, gss, w_gate, w_up, w_down, layer, name):
    _, e, d, ff = w_gate.shape
    tf = 512 if ff % 512 == 0 else ff
    n_grp = len(xgs)
    row_chunks = []
    for xg in xgs:
        bt, _, cap, _ = xg.shape
        nb = max(1, min(bt, 512 // cap))
        while bt % nb:
            nb -= 1
        row_chunks.append(nb)
    tok = lambda a: pl.BlockSpec((a.shape[0], 1) + a.shape[2:], lambda ei, fi: (0, ei, 0, 0))
    kern = functools.partial(_ffn_kernel, n_grp=n_grp, row_chunks=tuple(row_chunks), n_steps=ff // tf)
    return pl.pallas_call(
        kern,
        out_shape=[jax.ShapeDtypeStruct(xg.shape, BF16) for xg in xgs],
        grid=(e, ff // tf),
        in_specs=[tok(a) for a in xgs] + [tok(a) for a in gss]
                 + [pl.BlockSpec((None, 1, d, tf), lambda ei, fi: (layer, ei, 0, fi)),
                    pl.BlockSpec((None, 1, d, tf), lambda ei, fi: (layer, ei, 0, fi)),
                    pl.BlockSpec((None, 1, tf, d), lambda ei, fi: (layer, ei, fi, 0))],
        out_specs=[tok(a) for a in xgs],
        scratch_shapes=[pltpu.VMEM((xg.shape[0] * xg.shape[2], d), F32) for xg in xgs],
        compiler_params=_params(("arbitrary", "arbitrary")),
        name=name,
    )(*xgs, *gss, w_gate, w_up, w_down)


MXU_DEPTH = 256
SLOT_ALIGN = 16


def _combine_kernel(cnt_ref, pos_ref, y_ref, x_ref, g_ref, lng_ref, lnb_ref, *rest, cap, n_exp, win, n_tiles,
                    proj_cfg=None):
    if proj_cfg is None:
        (o_ref,), proj_refs = rest, ()
    else:
        proj_refs, o_ref = rest[:3] + rest[4:], rest[3]
    bi, i = pl.program_id(0), pl.program_id(1)
    pos = pos_ref[0]
    tm = pos.shape[0]
    grp = MXU_DEPTH // win
    starts, short = [], None
    for ei in range(n_exp):
        base = (bi * n_exp + ei) * (n_tiles + 1) + i
        st = jnp.minimum((cnt_ref[base] // SLOT_ALIGN) * SLOT_ALIGN, cap - win)
        starts.append(st)
        miss = cnt_ref[base + 1] > st + win
        short = miss if short is None else jnp.logical_or(short, miss)

    def windowed():
        lane = lax.broadcasted_iota(jnp.int32, (1, grp * win), 1)
        lane_f = lane.astype(F32)
        acc = None
        for k in range(n_exp // grp):
            tgt, rows = None, []
            for u in range(grp - 1, -1, -1):
                ei = k * grp + u
                st = starts[ei]
                rel = pos[:, ei:ei + 1] - st.astype(F32)
                rel = jnp.where(rel >= 0.0, jnp.where(rel < win, rel + float(u * win), -1.0), -1.0)
                tgt = rel if tgt is None else jnp.where(lane < (u + 1) * win, rel, tgt)
                rows.insert(0, y_ref[0, pl.ds(pl.multiple_of(ei * cap + st, SLOT_ALIGN), win), :])
            onehot = jnp.where(tgt == lane_f, 1.0, 0.0).astype(BF16)
            part = _dot(onehot, jnp.concatenate(rows, axis=0))
            acc = part if acc is None else acc + part
        return acc

    def dense():
        slot = lax.broadcasted_iota(jnp.int32, (1, cap), 1).astype(F32)
        acc = None
        for ei in range(n_exp):
            onehot = jnp.where(pos[:, ei:ei + 1] == slot, 1.0, 0.0).astype(BF16)
            part = _dot(onehot, y_ref[0, ei * cap:(ei + 1) * cap, :])
            acc = part if acc is None else acc + part
        return acc

    acc = lax.cond(short, dense, windowed)
    x_new = _layer_norm(DEEPNORM_ALPHA * x_ref[0] + g_ref[0] * acc, lng_ref[...], lnb_ref[...])
    o_ref[0] = x_new
    if proj_cfg is not None:
        sc_ref, sh_ref, w_ref = proj_refs[:3]
        h = (x_new * (1.0 + sc_ref[0]) + sh_ref[0]).astype(BF16)
        _inproj_body(h, w_ref, None, None, None, proj_refs[3:], **proj_cfg)


def _combine_call(cnt, pos_t, y, x, g2, lng, lnb, cap, name, proj=None):
    b, s, d = x.shape
    e = pos_t.shape[2]
    tm = min(ROUTE_TILE, s)
    n_tiles = s // tm
    win = min(64, cap)
    row = pl.BlockSpec((1, d), lambda bi, i, c: (0, 0))
    tile = lambda n: pl.BlockSpec((1, tm, n), lambda bi, i, c: (bi, i, 0))
    in_specs = [tile(e), pl.BlockSpec((1, e * cap, d), lambda bi, i, c: (bi, 0, 0)), tile(d), g2.spec(), row, row]
    operands = [cnt, pos_t, y.reshape(b, e * cap, d), x, g2.arr, lng, lnb]
    out_shape = [jax.ShapeDtypeStruct((b, s, d), F32)]
    out_specs = [tile(d)]
    proj_cfg = None
    if proj is not None:
        sc, sh, w, blocks, out_defs, ones_cols = proj
        in_specs += [sc.spec(), sh.spec(), pl.BlockSpec(w.shape, lambda bi, i, c: (0, 0))]
        operands += [sc.arr, sh.arr, w]
        out_shape += [jax.ShapeDtypeStruct((b, s, nc), dt) for nc, dt in out_defs]
        out_specs += [tile(nc) for nc, _ in out_defs]
        proj_cfg = dict(blocks=tuple(blocks), ones_cols=tuple(ones_cols), chunk=512, use_rope=False)
    res = pl.pallas_call(
        functools.partial(_combine_kernel, cap=cap, n_exp=e, win=win, n_tiles=n_tiles, proj_cfg=proj_cfg),
        out_shape=out_shape,
        grid_spec=pltpu.PrefetchScalarGridSpec(
            num_scalar_prefetch=1, grid=(b, n_tiles), in_specs=in_specs, out_specs=out_specs),
        compiler_params=_params(("arbitrary", "arbitrary")),
        name=name,
    )(*operands)
    return res[0] if proj is None else res


def _ffn_sublayer(streams, projs, lng, lnb, router, w_gate, w_up, w_down, layer):
    staged = []
    for si, (mix, w_out, x, g1, sc2, sh2, g2) in enumerate(streams):
        tag = f"l{layer}s{si}"
        n_tok = x.shape[1]
        cap = EC_FACTOR * n_tok // N_EXPERTS
        x1, hm, aff_t = _outproj_call(mix, w_out, x, g1, lng[0:1], lnb[0:1], sc2, sh2, router, "outproj_" + tag)
        aff = jnp.swapaxes(aff_t, 1, 2)
        pos, cnt = _route_call(aff, cap, "route_" + tag)
        xg, gs = _gather_call(cnt, pos, aff, hm, cap, "gather_" + tag)
        staged.append((x1, pos, cnt, xg, gs, g2, cap, tag))
    ys = _ffn_call([st[3] for st in staged], [st[4] for st in staged], w_gate, w_up, w_down, layer,
                   f"ffn_l{layer}")
    outs = []
    for (x1, pos, cnt, _, _, g2, cap, tag), y, proj in zip(staged, ys, projs):
        outs.append(_combine_call(cnt, jnp.swapaxes(pos, 1, 2), y, x1, g2, lng[1:2], lnb[1:2], cap,
                                  "combine_" + tag, proj))
    return outs


def kernel(x, c, ctx, c_ctx, ada_w, ada_b, ln_g, ln_b, l0_w_in, l0_w_out, l0_lam_q1, l0_lam_k1, l0_lam_q2,
           l0_lam_k2, l0_subln_g, l0_qnorm_g, l0_knorm_g, l1_w_in, l1_w_out, l1_rpb, l1_pool_w, l1_pool_scale,
           moe_router, moe_w_gate, moe_w_up, moe_w_down):
    b, s, d = x.shape
    n_ctx = ctx.shape[1]

    rows = -(-(b + 1) // 8) * 8
    cc = jnp.zeros((rows, d), F32).at[:b].set(c).at[b].set(c_ctx)
    mod = _ada_call(cc, ada_w, ada_b)

    mod5 = mod.reshape(DEPTH, rows, 6, 1, d)

    def mods(i):
        return ([_Mod(mod5, i, k, None) for k in range(6)], [_Mod(mod5, i, k, b) for k in range(6)])

    rope = _rope_tables(s)
    rope_id = (jnp.ones((n_ctx, LANES), F32), jnp.zeros((n_ctx, LANES), F32), jnp.zeros((n_ctx, LANES), F32))
    gmat = jnp.where((jnp.arange(LANES)[:, None] // HEAD_DIM) == (jnp.arange(LANES)[None, :] // HEAD_DIM),
                     1.0 / HEAD_DIM, 0.0).astype(BF16)
    tile2 = lambda g: jnp.concatenate([g, g]).reshape(1, LANES)

    (sh1, sc1, g1, sh2, sc2, g2), (csh1, csc1, cg1, csh2, csc2, cg2) = mods(0)
    bq0 = 3 * 512
    pair_heads = [hh for j in range(B_HEADS // 2) for hh in (j, j + B_HEADS // 2)]
    w_in_b, w_out_b = l0_w_in.astype(BF16), l0_w_out.astype(BF16)
    w_in0 = jnp.concatenate([w_in_b[:, :bq0]]
                            + [w_in_b[:, bq0 + hh * HEAD_DIM:bq0 + (hh + 1) * HEAD_DIM] for hh in pair_heads]
                            + [w_in_b[:, bq0 + 512:]], axis=1)
    w_out0 = jnp.concatenate([w_out_b[:512]]
                             + [w_out_b[512 + hh * HEAD_DIM:512 + (hh + 1) * HEAD_DIM] for hh in pair_heads], axis=0)
    gains0 = jnp.concatenate([tile2(l0_qnorm_g), tile2(l0_knorm_g)], axis=0)
    q_exp2 = Q_SCALE * math.log2(math.e)
    blocks0 = ([("rope", 0, q_exp2, 0, EV_AQ + k * LANES) for k in range(4)]
               + [("rope", 0, 1.0, 0, EV_AK + k * LANES) for k in range(4)]
               + [("plain", 0, 1.0, 0, EV_AV + k * 2 * LANES) for k in range(4)]
               + [("norm", 0, q_exp2, 0, EV_BQ + k * LANES) for k in range(4)]
               + [("norm", 1, 1.0, 0, EV_BK), ("plain", 0, 1.0, 0, EV_BV)])
    ones0 = [(0, EV_AV + (2 * k + 1) * LANES) for k in range(4)] + [(0, EV_BV + LANES)]
    qkv = _inproj_call(x, sc1, sh1, w_in0, blocks0, [(EV_WIDTH, BF16)], rope, gains0, gmat,
                       use_rope=True, name="inproj_l0", ones_cols=ones0)[0]
    qkv_c = _inproj_call(ctx, csc1, csh1, w_in0, blocks0, [(EV_WIDTH, BF16)], rope_id, gains0, gmat,
                         use_rope=False, name="inproj_l0c", ones_cols=ones0)[0]
    lam_init = 0.8 - 0.6 * math.exp(-0.3 * 0)
    lamv = jnp.stack([l0_lam_q1, l0_lam_k1, l0_lam_q2, l0_lam_k2], axis=0)
    sub_g = l0_subln_g.reshape(1, LANES)
    mix = _attn_even_call(qkv, [qkv, qkv_c], lamv, sub_g, lam_init, "attn_l0")
    mix_c = _attn_even_call(qkv_c, [qkv_c], lamv, sub_g, lam_init, "attn_l0c")
    (sh1n, sc1n, g1n, sh2n, sc2n, g2n), (csh1n, csc1n, _, _, _, _) = mods(1)
    cw = C_HEADS * HEAD_DIM
    w_in1 = l1_w_in.astype(BF16)
    blocks1 = ([("plain", 0, q_exp2, 0, k * LANES) for k in range(4)]
               + [("plain", 0, 1.0, 0, cw + k * LANES) for k in range(4)]
               + [("plain", 0, 1.0, 0, 2 * cw + 2 * k * LANES) for k in range(4)]
               + [("plain", 0, 1.0, 1, k * LANES) for k in range(4)])
    ones1 = [(0, 2 * cw + (2 * k + 1) * LANES) for k in range(4)]
    blocks1c = ([("plain", 0, 1.0, 0, k * LANES) for k in range(4)]
                + [("plain", 0, 1.0, 0, cw + 2 * k * LANES) for k in range(4)])
    ones1c = [(0, cw + (2 * k + 1) * LANES) for k in range(4)]
    proj_lat = (sc1n, sh1n, w_in1, blocks1, [(4 * cw, BF16), (D_GROUPS * D_GROUP_DIM, F32)], ones1)
    proj_ctx = (csc1n, csh1n, w_in1[:, cw:3 * cw], blocks1c, [(3 * cw, BF16)], ones1c)
    (x, qkv1, du), (ctx, ckv) = _ffn_sublayer(
        [(mix, w_out0, x, g1, sc2, sh2, g2), (mix_c, w_out0, ctx, cg1, csc2, csh2, cg2)],
        [proj_lat, proj_ctx], ln_g[0], ln_b[0], moe_router[0], moe_w_gate, moe_w_up, moe_w_down, 0)

    mix1 = _odd_mixer_call(qkv1, ckv, l1_rpb * math.log2(math.e), du, l1_pool_w.astype(BF16),
                           l1_pool_scale.reshape(1, -1))
    (x,) = _ffn_sublayer([(mix1, l1_w_out.astype(BF16), x, g1n, sc2n, sh2n, g2n)],
                         [None], ln_g[1], ln_b[1], moe_router[1], moe_w_gate, moe_w_up, moe_w_down, 1)
    return x
```

```python
import collections
import functools
import math

import numpy as np

import jax
import jax.numpy as jnp
from jax import lax
from jax.experimental import pallas as pl
from jax.experimental.pallas import tpu as pltpu

F32 = jnp.float32
BF16 = jnp.bfloat16

DEPTH = 2
GRID_W = 64
HEAD_DIM = 64
A_HEADS = 4
B_HEADS = 8
B_KV_HEADS = 2
C_HEADS = 8
D_GROUPS = 4
D_GROUP_DIM = 128
POOL_WINDOWS = (2, 4, 8, 16)
NA_ROWS = 8
NA_COLS = 16
N_EXPERTS = 16
EC_FACTOR = 2
ROPE_THETA = 10000.0
LN_EPS = 1e-5
RMS_EPS = 1e-6
DEEPNORM_ALPHA = (2 * DEPTH) ** 0.25
Q_SCALE = HEAD_DIM ** -0.5

LANES = 128
VMEM_LIMIT = 56 * 1024 * 1024

NA_QROWS = 4
NEG_BIG = -1e30

_NN = (((1,), (0,)), ((), ()))
_NT = (((1,), (1,)), ((), ()))


def _dot(a, b, dims=_NN):
    return lax.dot_general(a, b, dims, preferred_element_type=F32)


def _split_bf16(a):
    hi = a.astype(BF16)
    lo = (a - hi.astype(F32)).astype(BF16)
    return hi, lo


def _dot3(a, b, dims=_NN):
    a_hi, a_lo = _split_bf16(a)
    b_hi, b_lo = _split_bf16(b)
    return _dot(a_hi, b_hi, dims) + (_dot(a_hi, b_lo, dims) + _dot(a_lo, b_hi, dims))


class _Mod(collections.namedtuple("_Mod", "arr layer k row")):
    def spec(self):
        d = self.arr.shape[-1]

        def index_map(*grid):
            return (self.layer, grid[0] if self.row is None else self.row, self.k, 0, 0)

        return pl.BlockSpec((None, None, 1, 1, d), index_map)


def _params(sem):
    return pltpu.CompilerParams(dimension_semantics=sem, vmem_limit_bytes=VMEM_LIMIT)


def _layer_norm(z, g, b):
    mu = jnp.mean(z, axis=-1, keepdims=True)
    zc = z - mu
    var = jnp.mean(zc * zc, axis=-1, keepdims=True)
    return zc * lax.rsqrt(var + LN_EPS) * g + b


def _lane_masks():
    lane = lax.broadcasted_iota(jnp.int32, (1, LANES), 1)
    lo = jnp.where(lane < HEAD_DIM, 1.0, 0.0).astype(F32)
    return lo, 1.0 - lo


def _ada_kernel(c_ref, w_ref, b_ref, o_ref):
    c = c_ref[...]
    s = c * (1.0 / (1.0 + jnp.exp(-c)))
    o_ref[0] = _dot3(s, w_ref[0]) + b_ref[0]


def _ada_call(cc, ada_w, ada_b):
    depth, d, n = ada_w.shape
    rows = cc.shape[0]
    tn = 1536 if n % 1536 == 0 else n
    return pl.pallas_call(
        _ada_kernel,
        out_shape=jax.ShapeDtypeStruct((depth, rows, n), F32),
        grid=(depth, n // tn),
        in_specs=[pl.BlockSpec((rows, d), lambda l, j: (0, 0)),
                  pl.BlockSpec((1, d, tn), lambda l, j: (l, 0, j)),
                  pl.BlockSpec((1, 1, tn), lambda l, j: (l, 0, j))],
        out_specs=pl.BlockSpec((1, rows, tn), lambda l, j: (l, 0, j)),
        compiler_params=_params(("arbitrary", "arbitrary")),
        name="ada_mod",
    )(cc, ada_w, ada_b.reshape(depth, 1, n))


def _inproj_body(h, w_ref, rope_refs, gn_ref, gmat_ref, o_refs, *, blocks, ones_cols, chunk, use_rope,
                 rows=slice(None)):
    n = len(blocks) * LANES
    for oi, oc in ones_cols:
        o_refs[oi][0, rows, oc:oc + LANES] = jnp.ones((h.shape[0], LANES), o_refs[oi].dtype)
    starts = list(range(0, n, chunk))
    accs = {starts[0]: _dot(h, w_ref[:, starts[0]:min(starts[0] + chunk, n)])}
    for ci, c0 in enumerate(starts):
        cw = min(chunk, n - c0)
        if ci + 1 < len(starts):
            nxt = starts[ci + 1]
            accs[nxt] = _dot(h, w_ref[:, nxt:min(nxt + chunk, n)])
        acc = accs.pop(c0)
        for j in range(cw // LANES):
            kind, gain_row, factor, oi, oc = blocks[(c0 // LANES) + j]
            v = acc[:, j * LANES:(j + 1) * LANES]
            if kind == "norm":
                v2 = v * v
                hi, lo = _split_bf16(v2)
                ms = _dot(hi, gmat_ref[...]) + _dot(lo, gmat_ref[...])
                v = v * lax.rsqrt(ms + RMS_EPS) * gn_ref[gain_row:gain_row + 1, :]
            if kind in ("rope", "norm") and use_rope:
                cos_ref, sinp_ref, sinm_ref = rope_refs
                v = (v * cos_ref[...] + pltpu.roll(v, 16, 1) * sinp_ref[...]
                     + pltpu.roll(v, LANES - 16, 1) * sinm_ref[...])
            if factor != 1.0:
                v = v * factor
            o_refs[oi][0, rows, oc:oc + LANES] = v.astype(o_refs[oi].dtype)


def _inproj_kernel(x_ref, sc_ref, sh_ref, w_ref, cos_ref, sinp_ref, sinm_ref, gn_ref, gmat_ref, *o_refs, **cfg):
    h = (x_ref[0] * (1.0 + sc_ref[0]) + sh_ref[0]).astype(BF16)
    _inproj_body(h, w_ref, (cos_ref, sinp_ref, sinm_ref), gn_ref, gmat_ref, o_refs, **cfg)


def _inproj_call(x, sc, sh, w, blocks, out_defs, rope_tabs, gains, gmat, *, use_rope, name, ones_cols=()):
    b, s, d = x.shape
    n = w.shape[1]
    tm = min(512, s)
    cos, sinp, sinm = rope_tabs
    tab_spec = pl.BlockSpec((tm, LANES), lambda bi, i: (i, 0))
    kern = functools.partial(_inproj_kernel, blocks=tuple(blocks), ones_cols=tuple(ones_cols), chunk=512,
                             use_rope=use_rope)
    return pl.pallas_call(
        kern,
        out_shape=[jax.ShapeDtypeStruct((b, s, nc), dt) for nc, dt in out_defs],
        grid=(b, s // tm),
        in_specs=[pl.BlockSpec((1, tm, d), lambda bi, i: (bi, i, 0)),
                  sc.spec(), sh.spec(),
                  pl.BlockSpec((d, n), lambda bi, i: (0, 0)),
                  tab_spec, tab_spec, tab_spec,
                  pl.BlockSpec(gains.shape, lambda bi, i: (0, 0)),
                  pl.BlockSpec(gmat.shape, lambda bi, i: (0, 0))],
        out_specs=[pl.BlockSpec((1, tm, nc), lambda bi, i: (bi, i, 0)) for nc, _ in out_defs],
        compiler_params=_params(("arbitrary", "arbitrary")),
        name=name,
    )(x, sc.arr, sh.arr, w, cos, sinp, sinm, gains, gmat)


def _rope_tables(s):
    n_freq = HEAD_DIM // 4
    t = jnp.arange(s, dtype=jnp.int32)
    inv = ROPE_THETA ** (-jnp.arange(n_freq, dtype=F32) / n_freq)
    ang_r = (t // GRID_W).astype(F32)[:, None] * inv
    ang_c = (t % GRID_W).astype(F32)[:, None] * inv
    ang = jnp.concatenate([ang_r, ang_r, ang_c, ang_c] * (LANES // HEAD_DIM), axis=-1)
    first = (jnp.arange(LANES) % 32) < 16
    cos, sin = jnp.cos(ang), jnp.sin(ang)
    return cos, jnp.where(first, 0.0, sin), jnp.where(first, -sin, 0.0)


EV_AQ, EV_AK, EV_AV = 0, 512, 1024
EV_BQ = EV_AV + A_HEADS * 2 * LANES
EV_BK = EV_BQ + (B_HEADS // 2) * LANES
EV_BV = EV_BK + LANES
EV_WIDTH = EV_BV + 2 * LANES


def _attn_even_kernel(q_ref, lam_ref, sg_ref, *refs, n_kv, tq, lam_init):
    kv_refs, o_ref = refs[:n_kv], refs[n_kv]
    lo, hi = _lane_masks()
    lo_b, hi_b = lo.astype(BF16), hi.astype(BF16)
    lv = lam_ref[...]
    lam = (jnp.exp(jnp.sum(lv[0:1] * lv[1:2], axis=-1, keepdims=True))
           - jnp.exp(jnp.sum(lv[2:3] * lv[3:4], axis=-1, keepdims=True)) + lam_init)

    def scores(q, kcol):
        qq = jnp.concatenate([q * lo_b, q * hi_b], axis=0)
        ss = [_dot(qq, kv[0, :, kcol:kcol + LANES], _NT) for kv in kv_refs]
        m = ss[0].max(axis=-1, keepdims=True)
        for s in ss[1:]:
            m = jnp.maximum(m, s.max(axis=-1, keepdims=True))
        return ss, m

    def weighted(ss, m, vcol):
        acc = None
        for s, kv in zip(ss, kv_refs):
            part = _dot(jnp.exp2(s - m).astype(BF16), kv[0, :, vcol:vcol + 2 * LANES])
            acc = part if acc is None else acc + part
        return acc[:, :LANES] * (1.0 / acc[:, LANES:])

    units = ([("a", h, EV_AQ + h * LANES, EV_AK + h * LANES, EV_AV + h * 2 * LANES) for h in range(A_HEADS)]
             + [("b", j, EV_BQ + j * LANES, EV_BK, EV_BV) for j in range(B_HEADS // 2)])

    def finish(unit, ss, m):
        kind, idx, _, _, vcol = unit
        on = weighted(ss, m, vcol)
        if kind == "a":
            o = on[:tq] - lam * on[tq:]
            ms = jnp.mean(o * o, axis=-1, keepdims=True)
            o = o * lax.rsqrt(ms + RMS_EPS) * sg_ref[...] * (1.0 - lam_init)
            o_ref[0, :, idx * LANES:(idx + 1) * LANES] = o.astype(o_ref.dtype)
        else:
            o = on[:tq] * lo + on[tq:] * hi
            o_ref[0, :, 512 + idx * LANES:512 + (idx + 1) * LANES] = o.astype(o_ref.dtype)

    pending = None
    for unit in units:
        cur = scores(q_ref[0, :, unit[2]:unit[2] + LANES], unit[3])
        if pending is not None:
            finish(*pending)
        pending = (unit,) + cur
    finish(*pending)


def _attn_even_call(q_arr, kv_arrs, lamv, subln_g, lam_init, name):
    b, sq, n = q_arr.shape
    tq = min(256, sq)
    kern = functools.partial(_attn_even_kernel, n_kv=len(kv_arrs), tq=tq, lam_init=lam_init)
    return pl.pallas_call(
        kern,
        out_shape=jax.ShapeDtypeStruct((b, sq, 1024), BF16),
        grid=(b, sq // tq),
        in_specs=[pl.BlockSpec((1, tq, n), lambda bi, i: (bi, i, 0)),
                  pl.BlockSpec(lamv.shape, lambda bi, i: (0, 0)),
                  pl.BlockSpec(subln_g.shape, lambda bi, i: (0, 0))]
                 + [pl.BlockSpec((1,) + a.shape[1:], lambda bi, i: (bi, 0, 0)) for a in kv_arrs],
        out_specs=pl.BlockSpec((1, tq, 1024), lambda bi, i: (bi, i, 0)),
        compiler_params=_params(("arbitrary", "arbitrary")),
        name=name,
    )(q_arr, lamv, subln_g, *kv_arrs)


def _odd_mixer_kernel(qkv_ref, ckv_ref, bias_ref, du_ref, band_ref, edge_ref, pw_ref, ps_ref, o_ref, *,
                      tq, slab_rows, n_rows, n_blk, seq):
    i = pl.program_id(1)
    lo, hi = _lane_masks()
    lo_b, hi_b = lo.astype(BF16), hi.astype(BF16)
    t0 = pl.multiple_of(i * tq, tq)
    base = jnp.clip(i * NA_QROWS - NA_ROWS // 2, 0, n_rows - slab_rows)
    k0 = pl.multiple_of(base * GRID_W, GRID_W)
    nk = slab_rows * GRID_W
    cw = C_HEADS * HEAD_DIM

    def scores(j):
        cs = slice(j * LANES, (j + 1) * LANES)
        q = qkv_ref[0, pl.ds(t0, tq), cs]
        qq = jnp.concatenate([q * lo_b, q * hi_b], axis=0)
        kl = qkv_ref[0, pl.ds(k0, nk), cw + j * LANES:cw + (j + 1) * LANES]
        bias = jnp.concatenate([bias_ref[2 * j], bias_ref[2 * j + 1]], axis=0)
        s_l = _dot(qq, kl, _NT) + bias
        s_c = _dot(qq, ckv_ref[0, :, cs], _NT)
        m = jnp.maximum(s_l.max(axis=-1, keepdims=True), s_c.max(axis=-1, keepdims=True))
        return j, s_l, s_c, m

    def finish(j, s_l, s_c, m):
        vl = qkv_ref[0, pl.ds(k0, nk), 2 * cw + 2 * j * LANES:2 * cw + 2 * (j + 1) * LANES]
        vc = ckv_ref[0, :, cw + 2 * j * LANES:cw + 2 * (j + 1) * LANES]
        acc = _dot(jnp.exp2(s_l - m).astype(BF16), vl) + _dot(jnp.exp2(s_c - m).astype(BF16), vc)
        on = acc[:, :LANES] * (1.0 / acc[:, LANES:])
        o = on[:tq] * lo + on[tq:] * hi
        o_ref[0, :, j * LANES:(j + 1) * LANES] = o.astype(o_ref.dtype)

    pending = None
    for j in range(C_HEADS // 2):
        cur = scores(j)
        if pending is not None:
            finish(*pending)
        pending = cur
    finish(*pending)

    tprev = pl.multiple_of(jnp.maximum(i - 1, 0) * tq, tq)
    tnext = pl.multiple_of(jnp.minimum(i + 1, n_blk - 1) * tq, tq)
    has_prev = jnp.where(i > 0, 1.0, 0.0).astype(F32)
    has_next = jnp.where(i < n_blk - 1, 1.0, 0.0).astype(F32)
    tpos = t0 + lax.broadcasted_iota(jnp.int32, (tq, 1), 0)
    for g in range(D_GROUPS):
        half = POOL_WINDOWS[g] // 2
        gs = slice(g * D_GROUP_DIM, (g + 1) * D_GROUP_DIM)
        cur = du_ref[0, pl.ds(t0, tq), gs]
        prv = du_ref[0, pl.ds(tprev + (tq - POOL_EDGE), POOL_EDGE), gs]
        nxt = du_ref[0, pl.ds(tnext, POOL_EDGE), gs]

        def band_sum(band, u):
            u_hi, u_lo = _split_bf16(u)
            return _dot(band, u_hi) + _dot(band, u_lo)

        wsum = band_sum(band_ref[g], cur)
        wsum = jnp.concatenate([wsum[:POOL_EDGE] + has_prev * band_sum(edge_ref[g, 0], prv),
                                wsum[POOL_EDGE:tq - POOL_EDGE],
                                wsum[tq - POOL_EDGE:] + has_next * band_sum(edge_ref[g, 1], nxt)], axis=0)
        cnt = (jnp.minimum(tpos + half, seq) - jnp.maximum(tpos - half, 0)).astype(F32)
        pooled = wsum * (1.0 / cnt) - cur
        od = _dot(pooled.astype(BF16), pw_ref[g]) * ps_ref[:, gs]
        o_ref[0, :, cw + g * D_GROUP_DIM:cw + (g + 1) * D_GROUP_DIM] = od.astype(o_ref.dtype)


def _na_geometry(n_rows):
    kh = min(NA_ROWS, n_rows)
    slab = min(n_rows, NA_QROWS + kh)
    n_blk = n_rows // NA_QROWS
    bases = [min(max(i * NA_QROWS - NA_ROWS // 2, 0), n_rows - slab) for i in range(n_blk)]
    sigs, type_of = [], []
    for i in range(n_blk):
        sig = tuple((min(max(i * NA_QROWS + r - kh // 2, 0), n_rows - kh) - bases[i],
                     i * NA_QROWS + r - bases[i]) for r in range(NA_QROWS))
        if sig not in sigs:
            sigs.append(sig)
        type_of.append(sigs.index(sig))
    return kh, slab, n_blk, sigs, type_of


def _na_bias_table(rpb, n_rows):
    kh, slab, _, sigs, _ = _na_geometry(n_rows)
    kw = NA_COLS
    n_dc = 2 * NA_COLS - 1
    w = np.arange(GRID_W)
    cstart = np.clip(w - kw // 2, 0, GRID_W - kw)
    col_ok = (w[None, :] >= cstart[:, None]) & (w[None, :] < cstart[:, None] + kw)
    dc = np.clip(w[None, :] - w[:, None] + (NA_COLS - 1), 0, n_dc - 1)
    heads = rpb.shape[0]
    onehot = ((dc[None] == np.arange(n_dc)[:, None, None]) & col_ok[None]).astype(np.float32)
    toep = jnp.einsum("hrd,dwj->hwrj", rpb, jnp.asarray(onehot), precision=lax.Precision.HIGHEST)
    toep = jnp.where(jnp.asarray(col_ok)[:, None, :], toep, NEG_BIG)
    pad = slab + NA_QROWS
    toep = jnp.pad(toep, ((0, 0), (0, 0), (pad, pad), (0, 0)), constant_values=NEG_BIG)
    toep = toep.reshape(heads, GRID_W, -1)
    blocks = []
    for sig in sigs:
        for rs_rel, qr_rel in sig:
            d0 = pad - qr_rel + (NA_ROWS - 1)
            in_win = np.repeat(np.array([rs_rel <= m < rs_rel + kh for m in range(slab)]), GRID_W)
            blk = lax.slice_in_dim(toep, d0 * GRID_W, (d0 + slab) * GRID_W, axis=2)
            blocks.append(jnp.where(jnp.asarray(in_win), blk, NEG_BIG))
    big = jnp.stack(blocks, axis=1)
    return big.reshape(heads, len(sigs), NA_QROWS * GRID_W, slab * GRID_W)


POOL_EDGE = 16


def _pool_bands(tq):
    t = jnp.arange(tq, dtype=jnp.int32)[:, None]
    sidx = jnp.arange(tq, dtype=jnp.int32)[None, :]
    main, edge = [], []
    for wdw in POOL_WINDOWS:
        half = wdw // 2
        full = [((sidx + (m - 1) * tq >= t - half) & (sidx + (m - 1) * tq < t + half)).astype(BF16)
                for m in range(3)]
        main.append(full[1])
        edge.append(jnp.stack([full[0][:POOL_EDGE, tq - POOL_EDGE:], full[2][tq - POOL_EDGE:, :POOL_EDGE]]))
    return jnp.stack(main), jnp.stack(edge)


def _odd_mixer_call(qkv, ckv, rpb, du, pool_w, pool_scale):
    b, s, n = qkv.shape
    n_rows = s // GRID_W
    _, slab, n_blk, _, type_of = _na_geometry(n_rows)
    tq = NA_QROWS * GRID_W
    bias = _na_bias_table(rpb, n_rows)
    bands, edges = _pool_bands(tq)

    def bias_map(bi, i):
        t = jnp.int32(type_of[-1])
        for blk in range(n_blk - 2, -1, -1):
            t = jnp.where(i == blk, jnp.int32(type_of[blk]), t)
        return (0, t, 0, 0)

    kern = functools.partial(_odd_mixer_kernel, tq=tq, slab_rows=slab, n_rows=n_rows, n_blk=n_blk, seq=s)
    return pl.pallas_call(
        kern,
        out_shape=jax.ShapeDtypeStruct((b, s, 1024), BF16),
        grid=(b, n_blk),
        in_specs=[pl.BlockSpec((1, s, n), lambda bi, i: (bi, 0, 0)),
                  pl.BlockSpec((1,) + ckv.shape[1:], lambda bi, i: (bi, 0, 0)),
                  pl.BlockSpec((bias.shape[0], None) + bias.shape[2:], bias_map),
                  pl.BlockSpec((1, s, du.shape[2]), lambda bi, i: (bi, 0, 0)),
                  pl.BlockSpec(bands.shape, lambda bi, i: (0, 0, 0)),
                  pl.BlockSpec(edges.shape, lambda bi, i: (0, 0, 0, 0)),
                  pl.BlockSpec(pool_w.shape, lambda bi, i: (0, 0, 0)),
                  pl.BlockSpec(pool_scale.shape, lambda bi, i: (0, 0))],
        out_specs=pl.BlockSpec((1, tq, 1024), lambda bi, i: (bi, i, 0)),
        compiler_params=_params(("arbitrary", "arbitrary")),
        name="odd_mixer",
    )(qkv, ckv, bias, du, bands, edges, pool_w, pool_scale)


OUTPROJ_SUB = 256


def _outproj_kernel(mix_ref, w_ref, x_ref, g1_ref, lng_ref, lnb_ref, sc_ref, sh_ref, r2_ref, rh_ref,
                    x1_ref, hm_ref, aff_ref, *, n_exp):
    tm = x_ref.shape[1]
    subs = [slice(r0, r0 + OUTPROJ_SUB) for r0 in range(0, tm, OUTPROJ_SUB)]
    ys = [_dot(mix_ref[0, rows, :], w_ref[...]) for rows in subs]
    for rows, y in zip(subs, ys):
        x1 = _layer_norm(DEEPNORM_ALPHA * x_ref[0, rows, :] + g1_ref[0] * y, lng_ref[...], lnb_ref[...])
        x1_ref[0, rows, :] = x1
        hm = x1 * (1.0 + sc_ref[0]) + sh_ref[0]
        hm_hi = hm.astype(BF16)
        hm_ref[0, rows, :] = hm_hi
        hm_lo = (hm - hm_hi.astype(F32)).astype(BF16)
        part = _dot(hm_hi, r2_ref[...])
        logits = part[:, :n_exp] + part[:, n_exp:] + _dot(hm_lo, rh_ref[...])
        ex = jnp.exp(logits - logits.max(axis=-1, keepdims=True))
        aff_ref[0, rows, :] = ex * (1.0 / ex.sum(axis=-1, keepdims=True))


def _outproj_call(mix, w, x, g1, lng, lnb, sc2, sh2, router, name):
    b, s, d = x.shape
    dm = mix.shape[2]
    e = router.shape[1]
    tm = min(4 * OUTPROJ_SUB, s)
    r_hi = router.astype(BF16)
    r_lo = (router - r_hi.astype(F32)).astype(BF16)
    r2 = jnp.concatenate([r_hi, r_lo], axis=1)
    row = pl.BlockSpec((1, d), lambda bi, i: (0, 0))
    tile = lambda n: pl.BlockSpec((1, tm, n), lambda bi, i: (bi, i, 0))
    return pl.pallas_call(
        functools.partial(_outproj_kernel, n_exp=e),
        out_shape=[jax.ShapeDtypeStruct((b, s, d), F32),
                   jax.ShapeDtypeStruct((b, s, d), BF16),
                   jax.ShapeDtypeStruct((b, s, e), F32)],
        grid=(b, s // tm),
        in_specs=[tile(dm),
                  pl.BlockSpec((dm, d), lambda bi, i: (0, 0)),
                  tile(d),
                  g1.spec(), row, row, sc2.spec(), sh2.spec(),
                  pl.BlockSpec((d, 2 * e), lambda bi, i: (0, 0)),
                  pl.BlockSpec((d, e), lambda bi, i: (0, 0))],
        out_specs=[tile(d), tile(d), tile(e)],
        compiler_params=_params(("arbitrary", "arbitrary")),
        name=name,
    )(mix, w, x, g1.arr, lng, lnb, sc2.arr, sh2.arr, r2, r_hi)


def _lane_cumsum(m):
    rows, s = m.shape
    r_i = lax.broadcasted_iota(jnp.int32, (LANES, LANES), 0)
    c_i = lax.broadcasted_iota(jnp.int32, (LANES, LANES), 1)
    tri = jnp.where(r_i <= c_i, 1.0, 0.0).astype(BF16)
    carry = jnp.zeros((rows, 1), F32)
    out = []
    for c in range(s // LANES):
        blk = m[:, c * LANES:(c + 1) * LANES]
        out.append(_dot(blk.astype(BF16), tri) + carry)
        carry = carry + blk.sum(axis=-1, keepdims=True)
    return jnp.concatenate(out, axis=-1)


ROUTE_TILE = 256


def _route_kernel(aff_ref, pos_ref, cnt_ref, *, cap):
    a = aff_ref[...]
    thr = jnp.zeros((a.shape[0], 1), jnp.int32)
    for bit in range(30, -1, -1):
        cand = thr | jnp.int32(1 << bit)
        cnt = jnp.where(a >= pltpu.bitcast(cand, F32), 1.0, 0.0).sum(axis=-1, keepdims=True)
        thr = jnp.where(cnt >= cap, cand, thr)
    gt = jnp.where(a >= pltpu.bitcast(thr + 1, F32), 1.0, 0.0)
    eq = jnp.where(a >= pltpu.bitcast(thr, F32), 1.0, 0.0) - gt
    need = cap - gt.sum(axis=-1, keepdims=True)
    sel = gt + eq * jnp.where(_lane_cumsum(eq) <= need, 1.0, 0.0)
    pos_ref[...] = jnp.where(sel > 0.5, _lane_cumsum(sel) - 1.0, -1.0)
    tok = lax.broadcasted_iota(jnp.int32, (a.shape[1], LANES), 0)
    tile = lax.broadcasted_iota(jnp.int32, (a.shape[1], LANES), 1)
    before = jnp.where(tok < tile * ROUTE_TILE, 1.0, 0.0).astype(BF16)
    cnt_ref[...] = _dot(sel.astype(BF16), before)


def _route_call(aff, cap, name):
    b, e, s = aff.shape
    n_tiles = s // min(ROUTE_TILE, s)
    full = pl.BlockSpec((b * e, s), lambda i: (0, 0))
    pos, cnt = pl.pallas_call(
        functools.partial(_route_kernel, cap=cap),
        out_shape=[jax.ShapeDtypeStruct((b * e, s), F32), jax.ShapeDtypeStruct((b * e, LANES), F32)],
        grid=(1,),
        in_specs=[full],
        out_specs=[full, pl.BlockSpec((b * e, LANES), lambda i: (0, 0))],
        compiler_params=_params(("arbitrary",)),
        name=name,
    )(aff.reshape(b * e, s))
    return pos.reshape(b, e, s), cnt[:, :n_tiles + 1].astype(jnp.int32).reshape(-1)


def _gather_kernel(cnt_ref, pos_ref, aff_ref, hm_ref, xg_ref, gs_ref, *, cap, n_exp, win, n_tiles):
    bi, i = pl.program_id(0), pl.program_id(1)

    @pl.when(i == 0)
    def _():
        xg_ref[...] = jnp.zeros(xg_ref.shape, xg_ref.dtype)
        gs_ref[...] = jnp.zeros(gs_ref.shape, gs_ref.dtype)

    starts, short = [], None
    for ei in range(n_exp):
        base = (bi * n_exp + ei) * (n_tiles + 1) + i
        st = jnp.minimum((cnt_ref[base] // SLOT_ALIGN) * SLOT_ALIGN, cap - win)
        starts.append(pl.multiple_of(st, SLOT_ALIGN))
        miss = cnt_ref[base + 1] > st + win
        short = miss if short is None else jnp.logical_or(short, miss)

    def scatter_rows(n_rows, first, stacked):
        row = lax.broadcasted_iota(jnp.int32, (n_rows, 1), 0).astype(F32)
        blocks = []
        for ei in range(n_exp):
            rel = pos_ref[0, ei:ei + 1, :] if first is None else pos_ref[0, ei:ei + 1, :] - first[ei].astype(F32)
            hit = rel == row
            blocks.append(jnp.where(hit, 1.0, 0.0).astype(BF16))
            gate = jnp.where(hit, aff_ref[0, ei:ei + 1, :], 0.0).sum(axis=-1, keepdims=True)
            rows = slice(None) if first is None else pl.ds(first[ei], n_rows)
            gs_ref[0, ei, rows, :] += gate
        if stacked:
            part = _dot(jnp.concatenate(blocks, axis=0), hm_ref[0])
        for ei in range(n_exp):
            rows = slice(None) if first is None else pl.ds(first[ei], n_rows)
            sub = part[ei * n_rows:(ei + 1) * n_rows] if stacked else _dot(blocks[ei], hm_ref[0])
            xg_ref[0, ei, rows, :] += sub.astype(xg_ref.dtype)

    @pl.when(jnp.logical_not(short))
    def _():
        scatter_rows(win, starts, True)

    @pl.when(short)
    def _():
        scatter_rows(cap, None, False)


def _gather_call(cnt, pos, aff, hm, cap, name):
    b, e, s = pos.shape
    d = hm.shape[2]
    tm = min(ROUTE_TILE, s)
    n_tiles = s // tm
    win = min(64, cap)
    tok = pl.BlockSpec((1, e, tm), lambda bi, i, c: (bi, 0, i))
    return pl.pallas_call(
        functools.partial(_gather_kernel, cap=cap, n_exp=e, win=win, n_tiles=n_tiles),
        out_shape=[jax.ShapeDtypeStruct((b, e, cap, d), BF16),
                   jax.ShapeDtypeStruct((b, e, cap, 1), F32)],
        grid_spec=pltpu.PrefetchScalarGridSpec(
            num_scalar_prefetch=1,
            grid=(b, n_tiles),
            in_specs=[tok, tok, pl.BlockSpec((1, tm, d), lambda bi, i, c: (bi, i, 0))],
            out_specs=[pl.BlockSpec((1, e, cap, d), lambda bi, i, c: (bi, 0, 0, 0)),
                       pl.BlockSpec((1, e, cap, 1), lambda bi, i, c: (bi, 0, 0, 0))]),
        compiler_params=_params(("arbitrary", "arbitrary")),
        name=name,
    )(cnt, pos, aff, hm)


def _ffn_kernel(*refs, n_grp, row_chunks, n_steps):
    xg_refs = refs[:n_grp]
    gs_refs = refs[n_grp:2 * n_grp]
    wg_ref, wu_ref, wd_ref = refs[2 * n_grp:2 * n_grp + 3]
    y_refs = refs[2 * n_grp + 3:3 * n_grp + 3]
    acc_refs = refs[3 * n_grp + 3:]
    f = pl.program_id(1)
    last = pl.num_programs(1) - 1
    wg = wg_ref[0].astype(BF16)

    def body(first, final):
        wu = wd = None
        for xg_ref, gs_ref, y_ref, acc_ref, nb in zip(xg_refs, gs_refs, y_refs, acc_refs, row_chunks):
            bt, _, cap, d = xg_ref.shape
            for b0 in range(0, bt, nb):
                rows = nb * cap
                r0 = b0 * cap
                x = xg_ref[b0:b0 + nb, 0].reshape(rows, d)
                hg = _dot(x, wg)
                if wu is None:
                    wu = wu_ref[0].astype(BF16)
                hu = _dot(x, wu)
                if wd is None:
                    wd = wd_ref[0].astype(BF16)
                hid = (hg * (1.0 / (1.0 + jnp.exp(-hg))) * hu).astype(BF16)
                part = _dot(hid, wd)
                if not first:
                    part = acc_ref[r0:r0 + rows, :] + part
                if final:
                    gate = gs_ref[b0:b0 + nb, 0].reshape(rows, 1)
                    y_ref[b0:b0 + nb, 0] = (part * gate).reshape(nb, cap, d).astype(y_ref.dtype)
                else:
                    acc_ref[r0:r0 + rows, :] = part

    if n_steps == 1:
        body(True, True)
    else:
        pl.when(f == 0)(functools.partial(body, True, False))
        pl.when(jnp.logical_and(f > 0, f < last))(functools.partial(body, False, False))
        pl.when(f == last)(functools.partial(body, False, True))


def _ffn_call(xgs, gss, w_gate, w_up, w_down, layer, name):
    _, e, d, ff = w_gate.shape
    tf = 512 if ff % 512 == 0 else ff
    n_grp = len(xgs)
    row_chunks = []
    for xg in xgs:
        bt, _, cap, _ = xg.shape
        nb = max(1, min(bt, 512 // cap))
        while bt % nb:
            nb -= 1
        row_chunks.append(nb)
    tok = lambda a: pl.BlockSpec((a.shape[0], 1) + a.shape[2:], lambda ei, fi: (0, ei, 0, 0))
    kern = functools.partial(_ffn_kernel, n_grp=n_grp, row_chunks=tuple(row_chunks), n_steps=ff // tf)
    return pl.pallas_call(
        kern,
        out_shape=[jax.ShapeDtypeStruct(xg.shape, BF16) for xg in xgs],
        grid=(e, ff // tf),
        in_specs=[tok(a) for a in xgs] + [tok(a) for a in gss]
                 + [pl.BlockSpec((None, 1, d, tf), lambda ei, fi: (layer, ei, 0, fi)),
                    pl.BlockSpec((None, 1, d, tf), lambda ei, fi: (layer, ei, 0, fi)),
                    pl.BlockSpec((None, 1, tf, d), lambda ei, fi: (layer, ei, fi, 0))],
        out_specs=[tok(a) for a in xgs],
        scratch_shapes=[pltpu.VMEM((xg.shape[0] * xg.shape[2], d), F32) for xg in xgs],
        compiler_params=_params(("arbitrary", "arbitrary")),
        name=name,
    )(*xgs, *gss, w_gate, w_up, w_down)


MXU_DEPTH = 256
SLOT_ALIGN = 16


def _combine_kernel(cnt_ref, pos_ref, y_ref, x_ref, g_ref, lng_ref, lnb_ref, *rest, cap, n_exp, win, n_tiles,
                    n_sub, proj_cfg=None):
    if proj_cfg is None:
        (o_ref,), proj_refs = rest, ()
    else:
        proj_refs, o_ref = rest[:3] + rest[4:], rest[3]
    bi, i = pl.program_id(0), pl.program_id(1)
    sub = pos_ref.shape[1] // n_sub
    grp = MXU_DEPTH // win
    starts, short = [], None
    for r in range(n_sub):
        starts.append([])
        for ei in range(n_exp):
            base = (bi * n_exp + ei) * (n_tiles + 1) + i * n_sub + r
            st = jnp.minimum((cnt_ref[base] // SLOT_ALIGN) * SLOT_ALIGN, cap - win)
            starts[r].append(st)
            miss = cnt_ref[base + 1] > st + win
            short = miss if short is None else jnp.logical_or(short, miss)

    def windowed(r):
        pos = pos_ref[0, r * sub:(r + 1) * sub, :]
        lane = lax.broadcasted_iota(jnp.int32, (1, grp * win), 1)
        lane_f = lane.astype(F32)
        acc = None
        for k in range(n_exp // grp):
            tgt, rows = None, []
            for u in range(grp - 1, -1, -1):
                ei = k * grp + u
                st = starts[r][ei]
                rel = pos[:, ei:ei + 1] - st.astype(F32)
                rel = jnp.where(rel >= 0.0, jnp.where(rel < win, rel + float(u * win), -1.0), -1.0)
                tgt = rel if tgt is None else jnp.where(lane < (u + 1) * win, rel, tgt)
                rows.insert(0, y_ref[0, pl.ds(pl.multiple_of(ei * cap + st, SLOT_ALIGN), win), :])
            onehot = jnp.where(tgt == lane_f, 1.0, 0.0).astype(BF16)
            part = _dot(onehot, jnp.concatenate(rows, axis=0))
            acc = part if acc is None else acc + part
        return acc

    def dense(r):
        pos = pos_ref[0, r * sub:(r + 1) * sub, :]
        slot = lax.broadcasted_iota(jnp.int32, (1, cap), 1).astype(F32)
        acc = None
        for ei in range(n_exp):
            onehot = jnp.where(pos[:, ei:ei + 1] == slot, 1.0, 0.0).astype(BF16)
            part = _dot(onehot, y_ref[0, ei * cap:(ei + 1) * cap, :])
            acc = part if acc is None else acc + part
        return acc

    def epilogue(r, acc):
        rows = slice(r * sub, (r + 1) * sub)
        x_new = _layer_norm(DEEPNORM_ALPHA * x_ref[0, rows, :] + g_ref[0] * acc, lng_ref[...], lnb_ref[...])
        o_ref[0, rows, :] = x_new
        if proj_cfg is not None:
            sc_ref, sh_ref, w_ref = proj_refs[:3]
            h = (x_new * (1.0 + sc_ref[0]) + sh_ref[0]).astype(BF16)
            _inproj_body(h, w_ref, None, None, None, proj_refs[3:], rows=rows, **proj_cfg)

    @pl.when(jnp.logical_not(short))
    def _():
        accs = [windowed(r) for r in range(n_sub)]
        for r in range(n_sub):
            epilogue(r, accs[r])

    @pl.when(short)
    def _():
        for r in range(n_sub):
            epilogue(r, dense(r))


def _combine_call(cnt, pos_t, y, x, g2, lng, lnb, cap, name, proj=None):
    b, s, d = x.shape
    e = pos_t.shape[2]
    n_tiles = s // min(ROUTE_TILE, s)
    n_sub = 2 if n_tiles % 2 == 0 else 1
    tm = s // (n_tiles // n_sub)
    win = min(64, cap)
    row = pl.BlockSpec((1, d), lambda bi, i, c: (0, 0))
    tile = lambda n: pl.BlockSpec((1, tm, n), lambda bi, i, c: (bi, i, 0))
    in_specs = [tile(e), pl.BlockSpec((1, e * cap, d), lambda bi, i, c: (bi, 0, 0)), tile(d), g2.spec(), row, row]
    operands = [cnt, pos_t, y.reshape(b, e * cap, d), x, g2.arr, lng, lnb]
    out_shape = [jax.ShapeDtypeStruct((b, s, d), F32)]
    out_specs = [tile(d)]
    proj_cfg = None
    if proj is not None:
        sc, sh, w, blocks, out_defs, ones_cols = proj
        in_specs += [sc.spec(), sh.spec(), pl.BlockSpec(w.shape, lambda bi, i, c: (0, 0))]
        operands += [sc.arr, sh.arr, w]
        out_shape += [jax.ShapeDtypeStruct((b, s, nc), dt) for nc, dt in out_defs]
        out_specs += [tile(nc) for nc, _ in out_defs]
        proj_cfg = dict(blocks=tuple(blocks), ones_cols=tuple(ones_cols), chunk=512, use_rope=False)
    res = pl.pallas_call(
        functools.partial(_combine_kernel, cap=cap, n_exp=e, win=win, n_tiles=n_tiles, n_sub=n_sub,
                          proj_cfg=proj_cfg),
        out_shape=out_shape,
        grid_spec=pltpu.PrefetchScalarGridSpec(
            num_scalar_prefetch=1, grid=(b, n_tiles // n_sub), in_specs=in_specs, out_specs=out_specs),
        compiler_params=_params(("arbitrary", "arbitrary")),
        name=name,
    )(*operands)
    return res[0] if proj is None else res


def _ffn_sublayer(streams, projs, lng, lnb, router, w_gate, w_up, w_down, layer):
    staged = []
    for si, (mix, w_out, x, g1, sc2, sh2, g2) in enumerate(streams):
        tag = f"l{layer}s{si}"
        n_tok = x.shape[1]
        cap = EC_FACTOR * n_tok // N_EXPERTS
        x1, hm, aff_t = _outproj_call(mix, w_out, x, g1, lng[0:1], lnb[0:1], sc2, sh2, router, "outproj_" + tag)
        aff = jnp.swapaxes(aff_t, 1, 2)
        pos, cnt = _route_call(aff, cap, "route_" + tag)
        xg, gs = _gather_call(cnt, pos, aff, hm, cap, "gather_" + tag)
        staged.append((x1, pos, cnt, xg, gs, g2, cap, tag))
    ys = _ffn_call([st[3] for st in staged], [st[4] for st in staged], w_gate, w_up, w_down, layer,
                   f"ffn_l{layer}")
    outs = []
    for (x1, pos, cnt, _, _, g2, cap, tag), y, proj in zip(staged, ys, projs):
        outs.append(_combine_call(cnt, jnp.swapaxes(pos, 1, 2), y, x1, g2, lng[1:2], lnb[1:2], cap,
                                  "combine_" + tag, proj))
    return outs


def kernel(x, c, ctx, c_ctx, ada_w, ada_b, ln_g, ln_b, l0_w_in, l0_w_out, l0_lam_q1, l0_lam_k1, l0_lam_q2,
           l0_lam_k2, l0_subln_g, l0_qnorm_g, l0_knorm_g, l1_w_in, l1_w_out, l1_rpb, l1_pool_w, l1_pool_scale,
           moe_router, moe_w_gate, moe_w_up, moe_w_down):
    b, s, d = x.shape
    n_ctx = ctx.shape[1]

    rows = -(-(b + 1) // 8) * 8
    cc = jnp.zeros((rows, d), F32).at[:b].set(c).at[b].set(c_ctx)
    mod = _ada_call(cc, ada_w, ada_b)

    mod5 = mod.reshape(DEPTH, rows, 6, 1, d)

    def mods(i):
        return ([_Mod(mod5, i, k, None) for k in range(6)], [_Mod(mod5, i, k, b) for k in range(6)])

    rope = _rope_tables(s)
    rope_id = (jnp.ones((n_ctx, LANES), F32), jnp.zeros((n_ctx, LANES), F32), jnp.zeros((n_ctx, LANES), F32))
    gmat = jnp.where((jnp.arange(LANES)[:, None] // HEAD_DIM) == (jnp.arange(LANES)[None, :] // HEAD_DIM),
                     1.0 / HEAD_DIM, 0.0).astype(BF16)
    tile2 = lambda g: jnp.concatenate([g, g]).reshape(1, LANES)

    (sh1, sc1, g1, sh2, sc2, g2), (csh1, csc1, cg1, csh2, csc2, cg2) = mods(0)
    bq0 = 3 * 512
    pair_heads = [hh for j in range(B_HEADS // 2) for hh in (j, j + B_HEADS // 2)]
    w_in_b, w_out_b = l0_w_in.astype(BF16), l0_w_out.astype(BF16)
    w_in0 = jnp.concatenate([w_in_b[:, :bq0]]
                            + [w_in_b[:, bq0 + hh * HEAD_DIM:bq0 + (hh + 1) * HEAD_DIM] for hh in pair_heads]
                            + [w_in_b[:, bq0 + 512:]], axis=1)
    w_out0 = jnp.concatenate([w_out_b[:512]]
                             + [w_out_b[512 + hh * HEAD_DIM:512 + (hh + 1) * HEAD_DIM] for hh in pair_heads], axis=0)
    gains0 = jnp.concatenate([tile2(l0_qnorm_g), tile2(l0_knorm_g)], axis=0)
    q_exp2 = Q_SCALE * math.log2(math.e)
    blocks0 = ([("rope", 0, q_exp2, 0, EV_AQ + k * LANES) for k in range(4)]
               + [("rope", 0, 1.0, 0, EV_AK + k * LANES) for k in range(4)]
               + [("plain", 0, 1.0, 0, EV_AV + k * 2 * LANES) for k in range(4)]
               + [("norm", 0, q_exp2, 0, EV_BQ + k * LANES) for k in range(4)]
               + [("norm", 1, 1.0, 0, EV_BK), ("plain", 0, 1.0, 0, EV_BV)])
    ones0 = [(0, EV_AV + (2 * k + 1) * LANES) for k in range(4)] + [(0, EV_BV + LANES)]
    qkv = _inproj_call(x, sc1, sh1, w_in0, blocks0, [(EV_WIDTH, BF16)], rope, gains0, gmat,
                       use_rope=True, name="inproj_l0", ones_cols=ones0)[0]
    qkv_c = _inproj_call(ctx, csc1, csh1, w_in0, blocks0, [(EV_WIDTH, BF16)], rope_id, gains0, gmat,
                         use_rope=False, name="inproj_l0c", ones_cols=ones0)[0]
    lam_init = 0.8 - 0.6 * math.exp(-0.3 * 0)
    lamv = jnp.stack([l0_lam_q1, l0_lam_k1, l0_lam_q2, l0_lam_k2], axis=0)
    sub_g = l0_subln_g.reshape(1, LANES)
    mix = _attn_even_call(qkv, [qkv, qkv_c], lamv, sub_g, lam_init, "attn_l0")
    mix_c = _attn_even_call(qkv_c, [qkv_c], lamv, sub_g, lam_init, "attn_l0c")
    (sh1n, sc1n, g1n, sh2n, sc2n, g2n), (csh1n, csc1n, _, _, _, _) = mods(1)
    cw = C_HEADS * HEAD_DIM
    w_in1 = l1_w_in.astype(BF16)
    blocks1 = ([("plain", 0, q_exp2, 0, k * LANES) for k in range(4)]
               + [("plain", 0, 1.0, 0, cw + k * LANES) for k in range(4)]
               + [("plain", 0, 1.0, 0, 2 * cw + 2 * k * LANES) for k in range(4)]
               + [("plain", 0, 1.0, 1, k * LANES) for k in range(4)])
    ones1 = [(0, 2 * cw + (2 * k + 1) * LANES) for k in range(4)]
    blocks1c = ([("plain", 0, 1.0, 0, k * LANES) for k in range(4)]
                + [("plain", 0, 1.0, 0, cw + 2 * k * LANES) for k in range(4)])
    ones1c = [(0, cw + (2 * k + 1) * LANES) for k in range(4)]
    proj_lat = (sc1n, sh1n, w_in1, blocks1, [(4 * cw, BF16), (D_GROUPS * D_GROUP_DIM, F32)], ones1)
    proj_ctx = (csc1n, csh1n, w_in1[:, cw:3 * cw], blocks1c, [(3 * cw, BF16)], ones1c)
    (x, qkv1, du), (ctx, ckv) = _ffn_sublayer(
        [(mix, w_out0, x, g1, sc2, sh2, g2), (mix_c, w_out0, ctx, cg1, csc2, csh2, cg2)],
        [proj_lat, proj_ctx], ln_g[0], ln_b[0], moe_router[0], moe_w_gate, moe_w_up, moe_w_down, 0)

    mix1 = _odd_mixer_call(qkv1, ckv, l1_rpb * math.log2(math.e), du, l1_pool_w.astype(BF16),
                           l1_pool_scale.reshape(1, -1))
    (x,) = _ffn_sublayer([(mix1, l1_w_out.astype(BF16), x, g1n, sc2n, sh2n, g2n)],
                         [None], ln_g[1], ln_b[1], moe_router[1], moe_w_gate, moe_w_up, moe_w_down, 1)
    return x
```

```python
import collections
import functools
import math

import numpy as np

import jax
import jax.numpy as jnp
from jax import lax
from jax.experimental import pallas as pl
from jax.experimental.pallas import tpu as pltpu

F32 = jnp.float32
BF16 = jnp.bfloat16

DEPTH = 2
GRID_W = 64
HEAD_DIM = 64
A_HEADS = 4
B_HEADS = 8
B_KV_HEADS = 2
C_HEADS = 8
D_GROUPS = 4
D_GROUP_DIM = 128
POOL_WINDOWS = (2, 4, 8, 16)
NA_ROWS = 8
NA_COLS = 16
N_EXPERTS = 16
EC_FACTOR = 2
ROPE_THETA = 10000.0
LN_EPS = 1e-5
RMS_EPS = 1e-6
DEEPNORM_ALPHA = (2 * DEPTH) ** 0.25
Q_SCALE = HEAD_DIM ** -0.5

LANES = 128
VMEM_LIMIT = 56 * 1024 * 1024

NA_QROWS = 4
NEG_BIG = -1e30

_NN = (((1,), (0,)), ((), ()))
_NT = (((1,), (1,)), ((), ()))


def _dot(a, b, dims=_NN):
    return lax.dot_general(a, b, dims, preferred_element_type=F32)


def _split_bf16(a):
    hi = a.astype(BF16)
    lo = (a - hi.astype(F32)).astype(BF16)
    return hi, lo


def _dot3(a, b, dims=_NN):
    a_hi, a_lo = _split_bf16(a)
    b_hi, b_lo = _split_bf16(b)
    return _dot(a_hi, b_hi, dims) + (_dot(a_hi, b_lo, dims) + _dot(a_lo, b_hi, dims))


class _Mod(collections.namedtuple("_Mod", "arr layer k row")):
    def spec(self):
        d = self.arr.shape[-1]

        def index_map(*grid):
            return (self.layer, grid[0] if self.row is None else self.row, self.k, 0, 0)

        return pl.BlockSpec((None, None, 1, 1, d), index_map)


def _params(sem):
    return pltpu.CompilerParams(dimension_semantics=sem, vmem_limit_bytes=VMEM_LIMIT)


def _layer_norm(z, g, b):
    mu = jnp.mean(z, axis=-1, keepdims=True)
    zc = z - mu
    var = jnp.mean(zc * zc, axis=-1, keepdims=True)
    return zc * lax.rsqrt(var + LN_EPS) * g + b


def _lane_masks():
    lane = lax.broadcasted_iota(jnp.int32, (1, LANES), 1)
    lo = jnp.where(lane < HEAD_DIM, 1.0, 0.0).astype(F32)
    return lo, 1.0 - lo


def _ada_kernel(c_ref, w_ref, b_ref, o_ref):
    c = c_ref[...]
    s = c * (1.0 / (1.0 + jnp.exp(-c)))
    o_ref[0] = _dot3(s, w_ref[0]) + b_ref[0]


def _ada_call(cc, ada_w, ada_b):
    depth, d, n = ada_w.shape
    rows = cc.shape[0]
    tn = 1536 if n % 1536 == 0 else n
    return pl.pallas_call(
        _ada_kernel,
        out_shape=jax.ShapeDtypeStruct((depth, rows, n), F32),
        grid=(depth, n // tn),
        in_specs=[pl.BlockSpec((rows, d), lambda l, j: (0, 0)),
                  pl.BlockSpec((1, d, tn), lambda l, j: (l, 0, j)),
                  pl.BlockSpec((1, 1, tn), lambda l, j: (l, 0, j))],
        out_specs=pl.BlockSpec((1, rows, tn), lambda l, j: (l, 0, j)),
        compiler_params=_params(("arbitrary", "arbitrary")),
        name="ada_mod",
    )(cc, ada_w, ada_b.reshape(depth, 1, n))


def _inproj_body(h, w_ref, rope_refs, gn_ref, gmat_ref, o_refs, *, blocks, ones_cols, chunk, use_rope,
                 rows=slice(None)):
    n = len(blocks) * LANES
    for oi, oc in ones_cols:
        o_refs[oi][0, rows, oc:oc + LANES] = jnp.ones((h.shape[0], LANES), o_refs[oi].dtype)
    starts = list(range(0, n, chunk))
    accs = {starts[0]: _dot(h, w_ref[:, starts[0]:min(starts[0] + chunk, n)])}
    for ci, c0 in enumerate(starts):
        cw = min(chunk, n - c0)
        if ci + 1 < len(starts):
            nxt = starts[ci + 1]
            accs[nxt] = _dot(h, w_ref[:, nxt:min(nxt + chunk, n)])
        acc = accs.pop(c0)
        for j in range(cw // LANES):
            kind, gain_row, factor, oi, oc = blocks[(c0 // LANES) + j]
            v = acc[:, j * LANES:(j + 1) * LANES]
            if kind == "norm":
                v2 = v * v
                hi, lo = _split_bf16(v2)
                ms = _dot(hi, gmat_ref[...]) + _dot(lo, gmat_ref[...])
                v = v * lax.rsqrt(ms + RMS_EPS) * gn_ref[gain_row:gain_row + 1, :]
            if kind in ("rope", "norm") and use_rope:
                cos_ref, sinp_ref, sinm_ref = rope_refs
                v = (v * cos_ref[...] + pltpu.roll(v, 16, 1) * sinp_ref[...]
                     + pltpu.roll(v, LANES - 16, 1) * sinm_ref[...])
            if factor != 1.0:
                v = v * factor
            o_refs[oi][0, rows, oc:oc + LANES] = v.astype(o_refs[oi].dtype)


def _inproj_kernel(x_ref, sc_ref, sh_ref, w_ref, cos_ref, sinp_ref, sinm_ref, gn_ref, gmat_ref, *o_refs, **cfg):
    h = (x_ref[0] * (1.0 + sc_ref[0]) + sh_ref[0]).astype(BF16)
    _inproj_body(h, w_ref, (cos_ref, sinp_ref, sinm_ref), gn_ref, gmat_ref, o_refs, **cfg)


def _inproj_call(x, sc, sh, w, blocks, out_defs, rope_tabs, gains, gmat, *, use_rope, name, ones_cols=()):
    b, s, d = x.shape
    n = w.shape[1]
    tm = min(512, s)
    cos, sinp, sinm = rope_tabs
    tab_spec = pl.BlockSpec((tm, LANES), lambda bi, i: (i, 0))
    kern = functools.partial(_inproj_kernel, blocks=tuple(blocks), ones_cols=tuple(ones_cols), chunk=512,
                             use_rope=use_rope)
    return pl.pallas_call(
        kern,
        out_shape=[jax.ShapeDtypeStruct((b, s, nc), dt) for nc, dt in out_defs],
        grid=(b, s // tm),
        in_specs=[pl.BlockSpec((1, tm, d), lambda bi, i: (bi, i, 0)),
                  sc.spec(), sh.spec(),
                  pl.BlockSpec((d, n), lambda bi, i: (0, 0)),
                  tab_spec, tab_spec, tab_spec,
                  pl.BlockSpec(gains.shape, lambda bi, i: (0, 0)),
                  pl.BlockSpec(gmat.shape, lambda bi, i: (0, 0))],
        out_specs=[pl.BlockSpec((1, tm, nc), lambda bi, i: (bi, i, 0)) for nc, _ in out_defs],
        compiler_params=_params(("arbitrary", "arbitrary")),
        name=name,
    )(x, sc.arr, sh.arr, w, cos, sinp, sinm, gains, gmat)


def _rope_tables(s):
    n_freq = HEAD_DIM // 4
    t = np.arange(s)
    inv = ROPE_THETA ** (-np.arange(n_freq, dtype=np.float64) / n_freq)
    ang_r = (t // GRID_W)[:, None] * inv
    ang_c = (t % GRID_W)[:, None] * inv
    ang = np.concatenate([ang_r, ang_r, ang_c, ang_c] * (LANES // HEAD_DIM), axis=-1)
    first = (np.arange(LANES) % 32) < 16
    cos, sin = np.cos(ang), np.sin(ang)
    tabs = (cos, np.where(first, 0.0, sin), np.where(first, -sin, 0.0))
    return tuple(jnp.asarray(a.astype(np.float32)) for a in tabs)


EV_AQ, EV_AK, EV_AV = 0, 512, 1024
EV_BQ = EV_AV + A_HEADS * 2 * LANES
EV_BK = EV_BQ + (B_HEADS // 2) * LANES
EV_BV = EV_BK + LANES
EV_WIDTH = EV_BV + 2 * LANES


def _attn_even_kernel(q_ref, lam_ref, sg_ref, *refs, n_kv, tq, lam_init):
    kv_refs, o_ref = refs[:n_kv], refs[n_kv]
    lo, hi = _lane_masks()
    lo_b, hi_b = lo.astype(BF16), hi.astype(BF16)
    lv = lam_ref[...]
    lam = (jnp.exp(jnp.sum(lv[0:1] * lv[1:2], axis=-1, keepdims=True))
           - jnp.exp(jnp.sum(lv[2:3] * lv[3:4], axis=-1, keepdims=True)) + lam_init)

    def scores(q, kcol):
        qq = jnp.concatenate([q * lo_b, q * hi_b], axis=0)
        ss = [_dot(qq, kv[0, :, kcol:kcol + LANES], _NT) for kv in kv_refs]
        m = ss[0].max(axis=-1, keepdims=True)
        for s in ss[1:]:
            m = jnp.maximum(m, s.max(axis=-1, keepdims=True))
        return ss, m

    def weighted(ss, m, vcol):
        acc = None
        for s, kv in zip(ss, kv_refs):
            part = _dot(jnp.exp2(s - m).astype(BF16), kv[0, :, vcol:vcol + 2 * LANES])
            acc = part if acc is None else acc + part
        return acc[:, :LANES] * (1.0 / acc[:, LANES:])

    units = ([("a", h, EV_AQ + h * LANES, EV_AK + h * LANES, EV_AV + h * 2 * LANES) for h in range(A_HEADS)]
             + [("b", j, EV_BQ + j * LANES, EV_BK, EV_BV) for j in range(B_HEADS // 2)])

    def finish(unit, ss, m):
        kind, idx, _, _, vcol = unit
        on = weighted(ss, m, vcol)
        if kind == "a":
            o = on[:tq] - lam * on[tq:]
            ms = jnp.mean(o * o, axis=-1, keepdims=True)
            o = o * lax.rsqrt(ms + RMS_EPS) * sg_ref[...] * (1.0 - lam_init)
            o_ref[0, :, idx * LANES:(idx + 1) * LANES] = o.astype(o_ref.dtype)
        else:
            o = on[:tq] * lo + on[tq:] * hi
            o_ref[0, :, 512 + idx * LANES:512 + (idx + 1) * LANES] = o.astype(o_ref.dtype)

    pending = None
    for unit in units:
        cur = scores(q_ref[0, :, unit[2]:unit[2] + LANES], unit[3])
        if pending is not None:
            finish(*pending)
        pending = (unit,) + cur
    finish(*pending)


def _attn_even_call(q_arr, kv_arrs, lamv, subln_g, lam_init, name):
    b, sq, n = q_arr.shape
    tq = min(256, sq)
    kern = functools.partial(_attn_even_kernel, n_kv=len(kv_arrs), tq=tq, lam_init=lam_init)
    return pl.pallas_call(
        kern,
        out_shape=jax.ShapeDtypeStruct((b, sq, 1024), BF16),
        grid=(b, sq // tq),
        in_specs=[pl.BlockSpec((1, tq, n), lambda bi, i: (bi, i, 0)),
                  pl.BlockSpec(lamv.shape, lambda bi, i: (0, 0)),
                  pl.BlockSpec(subln_g.shape, lambda bi, i: (0, 0))]
                 + [pl.BlockSpec((1,) + a.shape[1:], lambda bi, i: (bi, 0, 0)) for a in kv_arrs],
        out_specs=pl.BlockSpec((1, tq, 1024), lambda bi, i: (bi, i, 0)),
        compiler_params=_params(("arbitrary", "arbitrary")),
        name=name,
    )(q_arr, lamv, subln_g, *kv_arrs)


def _odd_mixer_kernel(qkv_ref, ckv_ref, bias_ref, du_ref, band_ref, edge_ref, pw_ref, ps_ref, o_ref, *,
                      tq, slab_rows, n_rows, n_blk, seq):
    i = pl.program_id(1)
    lo, hi = _lane_masks()
    lo_b, hi_b = lo.astype(BF16), hi.astype(BF16)
    t0 = pl.multiple_of(i * tq, tq)
    base = jnp.clip(i * NA_QROWS - NA_ROWS // 2, 0, n_rows - slab_rows)
    k0 = pl.multiple_of(base * GRID_W, GRID_W)
    nk = slab_rows * GRID_W
    cw = C_HEADS * HEAD_DIM

    def scores(j):
        cs = slice(j * LANES, (j + 1) * LANES)
        q = qkv_ref[0, pl.ds(t0, tq), cs]
        qq = jnp.concatenate([q * lo_b, q * hi_b], axis=0)
        kl = qkv_ref[0, pl.ds(k0, nk), cw + j * LANES:cw + (j + 1) * LANES]
        bias = jnp.concatenate([bias_ref[2 * j], bias_ref[2 * j + 1]], axis=0)
        s_l = _dot(qq, kl, _NT) + bias
        s_c = _dot(qq, ckv_ref[0, :, cs], _NT)
        m = jnp.maximum(s_l.max(axis=-1, keepdims=True), s_c.max(axis=-1, keepdims=True))
        return j, s_l, s_c, m

    def finish(j, s_l, s_c, m):
        vl = qkv_ref[0, pl.ds(k0, nk), 2 * cw + 2 * j * LANES:2 * cw + 2 * (j + 1) * LANES]
        vc = ckv_ref[0, :, cw + 2 * j * LANES:cw + 2 * (j + 1) * LANES]
        acc = _dot(jnp.exp2(s_l - m).astype(BF16), vl) + _dot(jnp.exp2(s_c - m).astype(BF16), vc)
        on = acc[:, :LANES] * (1.0 / acc[:, LANES:])
        o = on[:tq] * lo + on[tq:] * hi
        o_ref[0, :, j * LANES:(j + 1) * LANES] = o.astype(o_ref.dtype)

    pending = None
    for j in range(C_HEADS // 2):
        cur = scores(j)
        if pending is not None:
            finish(*pending)
        pending = cur
    finish(*pending)

    tprev = pl.multiple_of(jnp.maximum(i - 1, 0) * tq, tq)
    tnext = pl.multiple_of(jnp.minimum(i + 1, n_blk - 1) * tq, tq)
    has_prev = jnp.where(i > 0, 1.0, 0.0).astype(F32)
    has_next = jnp.where(i < n_blk - 1, 1.0, 0.0).astype(F32)
    tpos = t0 + lax.broadcasted_iota(jnp.int32, (tq, 1), 0)
    for g in range(D_GROUPS):
        half = POOL_WINDOWS[g] // 2
        gs = slice(g * D_GROUP_DIM, (g + 1) * D_GROUP_DIM)
        cur = du_ref[0, pl.ds(t0, tq), gs]
        prv = du_ref[0, pl.ds(tprev + (tq - POOL_EDGE), POOL_EDGE), gs]
        nxt = du_ref[0, pl.ds(tnext, POOL_EDGE), gs]

        def band_sum(band, u):
            u_hi, u_lo = _split_bf16(u)
            return _dot(band, u_hi) + _dot(band, u_lo)

        wsum = band_sum(band_ref[g], cur)
        wsum = jnp.concatenate([wsum[:POOL_EDGE] + has_prev * band_sum(edge_ref[g, 0], prv),
                                wsum[POOL_EDGE:tq - POOL_EDGE],
                                wsum[tq - POOL_EDGE:] + has_next * band_sum(edge_ref[g, 1], nxt)], axis=0)
        cnt = (jnp.minimum(tpos + half, seq) - jnp.maximum(tpos - half, 0)).astype(F32)
        pooled = wsum * (1.0 / cnt) - cur
        od = _dot(pooled.astype(BF16), pw_ref[g]) * ps_ref[:, gs]
        o_ref[0, :, cw + g * D_GROUP_DIM:cw + (g + 1) * D_GROUP_DIM] = od.astype(o_ref.dtype)


def _na_geometry(n_rows):
    kh = min(NA_ROWS, n_rows)
    slab = min(n_rows, NA_QROWS + kh)
    n_blk = n_rows // NA_QROWS
    bases = [min(max(i * NA_QROWS - NA_ROWS // 2, 0), n_rows - slab) for i in range(n_blk)]
    sigs, type_of = [], []
    for i in range(n_blk):
        sig = tuple((min(max(i * NA_QROWS + r - kh // 2, 0), n_rows - kh) - bases[i],
                     i * NA_QROWS + r - bases[i]) for r in range(NA_QROWS))
        if sig not in sigs:
            sigs.append(sig)
        type_of.append(sigs.index(sig))
    return kh, slab, n_blk, sigs, type_of


def _na_bias_table(rpb, n_rows):
    kh, slab, _, sigs, _ = _na_geometry(n_rows)
    kw = NA_COLS
    n_dc = 2 * NA_COLS - 1
    w = np.arange(GRID_W)
    cstart = np.clip(w - kw // 2, 0, GRID_W - kw)
    col_ok = (w[None, :] >= cstart[:, None]) & (w[None, :] < cstart[:, None] + kw)
    dc = np.clip(w[None, :] - w[:, None] + (NA_COLS - 1), 0, n_dc - 1)
    heads = rpb.shape[0]
    onehot = ((dc[None] == np.arange(n_dc)[:, None, None]) & col_ok[None]).astype(np.float32)
    toep = jnp.einsum("hrd,dwj->hwrj", rpb, jnp.asarray(onehot), precision=lax.Precision.HIGHEST)
    toep = jnp.where(jnp.asarray(col_ok)[:, None, :], toep, NEG_BIG)
    pad = slab + NA_QROWS
    toep = jnp.pad(toep, ((0, 0), (0, 0), (pad, pad), (0, 0)), constant_values=NEG_BIG)
    toep = toep.reshape(heads, GRID_W, -1)
    blocks = []
    for sig in sigs:
        for rs_rel, qr_rel in sig:
            d0 = pad - qr_rel + (NA_ROWS - 1)
            in_win = np.repeat(np.array([rs_rel <= m < rs_rel + kh for m in range(slab)]), GRID_W)
            blk = lax.slice_in_dim(toep, d0 * GRID_W, (d0 + slab) * GRID_W, axis=2)
            blocks.append(jnp.where(jnp.asarray(in_win), blk, NEG_BIG))
    big = jnp.stack(blocks, axis=1)
    return big.reshape(heads, len(sigs), NA_QROWS * GRID_W, slab * GRID_W)


POOL_EDGE = 16


def _pool_bands(tq):
    t = np.arange(tq)[:, None]
    sidx = np.arange(tq)[None, :]
    main, edge = [], []
    for wdw in POOL_WINDOWS:
        half = wdw // 2
        full = [((sidx + (m - 1) * tq >= t - half) & (sidx + (m - 1) * tq < t + half)).astype(np.float32)
                for m in range(3)]
        main.append(full[1])
        edge.append(np.stack([full[0][:POOL_EDGE, tq - POOL_EDGE:], full[2][tq - POOL_EDGE:, :POOL_EDGE]]))
    return (jnp.asarray(np.stack(main), BF16),
            jnp.asarray(np.stack(edge), BF16))


def _odd_mixer_call(qkv, ckv, rpb, du, pool_w, pool_scale):
    b, s, n = qkv.shape
    n_rows = s // GRID_W
    _, slab, n_blk, _, type_of = _na_geometry(n_rows)
    tq = NA_QROWS * GRID_W
    bias = _na_bias_table(rpb, n_rows)
    bands, edges = _pool_bands(tq)

    def bias_map(bi, i):
        t = jnp.int32(type_of[-1])
        for blk in range(n_blk - 2, -1, -1):
            t = jnp.where(i == blk, jnp.int32(type_of[blk]), t)
        return (0, t, 0, 0)

    kern = functools.partial(_odd_mixer_kernel, tq=tq, slab_rows=slab, n_rows=n_rows, n_blk=n_blk, seq=s)
    return pl.pallas_call(
        kern,
        out_shape=jax.ShapeDtypeStruct((b, s, 1024), BF16),
        grid=(b, n_blk),
        in_specs=[pl.BlockSpec((1, s, n), lambda bi, i: (bi, 0, 0)),
                  pl.BlockSpec((1,) + ckv.shape[1:], lambda bi, i: (bi, 0, 0)),
                  pl.BlockSpec((bias.shape[0], None) + bias.shape[2:], bias_map),
                  pl.BlockSpec((1, s, du.shape[2]), lambda bi, i: (bi, 0, 0)),
                  pl.BlockSpec(bands.shape, lambda bi, i: (0, 0, 0)),
                  pl.BlockSpec(edges.shape, lambda bi, i: (0, 0, 0, 0)),
                  pl.BlockSpec(pool_w.shape, lambda bi, i: (0, 0, 0)),
                  pl.BlockSpec(pool_scale.shape, lambda bi, i: (0, 0))],
        out_specs=pl.BlockSpec((1, tq, 1024), lambda bi, i: (bi, i, 0)),
        compiler_params=_params(("arbitrary", "arbitrary")),
        name="odd_mixer",
    )(qkv, ckv, bias, du, bands, edges, pool_w, pool_scale)


OUTPROJ_SUB = 256


def _outproj_kernel(mix_ref, w_ref, x_ref, g1_ref, lng_ref, lnb_ref, sc_ref, sh_ref, r2_ref, rh_ref,
                    x1_ref, hm_ref, aff_ref, *, n_exp):
    tm = x_ref.shape[1]
    subs = [slice(r0, r0 + OUTPROJ_SUB) for r0 in range(0, tm, OUTPROJ_SUB)]
    ys = [_dot(mix_ref[0, rows, :], w_ref[...]) for rows in subs]
    for rows, y in zip(subs, ys):
        x1 = _layer_norm(DEEPNORM_ALPHA * x_ref[0, rows, :] + g1_ref[0] * y, lng_ref[...], lnb_ref[...])
        x1_ref[0, rows, :] = x1
        hm = x1 * (1.0 + sc_ref[0]) + sh_ref[0]
        hm_hi = hm.astype(BF16)
        hm_ref[0, rows, :] = hm_hi
        hm_lo = (hm - hm_hi.astype(F32)).astype(BF16)
        part = _dot(hm_hi, r2_ref[...])
        logits = part[:, :n_exp] + part[:, n_exp:] + _dot(hm_lo, rh_ref[...])
        ex = jnp.exp(logits - logits.max(axis=-1, keepdims=True))
        aff_ref[0, rows, :] = ex * (1.0 / ex.sum(axis=-1, keepdims=True))


def _outproj_call(mix, w, x, g1, lng, lnb, sc2, sh2, router, name):
    b, s, d = x.shape
    dm = mix.shape[2]
    e = router.shape[1]
    tm = min(4 * OUTPROJ_SUB, s)
    r_hi = router.astype(BF16)
    r_lo = (router - r_hi.astype(F32)).astype(BF16)
    r2 = jnp.concatenate([r_hi, r_lo], axis=1)
    row = pl.BlockSpec((1, d), lambda bi, i: (0, 0))
    tile = lambda n: pl.BlockSpec((1, tm, n), lambda bi, i: (bi, i, 0))
    return pl.pallas_call(
        functools.partial(_outproj_kernel, n_exp=e),
        out_shape=[jax.ShapeDtypeStruct((b, s, d), F32),
                   jax.ShapeDtypeStruct((b, s, d), BF16),
                   jax.ShapeDtypeStruct((b, s, e), F32)],
        grid=(b, s // tm),
        in_specs=[tile(dm),
                  pl.BlockSpec((dm, d), lambda bi, i: (0, 0)),
                  tile(d),
                  g1.spec(), row, row, sc2.spec(), sh2.spec(),
                  pl.BlockSpec((d, 2 * e), lambda bi, i: (0, 0)),
                  pl.BlockSpec((d, e), lambda bi, i: (0, 0))],
        out_specs=[tile(d), tile(d), tile(e)],
        compiler_params=_params(("arbitrary", "arbitrary")),
        name=name,
    )(mix, w, x, g1.arr, lng, lnb, sc2.arr, sh2.arr, r2, r_hi)


def _lane_cumsum(m):
    rows, s = m.shape
    r_i = lax.broadcasted_iota(jnp.int32, (LANES, LANES), 0)
    c_i = lax.broadcasted_iota(jnp.int32, (LANES, LANES), 1)
    tri = jnp.where(r_i <= c_i, 1.0, 0.0).astype(BF16)
    carry = jnp.zeros((rows, 1), F32)
    out = []
    for c in range(s // LANES):
        blk = m[:, c * LANES:(c + 1) * LANES]
        out.append(_dot(blk.astype(BF16), tri) + carry)
        carry = carry + blk.sum(axis=-1, keepdims=True)
    return jnp.concatenate(out, axis=-1)


ROUTE_TILE = 256


def _route_kernel(aff_ref, pos_ref, cnt_ref, *, cap):
    a = aff_ref[...]
    thr = jnp.zeros((a.shape[0], 1), jnp.int32)
    for bit in range(30, -1, -1):
        cand = thr | jnp.int32(1 << bit)
        cnt = jnp.where(a >= pltpu.bitcast(cand, F32), 1.0, 0.0).sum(axis=-1, keepdims=True)
        thr = jnp.where(cnt >= cap, cand, thr)
    gt = jnp.where(a >= pltpu.bitcast(thr + 1, F32), 1.0, 0.0)
    eq = jnp.where(a >= pltpu.bitcast(thr, F32), 1.0, 0.0) - gt
    need = cap - gt.sum(axis=-1, keepdims=True)
    sel = gt + eq * jnp.where(_lane_cumsum(eq) <= need, 1.0, 0.0)
    pos_ref[...] = jnp.where(sel > 0.5, _lane_cumsum(sel) - 1.0, -1.0)
    tok = lax.broadcasted_iota(jnp.int32, (a.shape[1], LANES), 0)
    tile = lax.broadcasted_iota(jnp.int32, (a.shape[1], LANES), 1)
    before = jnp.where(tok < tile * ROUTE_TILE, 1.0, 0.0).astype(BF16)
    cnt_ref[...] = _dot(sel.astype(BF16), before)


def _route_call(aff, cap, name):
    b, e, s = aff.shape
    n_tiles = s // min(ROUTE_TILE, s)
    full = pl.BlockSpec((b * e, s), lambda i: (0, 0))
    pos, cnt = pl.pallas_call(
        functools.partial(_route_kernel, cap=cap),
        out_shape=[jax.ShapeDtypeStruct((b * e, s), F32), jax.ShapeDtypeStruct((b * e, LANES), F32)],
        grid=(1,),
        in_specs=[full],
        out_specs=[full, pl.BlockSpec((b * e, LANES), lambda i: (0, 0))],
        compiler_params=_params(("arbitrary",)),
        name=name,
    )(aff.reshape(b * e, s))
    return pos.reshape(b, e, s), cnt[:, :n_tiles + 1].astype(jnp.int32).reshape(-1)


def _gather_kernel(cnt_ref, pos_ref, aff_ref, hm_ref, xg_ref, gs_ref, *, cap, n_exp, win, n_tiles):
    bi, i = pl.program_id(0), pl.program_id(1)

    @pl.when(i == 0)
    def _():
        xg_ref[...] = jnp.zeros(xg_ref.shape, xg_ref.dtype)
        gs_ref[...] = jnp.zeros(gs_ref.shape, gs_ref.dtype)

    starts, short = [], None
    for ei in range(n_exp):
        base = (bi * n_exp + ei) * (n_tiles + 1) + i
        st = jnp.minimum((cnt_ref[base] // SLOT_ALIGN) * SLOT_ALIGN, cap - win)
        starts.append(pl.multiple_of(st, SLOT_ALIGN))
        miss = cnt_ref[base + 1] > st + win
        short = miss if short is None else jnp.logical_or(short, miss)

    def scatter_rows(n_rows, first, stacked):
        row = lax.broadcasted_iota(jnp.int32, (n_rows, 1), 0).astype(F32)
        blocks = []
        for ei in range(n_exp):
            rel = pos_ref[0, ei:ei + 1, :] if first is None else pos_ref[0, ei:ei + 1, :] - first[ei].astype(F32)
            hit = rel == row
            blocks.append(jnp.where(hit, 1.0, 0.0).astype(BF16))
            gate = jnp.where(hit, aff_ref[0, ei:ei + 1, :], 0.0).sum(axis=-1, keepdims=True)
            rows = slice(None) if first is None else pl.ds(first[ei], n_rows)
            gs_ref[0, ei, rows, :] += gate
        if stacked:
            part = _dot(jnp.concatenate(blocks, axis=0), hm_ref[0])
        for ei in range(n_exp):
            rows = slice(None) if first is None else pl.ds(first[ei], n_rows)
            sub = part[ei * n_rows:(ei + 1) * n_rows] if stacked else _dot(blocks[ei], hm_ref[0])
            xg_ref[0, ei, rows, :] += sub.astype(xg_ref.dtype)

    @pl.when(jnp.logical_not(short))
    def _():
        scatter_rows(win, starts, True)

    @pl.when(short)
    def _():
        scatter_rows(cap, None, False)


def _gather_call(cnt, pos, aff, hm, cap, name):
    b, e, s = pos.shape
    d = hm.shape[2]
    tm = min(ROUTE_TILE, s)
    n_tiles = s // tm
    win = min(64, cap)
    tok = pl.BlockSpec((1, e, tm), lambda bi, i, c: (bi, 0, i))
    return pl.pallas_call(
        functools.partial(_gather_kernel, cap=cap, n_exp=e, win=win, n_tiles=n_tiles),
        out_shape=[jax.ShapeDtypeStruct((b, e, cap, d), BF16),
                   jax.ShapeDtypeStruct((b, e, cap, 1), F32)],
        grid_spec=pltpu.PrefetchScalarGridSpec(
            num_scalar_prefetch=1,
            grid=(b, n_tiles),
            in_specs=[tok, tok, pl.BlockSpec((1, tm, d), lambda bi, i, c: (bi, i, 0))],
            out_specs=[pl.BlockSpec((1, e, cap, d), lambda bi, i, c: (bi, 0, 0, 0)),
                       pl.BlockSpec((1, e, cap, 1), lambda bi, i, c: (bi, 0, 0, 0))]),
        compiler_params=_params(("arbitrary", "arbitrary")),
        name=name,
    )(cnt, pos, aff, hm)


def _ffn_kernel(*refs, n_grp, row_chunks, n_steps):
    xg_refs = refs[:n_grp]
    gs_refs = refs[n_grp:2 * n_grp]
    wg_ref, wu_ref, wd_ref = refs[2 * n_grp:2 * n_grp + 3]
    y_refs = refs[2 * n_grp + 3:3 * n_grp + 3]
    acc_refs = refs[3 * n_grp + 3:]
    f = pl.program_id(1)
    last = pl.num_programs(1) - 1
    wg = wg_ref[0].astype(BF16)

    def body(first, final):
        wu = wd = None
        for xg_ref, gs_ref, y_ref, acc_ref, nb in zip(xg_refs, gs_refs, y_refs, acc_refs, row_chunks):
            bt, _, cap, d = xg_ref.shape
            for b0 in range(0, bt, nb):
                rows = nb * cap
                r0 = b0 * cap
                x = xg_ref[b0:b0 + nb, 0].reshape(rows, d)
                hg = _dot(x, wg)
                if wu is None:
                    wu = wu_ref[0].astype(BF16)
                hu = _dot(x, wu)
                if wd is None:
                    wd = wd_ref[0].astype(BF16)
                hid = (hg * (1.0 / (1.0 + jnp.exp(-hg))) * hu).astype(BF16)
                part = _dot(hid, wd)
                if not first:
                    part = acc_ref[r0:r0 + rows, :] + part
                if final:
                    gate = gs_ref[b0:b0 + nb, 0].reshape(rows, 1)
                    y_ref[b0:b0 + nb, 0] = (part * gate).reshape(nb, cap, d).astype(y_ref.dtype)
                else:
                    acc_ref[r0:r0 + rows, :] = part

    if n_steps == 1:
        body(True, True)
    else:
        pl.when(f == 0)(functools.partial(body, True, False))
        pl.when(jnp.logical_and(f > 0, f < last))(functools.partial(body, False, False))
        pl.when(f == last)(functools.partial(body, False, True))


def _ffn_call(xgs, gss, w_gate, w_up, w_down, layer, name):
    _, e, d, ff = w_gate.shape
    tf = 512 if ff % 512 == 0 else ff
    n_grp = len(xgs)
    row_chunks = []
    for xg in xgs:
        bt, _, cap, _ = xg.shape
        nb = max(1, min(bt, 512 // cap))
        while bt % nb:
            nb -= 1
        row_chunks.append(nb)
    tok = lambda a: pl.BlockSpec((a.shape[0], 1) + a.shape[2:], lambda ei, fi: (0, ei, 0, 0))
    kern = functools.partial(_ffn_kernel, n_grp=n_grp, row_chunks=tuple(row_chunks), n_steps=ff // tf)
    return pl.pallas_call(
        kern,
        out_shape=[jax.ShapeDtypeStruct(xg.shape, BF16) for xg in xgs],
        grid=(e, ff // tf),
        in_specs=[tok(a) for a in xgs] + [tok(a) for a in gss]
                 + [pl.BlockSpec((None, 1, d, tf), lambda ei, fi: (layer, ei, 0, fi)),
                    pl.BlockSpec((None, 1, d, tf), lambda ei, fi: (layer, ei, 0, fi)),
                    pl.BlockSpec((None, 1, tf, d), lambda ei, fi: (layer, ei, fi, 0))],
        out_specs=[tok(a) for a in xgs],
        scratch_shapes=[pltpu.VMEM((xg.shape[0] * xg.shape[2], d), F32) for xg in xgs],
        compiler_params=_params(("arbitrary", "arbitrary")),
        name=name,
    )(*xgs, *gss, w_gate, w_up, w_down)


MXU_DEPTH = 256
SLOT_ALIGN = 16


def _combine_kernel(cnt_ref, pos_ref, y_ref, x_ref, g_ref, lng_ref, lnb_ref, *rest, cap, n_exp, win, n_tiles,
                    n_sub, proj_cfg=None):
    if proj_cfg is None:
        (o_ref,), proj_refs = rest, ()
    else:
        proj_refs, o_ref = rest[:3] + rest[4:], rest[3]
    bi, i = pl.program_id(0), pl.program_id(1)
    sub = pos_ref.shape[1] // n_sub
    grp = MXU_DEPTH // win
    starts, short = [], None
    for r in range(n_sub):
        starts.append([])
        for ei in range(n_exp):
            base = (bi * n_exp + ei) * (n_tiles + 1) + i * n_sub + r
            st = jnp.minimum((cnt_ref[base] // SLOT_ALIGN) * SLOT_ALIGN, cap - win)
            starts[r].append(st)
            miss = cnt_ref[base + 1] > st + win
            short = miss if short is None else jnp.logical_or(short, miss)

    def windowed(r):
        pos = pos_ref[0, r * sub:(r + 1) * sub, :]
        lane = lax.broadcasted_iota(jnp.int32, (1, grp * win), 1)
        lane_f = lane.astype(F32)
        acc = None
        for k in range(n_exp // grp):
            tgt, rows = None, []
            for u in range(grp - 1, -1, -1):
                ei = k * grp + u
                st = starts[r][ei]
                rel = pos[:, ei:ei + 1] - st.astype(F32)
                rel = jnp.where(rel >= 0.0, jnp.where(rel < win, rel + float(u * win), -1.0), -1.0)
                tgt = rel if tgt is None else jnp.where(lane < (u + 1) * win, rel, tgt)
                rows.insert(0, y_ref[0, pl.ds(pl.multiple_of(ei * cap + st, SLOT_ALIGN), win), :])
            onehot = jnp.where(tgt == lane_f, 1.0, 0.0).astype(BF16)
            part = _dot(onehot, jnp.concatenate(rows, axis=0))
            acc = part if acc is None else acc + part
        return acc

    def dense(r):
        pos = pos_ref[0, r * sub:(r + 1) * sub, :]
        slot = lax.broadcasted_iota(jnp.int32, (1, cap), 1).astype(F32)
        acc = None
        for ei in range(n_exp):
            onehot = jnp.where(pos[:, ei:ei + 1] == slot, 1.0, 0.0).astype(BF16)
            part = _dot(onehot, y_ref[0, ei * cap:(ei + 1) * cap, :])
            acc = part if acc is None else acc + part
        return acc

    def epilogue(r, acc):
        rows = slice(r * sub, (r + 1) * sub)
        x_new = _layer_norm(DEEPNORM_ALPHA * x_ref[0, rows, :] + g_ref[0] * acc, lng_ref[...], lnb_ref[...])
        o_ref[0, rows, :] = x_new
        if proj_cfg is not None:
            sc_ref, sh_ref, w_ref = proj_refs[:3]
            h = (x_new * (1.0 + sc_ref[0]) + sh_ref[0]).astype(BF16)
            _inproj_body(h, w_ref, None, None, None, proj_refs[3:], rows=rows, **proj_cfg)

    @pl.when(jnp.logical_not(short))
    def _():
        accs = [windowed(r) for r in range(n_sub)]
        for r in range(n_sub):
            epilogue(r, accs[r])

    @pl.when(short)
    def _():
        for r in range(n_sub):
            epilogue(r, dense(r))


def _combine_call(cnt, pos_t, y, x, g2, lng, lnb, cap, name, proj=None):
    b, s, d = x.shape
    e = pos_t.shape[2]
    n_tiles = s // min(ROUTE_TILE, s)
    n_sub = 2 if n_tiles % 2 == 0 else 1
    tm = s // (n_tiles // n_sub)
    win = min(64, cap)
    row = pl.BlockSpec((1, d), lambda bi, i, c: (0, 0))
    tile = lambda n: pl.BlockSpec((1, tm, n), lambda bi, i, c: (bi, i, 0))
    in_specs = [tile(e), pl.BlockSpec((1, e * cap, d), lambda bi, i, c: (bi, 0, 0)), tile(d), g2.spec(), row, row]
    operands = [cnt, pos_t, y.reshape(b, e * cap, d), x, g2.arr, lng, lnb]
    out_shape = [jax.ShapeDtypeStruct((b, s, d), F32)]
    out_specs = [tile(d)]
    proj_cfg = None
    if proj is not None:
        sc, sh, w, blocks, out_defs, ones_cols = proj
        in_specs += [sc.spec(), sh.spec(), pl.BlockSpec(w.shape, lambda bi, i, c: (0, 0))]
        operands += [sc.arr, sh.arr, w]
        out_shape += [jax.ShapeDtypeStruct((b, s, nc), dt) for nc, dt in out_defs]
        out_specs += [tile(nc) for nc, _ in out_defs]
        proj_cfg = dict(blocks=tuple(blocks), ones_cols=tuple(ones_cols), chunk=512, use_rope=False)
    res = pl.pallas_call(
        functools.partial(_combine_kernel, cap=cap, n_exp=e, win=win, n_tiles=n_tiles, n_sub=n_sub,
                          proj_cfg=proj_cfg),
        out_shape=out_shape,
        grid_spec=pltpu.PrefetchScalarGridSpec(
            num_scalar_prefetch=1, grid=(b, n_tiles // n_sub), in_specs=in_specs, out_specs=out_specs),
        compiler_params=_params(("arbitrary", "arbitrary")),
        name=name,
    )(*operands)
    return res[0] if proj is None else res


def _ffn_sublayer(streams, projs, lng, lnb, router, w_gate, w_up, w_down, layer):
    staged = []
    for si, (mix, w_out, x, g1, sc2, sh2, g2) in enumerate(streams):
        tag = f"l{layer}s{si}"
        n_tok = x.shape[1]
        cap = EC_FACTOR * n_tok // N_EXPERTS
        x1, hm, aff_t = _outproj_call(mix, w_out, x, g1, lng[0:1], lnb[0:1], sc2, sh2, router, "outproj_" + tag)
        aff = jnp.swapaxes(aff_t, 1, 2)
        pos, cnt = _route_call(aff, cap, "route_" + tag)
        xg, gs = _gather_call(cnt, pos, aff, hm, cap, "gather_" + tag)
        staged.append((x1, pos, cnt, xg, gs, g2, cap, tag))
    ys = _ffn_call([st[3] for st in staged], [st[4] for st in staged], w_gate, w_up, w_down, layer,
                   f"ffn_l{layer}")
    outs = []
    for (x1, pos, cnt, _, _, g2, cap, tag), y, proj in zip(staged, ys, projs):
        outs.append(_combine_call(cnt, jnp.swapaxes(pos, 1, 2), y, x1, g2, lng[1:2], lnb[1:2], cap,
                                  "combine_" + tag, proj))
    return outs


def kernel(x, c, ctx, c_ctx, ada_w, ada_b, ln_g, ln_b, l0_w_in, l0_w_out, l0_lam_q1, l0_lam_k1, l0_lam_q2,
           l0_lam_k2, l0_subln_g, l0_qnorm_g, l0_knorm_g, l1_w_in, l1_w_out, l1_rpb, l1_pool_w, l1_pool_scale,
           moe_router, moe_w_gate, moe_w_up, moe_w_down):
    b, s, d = x.shape
    n_ctx = ctx.shape[1]

    rows = -(-(b + 1) // 8) * 8
    cc = jnp.concatenate([c, c_ctx[None], jnp.zeros((rows - b - 1, d), F32)], axis=0)
    mod = _ada_call(cc, ada_w, ada_b)

    mod5 = mod.reshape(DEPTH, rows, 6, 1, d)

    def mods(i):
        return ([_Mod(mod5, i, k, None) for k in range(6)], [_Mod(mod5, i, k, b) for k in range(6)])

    rope = _rope_tables(s)
    rope_id = (jnp.ones((n_ctx, LANES), F32), jnp.zeros((n_ctx, LANES), F32), jnp.zeros((n_ctx, LANES), F32))
    gmat = jnp.asarray(np.where((np.arange(LANES)[:, None] // HEAD_DIM) == (np.arange(LANES)[None, :] // HEAD_DIM),
                                1.0 / HEAD_DIM, 0.0), BF16)
    tile2 = lambda g: jnp.concatenate([g, g]).reshape(1, LANES)

    (sh1, sc1, g1, sh2, sc2, g2), (csh1, csc1, cg1, csh2, csc2, cg2) = mods(0)
    bq0 = 3 * 512
    pair_heads = [hh for j in range(B_HEADS // 2) for hh in (j, j + B_HEADS // 2)]
    w_in_b, w_out_b = l0_w_in.astype(BF16), l0_w_out.astype(BF16)
    w_in0 = jnp.concatenate([w_in_b[:, :bq0]]
                            + [w_in_b[:, bq0 + hh * HEAD_DIM:bq0 + (hh + 1) * HEAD_DIM] for hh in pair_heads]
                            + [w_in_b[:, bq0 + 512:]], axis=1)
    w_out0 = jnp.concatenate([w_out_b[:512]]
                             + [w_out_b[512 + hh * HEAD_DIM:512 + (hh + 1) * HEAD_DIM] for hh in pair_heads], axis=0)
    gains0 = jnp.concatenate([tile2(l0_qnorm_g), tile2(l0_knorm_g)], axis=0)
    q_exp2 = Q_SCALE * math.log2(math.e)
    blocks0 = ([("rope", 0, q_exp2, 0, EV_AQ + k * LANES) for k in range(4)]
               + [("rope", 0, 1.0, 0, EV_AK + k * LANES) for k in range(4)]
               + [("plain", 0, 1.0, 0, EV_AV + k * 2 * LANES) for k in range(4)]
               + [("norm", 0, q_exp2, 0, EV_BQ + k * LANES) for k in range(4)]
               + [("norm", 1, 1.0, 0, EV_BK), ("plain", 0, 1.0, 0, EV_BV)])
    ones0 = [(0, EV_AV + (2 * k + 1) * LANES) for k in range(4)] + [(0, EV_BV + LANES)]
    qkv = _inproj_call(x, sc1, sh1, w_in0, blocks0, [(EV_WIDTH, BF16)], rope, gains0, gmat,
                       use_rope=True, name="inproj_l0", ones_cols=ones0)[0]
    qkv_c = _inproj_call(ctx, csc1, csh1, w_in0, blocks0, [(EV_WIDTH, BF16)], rope_id, gains0, gmat,
                         use_rope=False, name="inproj_l0c", ones_cols=ones0)[0]
    lam_init = 0.8 - 0.6 * math.exp(-0.3 * 0)
    lamv = jnp.stack([l0_lam_q1, l0_lam_k1, l0_lam_q2, l0_lam_k2], axis=0)
    sub_g = l0_subln_g.reshape(1, LANES)
    mix = _attn_even_call(qkv, [qkv, qkv_c], lamv, sub_g, lam_init, "attn_l0")
    mix_c = _attn_even_call(qkv_c, [qkv_c], lamv, sub_g, lam_init, "attn_l0c")
    (sh1n, sc1n, g1n, sh2n, sc2n, g2n), (csh1n, csc1n, _, _, _, _) = mods(1)
    cw = C_HEADS * HEAD_DIM
    w_in1 = l1_w_in.astype(BF16)
    blocks1 = ([("plain", 0, q_exp2, 0, k * LANES) for k in range(4)]
               + [("plain", 0, 1.0, 0, cw + k * LANES) for k in range(4)]
               + [("plain", 0, 1.0, 0, 2 * cw + 2 * k * LANES) for k in range(4)]
               + [("plain", 0, 1.0, 1, k * LANES) for k in range(4)])
    ones1 = [(0, 2 * cw + (2 * k + 1) * LANES) for k in range(4)]
    blocks1c = ([("plain", 0, 1.0, 0, k * LANES) for k in range(4)]
                + [("plain", 0, 1.0, 0, cw + 2 * k * LANES) for k in range(4)])
    ones1c = [(0, cw + (2 * k + 1) * LANES) for k in range(4)]
    proj_lat = (sc1n, sh1n, w_in1, blocks1, [(4 * cw, BF16), (D_GROUPS * D_GROUP_DIM, F32)], ones1)
    proj_ctx = (csc1n, csh1n, w_in1[:, cw:3 * cw], blocks1c, [(3 * cw, BF16)], ones1c)
    (x, qkv1, du), (ctx, ckv) = _ffn_sublayer(
        [(mix, w_out0, x, g1, sc2, sh2, g2), (mix_c, w_out0, ctx, cg1, csc2, csh2, cg2)],
        [proj_lat, proj_ctx], ln_g[0], ln_b[0], moe_router[0], moe_w_gate, moe_w_up, moe_w_down, 0)

    mix1 = _odd_mixer_call(qkv1, ckv, l1_rpb * math.log2(math.e), du, l1_pool_w.astype(BF16),
                           l1_pool_scale.reshape(1, -1))
    (x,) = _ffn_sublayer([(mix1, l1_w_out.astype(BF16), x, g1n, sc2n, sh2n, g2n)],
                         [None], ln_g[1], ln_b[1], moe_router[1], moe_w_gate, moe_w_up, moe_w_down, 1)
    return x
```

```python
import collections
import functools
import math

import numpy as np

import jax
import jax.numpy as jnp
from jax import lax
from jax.experimental import pallas as pl
from jax.experimental.pallas import tpu as pltpu

F32 = jnp.float32
BF16 = jnp.bfloat16

DEPTH = 2
GRID_W = 64
HEAD_DIM = 64
A_HEADS = 4
B_HEADS = 8
B_KV_HEADS = 2
C_HEADS = 8
D_GROUPS = 4
D_GROUP_DIM = 128
POOL_WINDOWS = (2, 4, 8, 16)
NA_ROWS = 8
NA_COLS = 16
N_EXPERTS = 16
EC_FACTOR = 2
ROPE_THETA = 10000.0
LN_EPS = 1e-5
RMS_EPS = 1e-6
DEEPNORM_ALPHA = (2 * DEPTH) ** 0.25
Q_SCALE = HEAD_DIM ** -0.5

LANES = 128
VMEM_LIMIT = 56 * 1024 * 1024

NA_QROWS = 4
NEG_BIG = -1e30

_NN = (((1,), (0,)), ((), ()))
_NT = (((1,), (1,)), ((), ()))


def _dot(a, b, dims=_NN):
    return lax.dot_general(a, b, dims, preferred_element_type=F32)


def _split_bf16(a):
    hi = a.astype(BF16)
    lo = (a - hi.astype(F32)).astype(BF16)
    return hi, lo


def _dot3(a, b, dims=_NN):
    a_hi, a_lo = _split_bf16(a)
    b_hi, b_lo = _split_bf16(b)
    return _dot(a_hi, b_hi, dims) + (_dot(a_hi, b_lo, dims) + _dot(a_lo, b_hi, dims))


class _Mod(collections.namedtuple("_Mod", "arr layer k row")):
    def spec(self):
        d = self.arr.shape[-1]

        def index_map(*grid):
            return (self.layer, grid[0] if self.row is None else self.row, self.k, 0, 0)

        return pl.BlockSpec((None, None, 1, 1, d), index_map)


def _params(sem):
    return pltpu.CompilerParams(dimension_semantics=sem, vmem_limit_bytes=VMEM_LIMIT)


def _layer_norm(z, g, b):
    mu = jnp.mean(z, axis=-1, keepdims=True)
    zc = z - mu
    var = jnp.mean(zc * zc, axis=-1, keepdims=True)
    return zc * lax.rsqrt(var + LN_EPS) * g + b


def _lane_masks():
    lane = lax.broadcasted_iota(jnp.int32, (1, LANES), 1)
    lo = jnp.where(lane < HEAD_DIM, 1.0, 0.0).astype(F32)
    return lo, 1.0 - lo


def _ada_kernel(c_ref, w_ref, b_ref, o_ref):
    c = c_ref[...]
    s = c * (1.0 / (1.0 + jnp.exp(-c)))
    o_ref[0] = _dot3(s, w_ref[0]) + b_ref[0]


def _ada_call(cc, ada_w, ada_b):
    depth, d, n = ada_w.shape
    rows = cc.shape[0]
    tn = 1536 if n % 1536 == 0 else n
    return pl.pallas_call(
        _ada_kernel,
        out_shape=jax.ShapeDtypeStruct((depth, rows, n), F32),
        grid=(depth, n // tn),
        in_specs=[pl.BlockSpec((rows, d), lambda l, j: (0, 0)),
                  pl.BlockSpec((1, d, tn), lambda l, j: (l, 0, j)),
                  pl.BlockSpec((1, 1, tn), lambda l, j: (l, 0, j))],
        out_specs=pl.BlockSpec((1, rows, tn), lambda l, j: (l, 0, j)),
        compiler_params=_params(("arbitrary", "arbitrary")),
        name="ada_mod",
    )(cc, ada_w, ada_b.reshape(depth, 1, n))


def _inproj_body(h, w_ref, rope_refs, gn_ref, gmat_ref, o_refs, *, blocks, ones_cols, chunk, use_rope,
                 rows=slice(None)):
    n = len(blocks) * LANES
    for oi, oc in ones_cols:
        o_refs[oi][0, rows, oc:oc + LANES] = jnp.ones((h.shape[0], LANES), o_refs[oi].dtype)
    starts = list(range(0, n, chunk))
    accs = {starts[0]: _dot(h, w_ref[:, starts[0]:min(starts[0] + chunk, n)])}
    for ci, c0 in enumerate(starts):
        cw = min(chunk, n - c0)
        if ci + 1 < len(starts):
            nxt = starts[ci + 1]
            accs[nxt] = _dot(h, w_ref[:, nxt:min(nxt + chunk, n)])
        acc = accs.pop(c0)
        for j in range(cw // LANES):
            kind, gain_row, factor, oi, oc = blocks[(c0 // LANES) + j]
            v = acc[:, j * LANES:(j + 1) * LANES]
            if kind == "norm":
                v2 = v * v
                hi, lo = _split_bf16(v2)
                ms = _dot(hi, gmat_ref[...]) + _dot(lo, gmat_ref[...])
                v = v * lax.rsqrt(ms + RMS_EPS) * gn_ref[gain_row:gain_row + 1, :]
            if kind in ("rope", "norm") and use_rope:
                cos_ref, sinp_ref, sinm_ref = rope_refs
                v = (v * cos_ref[...] + pltpu.roll(v, 16, 1) * sinp_ref[...]
                     + pltpu.roll(v, LANES - 16, 1) * sinm_ref[...])
            if factor != 1.0:
                v = v * factor
            o_refs[oi][0, rows, oc:oc + LANES] = v.astype(o_refs[oi].dtype)


def _inproj_kernel(x_ref, sc_ref, sh_ref, w_ref, cos_ref, sinp_ref, sinm_ref, gn_ref, gmat_ref, *o_refs, **cfg):
    h = (x_ref[0] * (1.0 + sc_ref[0]) + sh_ref[0]).astype(BF16)
    _inproj_body(h, w_ref, (cos_ref, sinp_ref, sinm_ref), gn_ref, gmat_ref, o_refs, **cfg)


def _inproj_call(x, sc, sh, w, blocks, out_defs, rope_tabs, gains, gmat, *, use_rope, name, ones_cols=()):
    b, s, d = x.shape
    n = w.shape[1]
    tm = min(1024, s)
    cos, sinp, sinm = rope_tabs
    tab_spec = pl.BlockSpec((tm, LANES), lambda bi, i: (i, 0))
    kern = functools.partial(_inproj_kernel, blocks=tuple(blocks), ones_cols=tuple(ones_cols), chunk=512,
                             use_rope=use_rope)
    return pl.pallas_call(
        kern,
        out_shape=[jax.ShapeDtypeStruct((b, s, nc), dt) for nc, dt in out_defs],
        grid=(b, s // tm),
        in_specs=[pl.BlockSpec((1, tm, d), lambda bi, i: (bi, i, 0)),
                  sc.spec(), sh.spec(),
                  pl.BlockSpec((d, n), lambda bi, i: (0, 0)),
                  tab_spec, tab_spec, tab_spec,
                  pl.BlockSpec(gains.shape, lambda bi, i: (0, 0)),
                  pl.BlockSpec(gmat.shape, lambda bi, i: (0, 0))],
        out_specs=[pl.BlockSpec((1, tm, nc), lambda bi, i: (bi, i, 0)) for nc, _ in out_defs],
        compiler_params=_params(("arbitrary", "arbitrary")),
        name=name,
    )(x, sc.arr, sh.arr, w, cos, sinp, sinm, gains, gmat)


def _rope_tables(s):
    n_freq = HEAD_DIM // 4
    t = np.arange(s)
    inv = ROPE_THETA ** (-np.arange(n_freq, dtype=np.float64) / n_freq)
    ang_r = (t // GRID_W)[:, None] * inv
    ang_c = (t % GRID_W)[:, None] * inv
    ang = np.concatenate([ang_r, ang_r, ang_c, ang_c] * (LANES // HEAD_DIM), axis=-1)
    first = (np.arange(LANES) % 32) < 16
    cos, sin = np.cos(ang), np.sin(ang)
    tabs = (cos, np.where(first, 0.0, sin), np.where(first, -sin, 0.0))
    return tuple(jnp.asarray(a.astype(np.float32)) for a in tabs)


EV_AQ, EV_AK, EV_AV = 0, 512, 1024
EV_BQ = EV_AV + A_HEADS * 2 * LANES
EV_BK = EV_BQ + (B_HEADS // 2) * LANES
EV_BV = EV_BK + LANES
EV_WIDTH = EV_BV + 2 * LANES


def _attn_even_kernel(q_ref, lam_ref, sg_ref, *refs, n_kv, tq, lam_init):
    kv_refs, o_ref = refs[:n_kv], refs[n_kv]
    lo, hi = _lane_masks()
    lo_b, hi_b = lo.astype(BF16), hi.astype(BF16)
    lv = lam_ref[...]
    lam = (jnp.exp(jnp.sum(lv[0:1] * lv[1:2], axis=-1, keepdims=True))
           - jnp.exp(jnp.sum(lv[2:3] * lv[3:4], axis=-1, keepdims=True)) + lam_init)

    def scores(q, kcol):
        qq = jnp.concatenate([q * lo_b, q * hi_b], axis=0)
        ss = [_dot(qq, kv[0, :, kcol:kcol + LANES], _NT) for kv in kv_refs]
        m = ss[0].max(axis=-1, keepdims=True)
        for s in ss[1:]:
            m = jnp.maximum(m, s.max(axis=-1, keepdims=True))
        return ss, m

    def weighted(ss, m, vcol):
        acc = None
        for s, kv in zip(ss, kv_refs):
            part = _dot(jnp.exp2(s - m).astype(BF16), kv[0, :, vcol:vcol + 2 * LANES])
            acc = part if acc is None else acc + part
        return acc[:, :LANES] * (1.0 / acc[:, LANES:])

    units = ([("a", h, EV_AQ + h * LANES, EV_AK + h * LANES, EV_AV + h * 2 * LANES) for h in range(A_HEADS)]
             + [("b", j, EV_BQ + j * LANES, EV_BK, EV_BV) for j in range(B_HEADS // 2)])

    def finish(unit, ss, m):
        kind, idx, _, _, vcol = unit
        on = weighted(ss, m, vcol)
        if kind == "a":
            o = on[:tq] - lam * on[tq:]
            ms = jnp.mean(o * o, axis=-1, keepdims=True)
            o = o * lax.rsqrt(ms + RMS_EPS) * sg_ref[...] * (1.0 - lam_init)
            o_ref[0, :, idx * LANES:(idx + 1) * LANES] = o.astype(o_ref.dtype)
        else:
            o = on[:tq] * lo + on[tq:] * hi
            o_ref[0, :, 512 + idx * LANES:512 + (idx + 1) * LANES] = o.astype(o_ref.dtype)

    pending = None
    for unit in units:
        cur = scores(q_ref[0, :, unit[2]:unit[2] + LANES], unit[3])
        if pending is not None:
            finish(*pending)
        pending = (unit,) + cur
    finish(*pending)


def _attn_even_call(q_arr, kv_arrs, lamv, subln_g, lam_init, name):
    b, sq, n = q_arr.shape
    tq = min(256, sq)
    kern = functools.partial(_attn_even_kernel, n_kv=len(kv_arrs), tq=tq, lam_init=lam_init)
    return pl.pallas_call(
        kern,
        out_shape=jax.ShapeDtypeStruct((b, sq, 1024), BF16),
        grid=(b, sq // tq),
        in_specs=[pl.BlockSpec((1, tq, n), lambda bi, i: (bi, i, 0)),
                  pl.BlockSpec(lamv.shape, lambda bi, i: (0, 0)),
                  pl.BlockSpec(subln_g.shape, lambda bi, i: (0, 0))]
                 + [pl.BlockSpec((1,) + a.shape[1:], lambda bi, i: (bi, 0, 0)) for a in kv_arrs],
        out_specs=pl.BlockSpec((1, tq, 1024), lambda bi, i: (bi, i, 0)),
        compiler_params=_params(("arbitrary", "arbitrary")),
        name=name,
    )(q_arr, lamv, subln_g, *kv_arrs)


def _odd_mixer_kernel(qkv_ref, ckv_ref, bias_ref, du_ref, band_ref, edge_ref, pw_ref, ps_ref, o_ref, *,
                      tq, slab_rows, n_rows, n_blk, seq):
    i = pl.program_id(1)
    lo, hi = _lane_masks()
    lo_b, hi_b = lo.astype(BF16), hi.astype(BF16)
    t0 = pl.multiple_of(i * tq, tq)
    base = jnp.clip(i * NA_QROWS - NA_ROWS // 2, 0, n_rows - slab_rows)
    k0 = pl.multiple_of(base * GRID_W, GRID_W)
    nk = slab_rows * GRID_W
    cw = C_HEADS * HEAD_DIM

    def scores(j):
        cs = slice(j * LANES, (j + 1) * LANES)
        q = qkv_ref[0, pl.ds(t0, tq), cs]
        qq = jnp.concatenate([q * lo_b, q * hi_b], axis=0)
        kl = qkv_ref[0, pl.ds(k0, nk), cw + j * LANES:cw + (j + 1) * LANES]
        bias = jnp.concatenate([bias_ref[2 * j], bias_ref[2 * j + 1]], axis=0)
        s_l = _dot(qq, kl, _NT) + bias
        s_c = _dot(qq, ckv_ref[0, :, cs], _NT)
        m = jnp.maximum(s_l.max(axis=-1, keepdims=True), s_c.max(axis=-1, keepdims=True))
        return j, s_l, s_c, m

    def finish(j, s_l, s_c, m):
        vl = qkv_ref[0, pl.ds(k0, nk), 2 * cw + 2 * j * LANES:2 * cw + 2 * (j + 1) * LANES]
        vc = ckv_ref[0, :, cw + 2 * j * LANES:cw + 2 * (j + 1) * LANES]
        acc = _dot(jnp.exp2(s_l - m).astype(BF16), vl) + _dot(jnp.exp2(s_c - m).astype(BF16), vc)
        on = acc[:, :LANES] * (1.0 / acc[:, LANES:])
        o = on[:tq] * lo + on[tq:] * hi
        o_ref[0, :, j * LANES:(j + 1) * LANES] = o.astype(o_ref.dtype)

    pending = None
    for j in range(C_HEADS // 2):
        cur = scores(j)
        if pending is not None:
            finish(*pending)
        pending = cur
    finish(*pending)

    tprev = pl.multiple_of(jnp.maximum(i - 1, 0) * tq, tq)
    tnext = pl.multiple_of(jnp.minimum(i + 1, n_blk - 1) * tq, tq)
    has_prev = jnp.where(i > 0, 1.0, 0.0).astype(F32)
    has_next = jnp.where(i < n_blk - 1, 1.0, 0.0).astype(F32)
    tpos = t0 + lax.broadcasted_iota(jnp.int32, (tq, 1), 0)
    for g in range(D_GROUPS):
        half = POOL_WINDOWS[g] // 2
        gs = slice(g * D_GROUP_DIM, (g + 1) * D_GROUP_DIM)
        cur = du_ref[0, pl.ds(t0, tq), gs]
        prv = du_ref[0, pl.ds(tprev + (tq - POOL_EDGE), POOL_EDGE), gs]
        nxt = du_ref[0, pl.ds(tnext, POOL_EDGE), gs]

        def band_sum(band, u):
            u_hi, u_lo = _split_bf16(u)
            return _dot(band, u_hi) + _dot(band, u_lo)

        wsum = band_sum(band_ref[g], cur)
        wsum = jnp.concatenate([wsum[:POOL_EDGE] + has_prev * band_sum(edge_ref[g, 0], prv),
                                wsum[POOL_EDGE:tq - POOL_EDGE],
                                wsum[tq - POOL_EDGE:] + has_next * band_sum(edge_ref[g, 1], nxt)], axis=0)
        cnt = (jnp.minimum(tpos + half, seq) - jnp.maximum(tpos - half, 0)).astype(F32)
        pooled = wsum * (1.0 / cnt) - cur
        od = _dot(pooled.astype(BF16), pw_ref[g]) * ps_ref[:, gs]
        o_ref[0, :, cw + g * D_GROUP_DIM:cw + (g + 1) * D_GROUP_DIM] = od.astype(o_ref.dtype)


def _na_geometry(n_rows):
    kh = min(NA_ROWS, n_rows)
    slab = min(n_rows, NA_QROWS + kh)
    n_blk = n_rows // NA_QROWS
    bases = [min(max(i * NA_QROWS - NA_ROWS // 2, 0), n_rows - slab) for i in range(n_blk)]
    sigs, type_of = [], []
    for i in range(n_blk):
        sig = tuple((min(max(i * NA_QROWS + r - kh // 2, 0), n_rows - kh) - bases[i],
                     i * NA_QROWS + r - bases[i]) for r in range(NA_QROWS))
        if sig not in sigs:
            sigs.append(sig)
        type_of.append(sigs.index(sig))
    return kh, slab, n_blk, sigs, type_of


def _na_bias_table(rpb, n_rows):
    kh, slab, _, sigs, _ = _na_geometry(n_rows)
    kw = NA_COLS
    n_dc = 2 * NA_COLS - 1
    w = np.arange(GRID_W)
    cstart = np.clip(w - kw // 2, 0, GRID_W - kw)
    col_ok = (w[None, :] >= cstart[:, None]) & (w[None, :] < cstart[:, None] + kw)
    dc = np.clip(w[None, :] - w[:, None] + (NA_COLS - 1), 0, n_dc - 1)
    heads = rpb.shape[0]
    onehot = ((dc[None] == np.arange(n_dc)[:, None, None]) & col_ok[None]).astype(np.float32)
    toep = jnp.einsum("hrd,dwj->hwrj", rpb, jnp.asarray(onehot), precision=lax.Precision.HIGHEST)
    toep = jnp.where(jnp.asarray(col_ok)[:, None, :], toep, NEG_BIG)
    pad = slab + NA_QROWS
    toep = jnp.pad(toep, ((0, 0), (0, 0), (pad, pad), (0, 0)), constant_values=NEG_BIG)
    toep = toep.reshape(heads, GRID_W, -1)
    blocks = []
    for sig in sigs:
        for rs_rel, qr_rel in sig:
            d0 = pad - qr_rel + (NA_ROWS - 1)
            in_win = np.repeat(np.array([rs_rel <= m < rs_rel + kh for m in range(slab)]), GRID_W)
            blk = lax.slice_in_dim(toep, d0 * GRID_W, (d0 + slab) * GRID_W, axis=2)
            blocks.append(jnp.where(jnp.asarray(in_win), blk, NEG_BIG))
    big = jnp.stack(blocks, axis=1)
    return big.reshape(heads, len(sigs), NA_QROWS * GRID_W, slab * GRID_W)


POOL_EDGE = 16


def _pool_bands(tq):
    t = np.arange(tq)[:, None]
    sidx = np.arange(tq)[None, :]
    main, edge = [], []
    for wdw in POOL_WINDOWS:
        half = wdw // 2
        full = [((sidx + (m - 1) * tq >= t - half) & (sidx + (m - 1) * tq < t + half)).astype(np.float32)
                for m in range(3)]
        main.append(full[1])
        edge.append(np.stack([full[0][:POOL_EDGE, tq - POOL_EDGE:], full[2][tq - POOL_EDGE:, :POOL_EDGE]]))
    return (jnp.asarray(np.stack(main), BF16),
            jnp.asarray(np.stack(edge), BF16))


def _odd_mixer_call(qkv, ckv, rpb, du, pool_w, pool_scale):
    b, s, n = qkv.shape
    n_rows = s // GRID_W
    _, slab, n_blk, _, type_of = _na_geometry(n_rows)
    tq = NA_QROWS * GRID_W
    bias = _na_bias_table(rpb, n_rows)
    bands, edges = _pool_bands(tq)

    def bias_map(bi, i):
        t = jnp.int32(type_of[-1])
        for blk in range(n_blk - 2, -1, -1):
            t = jnp.where(i == blk, jnp.int32(type_of[blk]), t)
        return (0, t, 0, 0)

    kern = functools.partial(_odd_mixer_kernel, tq=tq, slab_rows=slab, n_rows=n_rows, n_blk=n_blk, seq=s)
    return pl.pallas_call(
        kern,
        out_shape=jax.ShapeDtypeStruct((b, s, 1024), BF16),
        grid=(b, n_blk),
        in_specs=[pl.BlockSpec((1, s, n), lambda bi, i: (bi, 0, 0)),
                  pl.BlockSpec((1,) + ckv.shape[1:], lambda bi, i: (bi, 0, 0)),
                  pl.BlockSpec((bias.shape[0], None) + bias.shape[2:], bias_map),
                  pl.BlockSpec((1, s, du.shape[2]), lambda bi, i: (bi, 0, 0)),
                  pl.BlockSpec(bands.shape, lambda bi, i: (0, 0, 0)),
                  pl.BlockSpec(edges.shape, lambda bi, i: (0, 0, 0, 0)),
                  pl.BlockSpec(pool_w.shape, lambda bi, i: (0, 0, 0)),
                  pl.BlockSpec(pool_scale.shape, lambda bi, i: (0, 0))],
        out_specs=pl.BlockSpec((1, tq, 1024), lambda bi, i: (bi, i, 0)),
        compiler_params=_params(("arbitrary", "arbitrary")),
        name="odd_mixer",
    )(qkv, ckv, bias, du, bands, edges, pool_w, pool_scale)


OUTPROJ_SUB = 256


def _outproj_kernel(mix_ref, w_ref, x_ref, g1_ref, lng_ref, lnb_ref, sc_ref, sh_ref, r2_ref, rh_ref,
                    x1_ref, hm_ref, aff_ref, *, n_exp):
    tm = x_ref.shape[1]
    subs = [slice(r0, r0 + OUTPROJ_SUB) for r0 in range(0, tm, OUTPROJ_SUB)]
    ys = [_dot(mix_ref[0, rows, :], w_ref[...]) for rows in subs]
    for rows, y in zip(subs, ys):
        x1 = _layer_norm(DEEPNORM_ALPHA * x_ref[0, rows, :] + g1_ref[0] * y, lng_ref[...], lnb_ref[...])
        x1_ref[0, rows, :] = x1
        hm = x1 * (1.0 + sc_ref[0]) + sh_ref[0]
        hm_hi = hm.astype(BF16)
        hm_ref[0, rows, :] = hm_hi
        hm_lo = (hm - hm_hi.astype(F32)).astype(BF16)
        part = _dot(hm_hi, r2_ref[...])
        logits = part[:, :n_exp] + part[:, n_exp:] + _dot(hm_lo, rh_ref[...])
        ex = jnp.exp(logits - logits.max(axis=-1, keepdims=True))
        aff_ref[0, rows, :] = ex * (1.0 / ex.sum(axis=-1, keepdims=True))


def _outproj_call(mix, w, x, g1, lng, lnb, sc2, sh2, router, name):
    b, s, d = x.shape
    dm = mix.shape[2]
    e = router.shape[1]
    tm = min(4 * OUTPROJ_SUB, s)
    r_hi = router.astype(BF16)
    r_lo = (router - r_hi.astype(F32)).astype(BF16)
    r2 = jnp.concatenate([r_hi, r_lo], axis=1)
    row = pl.BlockSpec((1, d), lambda bi, i: (0, 0))
    tile = lambda n: pl.BlockSpec((1, tm, n), lambda bi, i: (bi, i, 0))
    return pl.pallas_call(
        functools.partial(_outproj_kernel, n_exp=e),
        out_shape=[jax.ShapeDtypeStruct((b, s, d), F32),
                   jax.ShapeDtypeStruct((b, s, d), BF16),
                   jax.ShapeDtypeStruct((b, s, e), F32)],
        grid=(b, s // tm),
        in_specs=[tile(dm),
                  pl.BlockSpec((dm, d), lambda bi, i: (0, 0)),
                  tile(d),
                  g1.spec(), row, row, sc2.spec(), sh2.spec(),
                  pl.BlockSpec((d, 2 * e), lambda bi, i: (0, 0)),
                  pl.BlockSpec((d, e), lambda bi, i: (0, 0))],
        out_specs=[tile(d), tile(d), tile(e)],
        compiler_params=_params(("arbitrary", "arbitrary")),
        name=name,
    )(mix, w, x, g1.arr, lng, lnb, sc2.arr, sh2.arr, r2, r_hi)


def _lane_cumsum(m):
    rows, s = m.shape
    r_i = lax.broadcasted_iota(jnp.int32, (LANES, LANES), 0)
    c_i = lax.broadcasted_iota(jnp.int32, (LANES, LANES), 1)
    tri = jnp.where(r_i <= c_i, 1.0, 0.0).astype(BF16)
    carry = jnp.zeros((rows, 1), F32)
    out = []
    for c in range(s // LANES):
        blk = m[:, c * LANES:(c + 1) * LANES]
        out.append(_dot(blk.astype(BF16), tri) + carry)
        carry = carry + blk.sum(axis=-1, keepdims=True)
    return jnp.concatenate(out, axis=-1)


ROUTE_TILE = 256


def _route_kernel(aff_ref, pos_ref, cnt_ref, *, cap):
    a = aff_ref[...]
    thr = jnp.zeros((a.shape[0], 1), jnp.int32)
    for bit in range(30, -1, -1):
        cand = thr | jnp.int32(1 << bit)
        cnt = jnp.where(a >= pltpu.bitcast(cand, F32), 1.0, 0.0).sum(axis=-1, keepdims=True)
        thr = jnp.where(cnt >= cap, cand, thr)
    gt = jnp.where(a >= pltpu.bitcast(thr + 1, F32), 1.0, 0.0)
    eq = jnp.where(a >= pltpu.bitcast(thr, F32), 1.0, 0.0) - gt
    need = cap - gt.sum(axis=-1, keepdims=True)
    sel = gt + eq * jnp.where(_lane_cumsum(eq) <= need, 1.0, 0.0)
    pos_ref[...] = jnp.where(sel > 0.5, _lane_cumsum(sel) - 1.0, -1.0)
    tok = lax.broadcasted_iota(jnp.int32, (a.shape[1], LANES), 0)
    tile = lax.broadcasted_iota(jnp.int32, (a.shape[1], LANES), 1)
    before = jnp.where(tok < tile * ROUTE_TILE, 1.0, 0.0).astype(BF16)
    cnt_ref[...] = _dot(sel.astype(BF16), before)


def _route_call(aff, cap, name):
    b, e, s = aff.shape
    n_tiles = s // min(ROUTE_TILE, s)
    full = pl.BlockSpec((b * e, s), lambda i: (0, 0))
    pos, cnt = pl.pallas_call(
        functools.partial(_route_kernel, cap=cap),
        out_shape=[jax.ShapeDtypeStruct((b * e, s), F32), jax.ShapeDtypeStruct((b * e, LANES), F32)],
        grid=(1,),
        in_specs=[full],
        out_specs=[full, pl.BlockSpec((b * e, LANES), lambda i: (0, 0))],
        compiler_params=_params(("arbitrary",)),
        name=name,
    )(aff.reshape(b * e, s))
    return pos.reshape(b, e, s), cnt[:, :n_tiles + 1].astype(jnp.int32).reshape(-1)


def _gather_kernel(cnt_ref, pos_ref, aff_ref, hm_ref, xg_ref, gs_ref, *, cap, n_exp, win, n_tiles):
    bi, i = pl.program_id(0), pl.program_id(1)

    @pl.when(i == 0)
    def _():
        xg_ref[...] = jnp.zeros(xg_ref.shape, xg_ref.dtype)
        gs_ref[...] = jnp.zeros(gs_ref.shape, gs_ref.dtype)

    starts, short = [], None
    for ei in range(n_exp):
        base = (bi * n_exp + ei) * (n_tiles + 1) + i
        st = jnp.minimum((cnt_ref[base] // SLOT_ALIGN) * SLOT_ALIGN, cap - win)
        starts.append(pl.multiple_of(st, SLOT_ALIGN))
        miss = cnt_ref[base + 1] > st + win
        short = miss if short is None else jnp.logical_or(short, miss)

    def scatter_rows(n_rows, first, stacked):
        row = lax.broadcasted_iota(jnp.int32, (n_rows, 1), 0).astype(F32)
        blocks = []
        for ei in range(n_exp):
            rel = pos_ref[0, ei:ei + 1, :] if first is None else pos_ref[0, ei:ei + 1, :] - first[ei].astype(F32)
            hit = rel == row
            blocks.append(jnp.where(hit, 1.0, 0.0).astype(BF16))
            gate = jnp.where(hit, aff_ref[0, ei:ei + 1, :], 0.0).sum(axis=-1, keepdims=True)
            rows = slice(None) if first is None else pl.ds(first[ei], n_rows)
            gs_ref[0, ei, rows, :] += gate
        if stacked:
            part = _dot(jnp.concatenate(blocks, axis=0), hm_ref[0])
        for ei in range(n_exp):
            rows = slice(None) if first is None else pl.ds(first[ei], n_rows)
            sub = part[ei * n_rows:(ei + 1) * n_rows] if stacked else _dot(blocks[ei], hm_ref[0])
            xg_ref[0, ei, rows, :] += sub.astype(xg_ref.dtype)

    @pl.when(jnp.logical_not(short))
    def _():
        scatter_rows(win, starts, True)

    @pl.when(short)
    def _():
        scatter_rows(cap, None, False)


def _gather_call(cnt, pos, aff, hm, cap, name):
    b, e, s = pos.shape
    d = hm.shape[2]
    tm = min(ROUTE_TILE, s)
    n_tiles = s // tm
    win = min(64, cap)
    tok = pl.BlockSpec((1, e, tm), lambda bi, i, c: (bi, 0, i))
    return pl.pallas_call(
        functools.partial(_gather_kernel, cap=cap, n_exp=e, win=win, n_tiles=n_tiles),
        out_shape=[jax.ShapeDtypeStruct((b, e, cap, d), BF16),
                   jax.ShapeDtypeStruct((b, e, cap, 1), F32)],
        grid_spec=pltpu.PrefetchScalarGridSpec(
            num_scalar_prefetch=1,
            grid=(b, n_tiles),
            in_specs=[tok, tok, pl.BlockSpec((1, tm, d), lambda bi, i, c: (bi, i, 0))],
            out_specs=[pl.BlockSpec((1, e, cap, d), lambda bi, i, c: (bi, 0, 0, 0)),
                       pl.BlockSpec((1, e, cap, 1), lambda bi, i, c: (bi, 0, 0, 0))]),
        compiler_params=_params(("arbitrary", "arbitrary")),
        name=name,
    )(cnt, pos, aff, hm)


def _ffn_kernel(*refs, n_grp, row_chunks, n_steps):
    xg_refs = refs[:n_grp]
    gs_refs = refs[n_grp:2 * n_grp]
    wg_ref, wu_ref, wd_ref = refs[2 * n_grp:2 * n_grp + 3]
    y_refs = refs[2 * n_grp + 3:3 * n_grp + 3]
    acc_refs = refs[3 * n_grp + 3:]
    f = pl.program_id(1)
    last = pl.num_programs(1) - 1
    wg = wg_ref[0].astype(BF16)

    def body(first, final):
        wu = wd = None
        for xg_ref, gs_ref, y_ref, acc_ref, nb in zip(xg_refs, gs_refs, y_refs, acc_refs, row_chunks):
            bt, _, cap, d = xg_ref.shape
            for b0 in range(0, bt, nb):
                rows = nb * cap
                r0 = b0 * cap
                x = xg_ref[b0:b0 + nb, 0].reshape(rows, d)
                hg = _dot(x, wg)
                if wu is None:
                    wu = wu_ref[0].astype(BF16)
                hu = _dot(x, wu)
                if wd is None:
                    wd = wd_ref[0].astype(BF16)
                hid = (hg * (1.0 / (1.0 + jnp.exp(-hg))) * hu).astype(BF16)
                part = _dot(hid, wd)
                if not first:
                    part = acc_ref[r0:r0 + rows, :] + part
                if final:
                    gate = gs_ref[b0:b0 + nb, 0].reshape(rows, 1)
                    y_ref[b0:b0 + nb, 0] = (part * gate).reshape(nb, cap, d).astype(y_ref.dtype)
                else:
                    acc_ref[r0:r0 + rows, :] = part

    if n_steps == 1:
        body(True, True)
    else:
        pl.when(f == 0)(functools.partial(body, True, False))
        pl.when(jnp.logical_and(f > 0, f < last))(functools.partial(body, False, False))
        pl.when(f == last)(functools.partial(body, False, True))


def _ffn_call(xgs, gss, w_gate, w_up, w_down, layer, name):
    _, e, d, ff = w_gate.shape
    tf = 512 if ff % 512 == 0 else ff
    n_grp = len(xgs)
    row_chunks = []
    for xg in xgs:
        bt, _, cap, _ = xg.shape
        nb = max(1, min(bt, 512 // cap))
        while bt % nb:
            nb -= 1
        row_chunks.append(nb)
    tok = lambda a: pl.BlockSpec((a.shape[0], 1) + a.shape[2:], lambda ei, fi: (0, ei, 0, 0))
    kern = functools.partial(_ffn_kernel, n_grp=n_grp, row_chunks=tuple(row_chunks), n_steps=ff // tf)
    return pl.pallas_call(
        kern,
        out_shape=[jax.ShapeDtypeStruct(xg.shape, BF16) for xg in xgs],
        grid=(e, ff // tf),
        in_specs=[tok(a) for a in xgs] + [tok(a) for a in gss]
                 + [pl.BlockSpec((None, 1, d, tf), lambda ei, fi: (layer, ei, 0, fi)),
                    pl.BlockSpec((None, 1, d, tf), lambda ei, fi: (layer, ei, 0, fi)),
                    pl.BlockSpec((None, 1, tf, d), lambda ei, fi: (layer, ei, fi, 0))],
        out_specs=[tok(a) for a in xgs],
        scratch_shapes=[pltpu.VMEM((xg.shape[0] * xg.shape[2], d), F32) for xg in xgs],
        compiler_params=_params(("arbitrary", "arbitrary")),
        name=name,
    )(*xgs, *gss, w_gate, w_up, w_down)


MXU_DEPTH = 256
SLOT_ALIGN = 16


def _combine_kernel(cnt_ref, pos_ref, y_ref, x_ref, g_ref, lng_ref, lnb_ref, *rest, cap, n_exp, win, n_tiles,
                    n_sub, proj_cfg=None):
    if proj_cfg is None:
        (o_ref,), proj_refs = rest, ()
    else:
        proj_refs, o_ref = rest[:3] + rest[4:], rest[3]
    bi, i = pl.program_id(0), pl.program_id(1)
    sub = pos_ref.shape[1] // n_sub
    grp = MXU_DEPTH // win
    starts, short = [], None
    for r in range(n_sub):
        starts.append([])
        for ei in range(n_exp):
            base = (bi * n_exp + ei) * (n_tiles + 1) + i * n_sub + r
            st = jnp.minimum((cnt_ref[base] // SLOT_ALIGN) * SLOT_ALIGN, cap - win)
            starts[r].append(st)
            miss = cnt_ref[base + 1] > st + win
            short = miss if short is None else jnp.logical_or(short, miss)

    def windowed(r):
        pos = pos_ref[0, r * sub:(r + 1) * sub, :]
        lane = lax.broadcasted_iota(jnp.int32, (1, grp * win), 1)
        lane_f = lane.astype(F32)
        acc = None
        for k in range(n_exp // grp):
            tgt, rows = None, []
            for u in range(grp - 1, -1, -1):
                ei = k * grp + u
                st = starts[r][ei]
                rel = pos[:, ei:ei + 1] - st.astype(F32)
                rel = jnp.where(rel >= 0.0, jnp.where(rel < win, rel + float(u * win), -1.0), -1.0)
                tgt = rel if tgt is None else jnp.where(lane < (u + 1) * win, rel, tgt)
                rows.insert(0, y_ref[0, pl.ds(pl.multiple_of(ei * cap + st, SLOT_ALIGN), win), :])
            onehot = jnp.where(tgt == lane_f, 1.0, 0.0).astype(BF16)
            part = _dot(onehot, jnp.concatenate(rows, axis=0))
            acc = part if acc is None else acc + part
        return acc

    def dense(r):
        pos = pos_ref[0, r * sub:(r + 1) * sub, :]
        slot = lax.broadcasted_iota(jnp.int32, (1, cap), 1).astype(F32)
        acc = None
        for ei in range(n_exp):
            onehot = jnp.where(pos[:, ei:ei + 1] == slot, 1.0, 0.0).astype(BF16)
            part = _dot(onehot, y_ref[0, ei * cap:(ei + 1) * cap, :])
            acc = part if acc is None else acc + part
        return acc

    def epilogue(r, acc):
        rows = slice(r * sub, (r + 1) * sub)
        x_new = _layer_norm(DEEPNORM_ALPHA * x_ref[0, rows, :] + g_ref[0] * acc, lng_ref[...], lnb_ref[...])
        o_ref[0, rows, :] = x_new
        if proj_cfg is not None:
            sc_ref, sh_ref, w_ref = proj_refs[:3]
            h = (x_new * (1.0 + sc_ref[0]) + sh_ref[0]).astype(BF16)
            _inproj_body(h, w_ref, None, None, None, proj_refs[3:], rows=rows, **proj_cfg)

    @pl.when(jnp.logical_not(short))
    def _():
        accs = [windowed(r) for r in range(n_sub)]
        for r in range(n_sub):
            epilogue(r, accs[r])

    @pl.when(short)
    def _():
        for r in range(n_sub):
            epilogue(r, dense(r))


def _combine_call(cnt, pos_t, y, x, g2, lng, lnb, cap, name, proj=None):
    b, s, d = x.shape
    e = pos_t.shape[2]
    n_tiles = s // min(ROUTE_TILE, s)
    n_sub = 2 if n_tiles % 2 == 0 else 1
    tm = s // (n_tiles // n_sub)
    win = min(64, cap)
    row = pl.BlockSpec((1, d), lambda bi, i, c: (0, 0))
    tile = lambda n: pl.BlockSpec((1, tm, n), lambda bi, i, c: (bi, i, 0))
    in_specs = [tile(e), pl.BlockSpec((1, e * cap, d), lambda bi, i, c: (bi, 0, 0)), tile(d), g2.spec(), row, row]
    operands = [cnt, pos_t, y.reshape(b, e * cap, d), x, g2.arr, lng, lnb]
    out_shape = [jax.ShapeDtypeStruct((b, s, d), F32)]
    out_specs = [tile(d)]
    proj_cfg = None
    if proj is not None:
        sc, sh, w, blocks, out_defs, ones_cols = proj
        in_specs += [sc.spec(), sh.spec(), pl.BlockSpec(w.shape, lambda bi, i, c: (0, 0))]
        operands += [sc.arr, sh.arr, w]
        out_shape += [jax.ShapeDtypeStruct((b, s, nc), dt) for nc, dt in out_defs]
        out_specs += [tile(nc) for nc, _ in out_defs]
        proj_cfg = dict(blocks=tuple(blocks), ones_cols=tuple(ones_cols), chunk=512, use_rope=False)
    res = pl.pallas_call(
        functools.partial(_combine_kernel, cap=cap, n_exp=e, win=win, n_tiles=n_tiles, n_sub=n_sub,
                          proj_cfg=proj_cfg),
        out_shape=out_shape,
        grid_spec=pltpu.PrefetchScalarGridSpec(
            num_scalar_prefetch=1, grid=(b, n_tiles // n_sub), in_specs=in_specs, out_specs=out_specs),
        compiler_params=_params(("arbitrary", "arbitrary")),
        name=name,
    )(*operands)
    return res[0] if proj is None else res


def _ffn_sublayer(streams, projs, lng, lnb, router, w_gate, w_up, w_down, layer):
    staged = []
    for si, (mix, w_out, x, g1, sc2, sh2, g2) in enumerate(streams):
        tag = f"l{layer}s{si}"
        n_tok = x.shape[1]
        cap = EC_FACTOR * n_tok // N_EXPERTS
        x1, hm, aff_t = _outproj_call(mix, w_out, x, g1, lng[0:1], lnb[0:1], sc2, sh2, router, "outproj_" + tag)
        aff = jnp.swapaxes(aff_t, 1, 2)
        pos, cnt = _route_call(aff, cap, "route_" + tag)
        xg, gs = _gather_call(cnt, pos, aff, hm, cap, "gather_" + tag)
        staged.append((x1, pos, cnt, xg, gs, g2, cap, tag))
    ys = _ffn_call([st[3] for st in staged], [st[4] for st in staged], w_gate, w_up, w_down, layer,
                   f"ffn_l{layer}")
    outs = []
    for (x1, pos, cnt, _, _, g2, cap, tag), y, proj in zip(staged, ys, projs):
        outs.append(_combine_call(cnt, jnp.swapaxes(pos, 1, 2), y, x1, g2, lng[1:2], lnb[1:2], cap,
                                  "combine_" + tag, proj))
    return outs


def kernel(x, c, ctx, c_ctx, ada_w, ada_b, ln_g, ln_b, l0_w_in, l0_w_out, l0_lam_q1, l0_lam_k1, l0_lam_q2,
           l0_lam_k2, l0_subln_g, l0_qnorm_g, l0_knorm_g, l1_w_in, l1_w_out, l1_rpb, l1_pool_w, l1_pool_scale,
           moe_router, moe_w_gate, moe_w_up, moe_w_down):
    b, s, d = x.shape
    n_ctx = ctx.shape[1]

    rows = -(-(b + 1) // 8) * 8
    cc = jnp.concatenate([c, c_ctx[None], jnp.zeros((rows - b - 1, d), F32)], axis=0)
    mod = _ada_call(cc, ada_w, ada_b)

    mod5 = mod.reshape(DEPTH, rows, 6, 1, d)

    def mods(i):
        return ([_Mod(mod5, i, k, None) for k in range(6)], [_Mod(mod5, i, k, b) for k in range(6)])

    rope = _rope_tables(s)
    rope_id = (jnp.ones((n_ctx, LANES), F32), jnp.zeros((n_ctx, LANES), F32), jnp.zeros((n_ctx, LANES), F32))
    gmat = jnp.asarray(np.where((np.arange(LANES)[:, None] // HEAD_DIM) == (np.arange(LANES)[None, :] // HEAD_DIM),
                                1.0 / HEAD_DIM, 0.0), BF16)
    tile2 = lambda g: jnp.concatenate([g, g]).reshape(1, LANES)

    (sh1, sc1, g1, sh2, sc2, g2), (csh1, csc1, cg1, csh2, csc2, cg2) = mods(0)
    bq0 = 3 * 512
    pair_heads = [hh for j in range(B_HEADS // 2) for hh in (j, j + B_HEADS // 2)]
    w_in_b, w_out_b = l0_w_in.astype(BF16), l0_w_out.astype(BF16)
    w_in0 = jnp.concatenate([w_in_b[:, :bq0]]
                            + [w_in_b[:, bq0 + hh * HEAD_DIM:bq0 + (hh + 1) * HEAD_DIM] for hh in pair_heads]
                            + [w_in_b[:, bq0 + 512:]], axis=1)
    w_out0 = jnp.concatenate([w_out_b[:512]]
                             + [w_out_b[512 + hh * HEAD_DIM:512 + (hh + 1) * HEAD_DIM] for hh in pair_heads], axis=0)
    gains0 = jnp.concatenate([tile2(l0_qnorm_g), tile2(l0_knorm_g)], axis=0)
    q_exp2 = Q_SCALE * math.log2(math.e)
    blocks0 = ([("rope", 0, q_exp2, 0, EV_AQ + k * LANES) for k in range(4)]
               + [("rope", 0, 1.0, 0, EV_AK + k * LANES) for k in range(4)]
               + [("plain", 0, 1.0, 0, EV_AV + k * 2 * LANES) for k in range(4)]
               + [("norm", 0, q_exp2, 0, EV_BQ + k * LANES) for k in range(4)]
               + [("norm", 1, 1.0, 0, EV_BK), ("plain", 0, 1.0, 0, EV_BV)])
    ones0 = [(0, EV_AV + (2 * k + 1) * LANES) for k in range(4)] + [(0, EV_BV + LANES)]
    qkv = _inproj_call(x, sc1, sh1, w_in0, blocks0, [(EV_WIDTH, BF16)], rope, gains0, gmat,
                       use_rope=True, name="inproj_l0", ones_cols=ones0)[0]
    qkv_c = _inproj_call(ctx, csc1, csh1, w_in0, blocks0, [(EV_WIDTH, BF16)], rope_id, gains0, gmat,
                         use_rope=False, name="inproj_l0c", ones_cols=ones0)[0]
    lam_init = 0.8 - 0.6 * math.exp(-0.3 * 0)
    lamv = jnp.stack([l0_lam_q1, l0_lam_k1, l0_lam_q2, l0_lam_k2], axis=0)
    sub_g = l0_subln_g.reshape(1, LANES)
    mix = _attn_even_call(qkv, [qkv, qkv_c], lamv, sub_g, lam_init, "attn_l0")
    mix_c = _attn_even_call(qkv_c, [qkv_c], lamv, sub_g, lam_init, "attn_l0c")
    (sh1n, sc1n, g1n, sh2n, sc2n, g2n), (csh1n, csc1n, _, _, _, _) = mods(1)
    cw = C_HEADS * HEAD_DIM
    w_in1 = l1_w_in.astype(BF16)
    blocks1 = ([("plain", 0, q_exp2, 0, k * LANES) for k in range(4)]
               + [("plain", 0, 1.0, 0, cw + k * LANES) for k in range(4)]
               + [("plain", 0, 1.0, 0, 2 * cw + 2 * k * LANES) for k in range(4)]
               + [("plain", 0, 1.0, 1, k * LANES) for k in range(4)])
    ones1 = [(0, 2 * cw + (2 * k + 1) * LANES) for k in range(4)]
    blocks1c = ([("plain", 0, 1.0, 0, k * LANES) for k in range(4)]
                + [("plain", 0, 1.0, 0, cw + 2 * k * LANES) for k in range(4)])
    ones1c = [(0, cw + (2 * k + 1) * LANES) for k in range(4)]
    proj_lat = (sc1n, sh1n, w_in1, blocks1, [(4 * cw, BF16), (D_GROUPS * D_GROUP_DIM, F32)], ones1)
    proj_ctx = (csc1n, csh1n, w_in1[:, cw:3 * cw], blocks1c, [(3 * cw, BF16)], ones1c)
    (x, qkv1, du), (ctx, ckv) = _ffn_sublayer(
        [(mix, w_out0, x, g1, sc2, sh2, g2), (mix_c, w_out0, ctx, cg1, csc2, csh2, cg2)],
        [proj_lat, proj_ctx], ln_g[0], ln_b[0], moe_router[0], moe_w_gate, moe_w_up, moe_w_down, 0)

    mix1 = _odd_mixer_call(qkv1, ckv, l1_rpb * math.log2(math.e), du, l1_pool_w.astype(BF16),
                           l1_pool_scale.reshape(1, -1))
    (x,) = _ffn_sublayer([(mix1, l1_w_out.astype(BF16), x, g1n, sc2n, sh2n, g2n)],
                         [None], ln_g[1], ln_b[1], moe_router[1], moe_w_gate, moe_w_up, moe_w_down, 1)
    return x
```

```python
import collections
import functools
import math

import numpy as np

import jax
import jax.numpy as jnp
from jax import lax
from jax.experimental import pallas as pl
from jax.experimental.pallas import tpu as pltpu

F32 = jnp.float32
BF16 = jnp.bfloat16

DEPTH = 2
GRID_W = 64
HEAD_DIM = 64
A_HEADS = 4
B_HEADS = 8
B_KV_HEADS = 2
C_HEADS = 8
D_GROUPS = 4
D_GROUP_DIM = 128
POOL_WINDOWS = (2, 4, 8, 16)
NA_ROWS = 8
NA_COLS = 16
N_EXPERTS = 16
EC_FACTOR = 2
ROPE_THETA = 10000.0
LN_EPS = 1e-5
RMS_EPS = 1e-6
DEEPNORM_ALPHA = (2 * DEPTH) ** 0.25
Q_SCALE = HEAD_DIM ** -0.5

LANES = 128
VMEM_LIMIT = 56 * 1024 * 1024

NA_QROWS = 4
NEG_BIG = -1e30

_NN = (((1,), (0,)), ((), ()))
_NT = (((1,), (1,)), ((), ()))


def _dot(a, b, dims=_NN):
    return lax.dot_general(a, b, dims, preferred_element_type=F32)


def _split_bf16(a):
    hi = a.astype(BF16)
    lo = (a - hi.astype(F32)).astype(BF16)
    return hi, lo


def _dot3(a, b, dims=_NN):
    a_hi, a_lo = _split_bf16(a)
    b_hi, b_lo = _split_bf16(b)
    return _dot(a_hi, b_hi, dims) + (_dot(a_hi, b_lo, dims) + _dot(a_lo, b_hi, dims))


class _Mod(collections.namedtuple("_Mod", "arr layer k row")):
    def spec(self):
        d = self.arr.shape[-1]

        def index_map(*grid):
            return (self.layer, grid[0] if self.row is None else self.row, self.k, 0, 0)

        return pl.BlockSpec((None, None, 1, 1, d), index_map)


def _params(sem):
    return pltpu.CompilerParams(dimension_semantics=sem, vmem_limit_bytes=VMEM_LIMIT)


def _layer_norm(z, g, b):
    mu = jnp.mean(z, axis=-1, keepdims=True)
    zc = z - mu
    var = jnp.mean(zc * zc, axis=-1, keepdims=True)
    return zc * lax.rsqrt(var + LN_EPS) * g + b


def _lane_masks():
    lane = lax.broadcasted_iota(jnp.int32, (1, LANES), 1)
    lo = jnp.where(lane < HEAD_DIM, 1.0, 0.0).astype(F32)
    return lo, 1.0 - lo


def _ada_kernel(c_ref, w_ref, b_ref, o_ref):
    c = c_ref[...]
    s = c * (1.0 / (1.0 + jnp.exp(-c)))
    o_ref[0] = _dot3(s, w_ref[0]) + b_ref[0]


def _ada_call(cc, ada_w, ada_b):
    depth, d, n = ada_w.shape
    rows = cc.shape[0]
    tn = 1536 if n % 1536 == 0 else n
    return pl.pallas_call(
        _ada_kernel,
        out_shape=jax.ShapeDtypeStruct((depth, rows, n), F32),
        grid=(depth, n // tn),
        in_specs=[pl.BlockSpec((rows, d), lambda l, j: (0, 0)),
                  pl.BlockSpec((1, d, tn), lambda l, j: (l, 0, j)),
                  pl.BlockSpec((1, 1, tn), lambda l, j: (l, 0, j))],
        out_specs=pl.BlockSpec((1, rows, tn), lambda l, j: (l, 0, j)),
        compiler_params=_params(("arbitrary", "arbitrary")),
        name="ada_mod",
    )(cc, ada_w, ada_b.reshape(depth, 1, n))


def _inproj_body(h, w_ref, rope_refs, gn_ref, gmat_ref, o_refs, *, blocks, ones_cols, chunk, use_rope,
                 rows=slice(None)):
    n = len(blocks) * LANES
    for oi, oc in ones_cols:
        o_refs[oi][0, rows, oc:oc + LANES] = jnp.ones((h.shape[0], LANES), o_refs[oi].dtype)
    starts = list(range(0, n, chunk))
    accs = {starts[0]: _dot(h, w_ref[:, starts[0]:min(starts[0] + chunk, n)])}
    for ci, c0 in enumerate(starts):
        cw = min(chunk, n - c0)
        if ci + 1 < len(starts):
            nxt = starts[ci + 1]
            accs[nxt] = _dot(h, w_ref[:, nxt:min(nxt + chunk, n)])
        acc = accs.pop(c0)
        for j in range(cw // LANES):
            kind, gain_row, factor, oi, oc = blocks[(c0 // LANES) + j]
            v = acc[:, j * LANES:(j + 1) * LANES]
            if kind == "norm":
                v2 = v * v
                hi, lo = _split_bf16(v2)
                ms = _dot(hi, gmat_ref[...]) + _dot(lo, gmat_ref[...])
                v = v * lax.rsqrt(ms + RMS_EPS) * gn_ref[gain_row:gain_row + 1, :]
            if kind in ("rope", "norm") and use_rope:
                cos_ref, sinp_ref, sinm_ref = rope_refs
                v = (v * cos_ref[...] + pltpu.roll(v, 16, 1) * sinp_ref[...]
                     + pltpu.roll(v, LANES - 16, 1) * sinm_ref[...])
            if factor != 1.0:
                v = v * factor
            o_refs[oi][0, rows, oc:oc + LANES] = v.astype(o_refs[oi].dtype)


def _inproj_kernel(x_ref, sc_ref, sh_ref, w_ref, cos_ref, sinp_ref, sinm_ref, gn_ref, gmat_ref, *o_refs, **cfg):
    h = (x_ref[0] * (1.0 + sc_ref[0]) + sh_ref[0]).astype(BF16)
    _inproj_body(h, w_ref, (cos_ref, sinp_ref, sinm_ref), gn_ref, gmat_ref, o_refs, **cfg)


def _inproj_call(x, sc, sh, w, blocks, out_defs, rope_tabs, gains, gmat, *, use_rope, name, ones_cols=()):
    b, s, d = x.shape
    n = w.shape[1]
    tm = min(512, s)
    cos, sinp, sinm = rope_tabs
    tab_spec = pl.BlockSpec((tm, LANES), lambda bi, i: (i, 0))
    kern = functools.partial(_inproj_kernel, blocks=tuple(blocks), ones_cols=tuple(ones_cols), chunk=512,
                             use_rope=use_rope)
    return pl.pallas_call(
        kern,
        out_shape=[jax.ShapeDtypeStruct((b, s, nc), dt) for nc, dt in out_defs],
        grid=(b, s // tm),
        in_specs=[pl.BlockSpec((1, tm, d), lambda bi, i: (bi, i, 0)),
                  sc.spec(), sh.spec(),
                  pl.BlockSpec((d, n), lambda bi, i: (0, 0)),
                  tab_spec, tab_spec, tab_spec,
                  pl.BlockSpec(gains.shape, lambda bi, i: (0, 0)),
                  pl.BlockSpec(gmat.shape, lambda bi, i: (0, 0))],
        out_specs=[pl.BlockSpec((1, tm, nc), lambda bi, i: (bi, i, 0)) for nc, _ in out_defs],
        compiler_params=_params(("arbitrary", "arbitrary")),
        name=name,
    )(x, sc.arr, sh.arr, w, cos, sinp, sinm, gains, gmat)


def _rope_tables(s):
    n_freq = HEAD_DIM // 4
    t = np.arange(s)
    inv = ROPE_THETA ** (-np.arange(n_freq, dtype=np.float64) / n_freq)
    ang_r = (t // GRID_W)[:, None] * inv
    ang_c = (t % GRID_W)[:, None] * inv
    ang = np.concatenate([ang_r, ang_r, ang_c, ang_c] * (LANES // HEAD_DIM), axis=-1)
    first = (np.arange(LANES) % 32) < 16
    cos, sin = np.cos(ang), np.sin(ang)
    tabs = (cos, np.where(first, 0.0, sin), np.where(first, -sin, 0.0))
    return tuple(jnp.asarray(a.astype(np.float32)) for a in tabs)


EV_AQ, EV_AK, EV_AV = 0, 512, 1024
EV_BQ = EV_AV + A_HEADS * 2 * LANES
EV_BK = EV_BQ + (B_HEADS // 2) * LANES
EV_BV = EV_BK + LANES
EV_WIDTH = EV_BV + 2 * LANES


def _attn_even_kernel(q_ref, lam_ref, sg_ref, *refs, n_kv, tq, lam_init):
    kv_refs, o_ref = refs[:n_kv], refs[n_kv]
    lo, hi = _lane_masks()
    lo_b, hi_b = lo.astype(BF16), hi.astype(BF16)
    lv = lam_ref[...]
    lam = (jnp.exp(jnp.sum(lv[0:1] * lv[1:2], axis=-1, keepdims=True))
           - jnp.exp(jnp.sum(lv[2:3] * lv[3:4], axis=-1, keepdims=True)) + lam_init)

    def scores(q, kcol):
        qq = jnp.concatenate([q * lo_b, q * hi_b], axis=0)
        ss = [_dot(qq, kv[0, :, kcol:kcol + LANES], _NT) for kv in kv_refs]
        m = ss[0].max(axis=-1, keepdims=True)
        for s in ss[1:]:
            m = jnp.maximum(m, s.max(axis=-1, keepdims=True))
        return ss, m

    def weighted(ss, m, vcol):
        acc = None
        for s, kv in zip(ss, kv_refs):
            part = _dot(jnp.exp2(s - m).astype(BF16), kv[0, :, vcol:vcol + 2 * LANES])
            acc = part if acc is None else acc + part
        return acc[:, :LANES] * (1.0 / acc[:, LANES:])

    units = ([("a", h, EV_AQ + h * LANES, EV_AK + h * LANES, EV_AV + h * 2 * LANES) for h in range(A_HEADS)]
             + [("b", j, EV_BQ + j * LANES, EV_BK, EV_BV) for j in range(B_HEADS // 2)])

    def finish(unit, ss, m):
        kind, idx, _, _, vcol = unit
        on = weighted(ss, m, vcol)
        if kind == "a":
            o = on[:tq] - lam * on[tq:]
            ms = jnp.mean(o * o, axis=-1, keepdims=True)
            o = o * lax.rsqrt(ms + RMS_EPS) * sg_ref[...] * (1.0 - lam_init)
            o_ref[0, :, idx * LANES:(idx + 1) * LANES] = o.astype(o_ref.dtype)
        else:
            o = on[:tq] * lo + on[tq:] * hi
            o_ref[0, :, 512 + idx * LANES:512 + (idx + 1) * LANES] = o.astype(o_ref.dtype)

    pending = None
    for unit in units:
        cur = scores(q_ref[0, :, unit[2]:unit[2] + LANES], unit[3])
        if pending is not None:
            finish(*pending)
        pending = (unit,) + cur
    finish(*pending)


def _attn_even_call(q_arr, kv_arrs, lamv, subln_g, lam_init, name):
    b, sq, n = q_arr.shape
    tq = min(256, sq)
    kern = functools.partial(_attn_even_kernel, n_kv=len(kv_arrs), tq=tq, lam_init=lam_init)
    return pl.pallas_call(
        kern,
        out_shape=jax.ShapeDtypeStruct((b, sq, 1024), BF16),
        grid=(b, sq // tq),
        in_specs=[pl.BlockSpec((1, tq, n), lambda bi, i: (bi, i, 0)),
                  pl.BlockSpec(lamv.shape, lambda bi, i: (0, 0)),
                  pl.BlockSpec(subln_g.shape, lambda bi, i: (0, 0))]
                 + [pl.BlockSpec((1,) + a.shape[1:], lambda bi, i: (bi, 0, 0)) for a in kv_arrs],
        out_specs=pl.BlockSpec((1, tq, 1024), lambda bi, i: (bi, i, 0)),
        compiler_params=_params(("arbitrary", "arbitrary")),
        name=name,
    )(q_arr, lamv, subln_g, *kv_arrs)


def _odd_mixer_kernel(qkv_ref, ckv_ref, bias_ref, du_ref, band_ref, edge_ref, pw_ref, ps_ref, o_ref, *,
                      tq, slab_rows, n_rows, n_blk, seq):
    i = pl.program_id(1)
    lo, hi = _lane_masks()
    lo_b, hi_b = lo.astype(BF16), hi.astype(BF16)
    t0 = pl.multiple_of(i * tq, tq)
    base = jnp.clip(i * NA_QROWS - NA_ROWS // 2, 0, n_rows - slab_rows)
    k0 = pl.multiple_of(base * GRID_W, GRID_W)
    nk = slab_rows * GRID_W
    cw = C_HEADS * HEAD_DIM

    def scores(j):
        cs = slice(j * LANES, (j + 1) * LANES)
        q = qkv_ref[0, pl.ds(t0, tq), cs]
        qq = jnp.concatenate([q * lo_b, q * hi_b], axis=0)
        kl = qkv_ref[0, pl.ds(k0, nk), cw + j * LANES:cw + (j + 1) * LANES]
        bias = jnp.concatenate([bias_ref[2 * j], bias_ref[2 * j + 1]], axis=0)
        s_l = _dot(qq, kl, _NT) + bias
        s_c = _dot(qq, ckv_ref[0, :, cs], _NT)
        m = jnp.maximum(s_l.max(axis=-1, keepdims=True), s_c.max(axis=-1, keepdims=True))
        return j, s_l, s_c, m

    def finish(j, s_l, s_c, m):
        vl = qkv_ref[0, pl.ds(k0, nk), 2 * cw + 2 * j * LANES:2 * cw + 2 * (j + 1) * LANES]
        vc = ckv_ref[0, :, cw + 2 * j * LANES:cw + 2 * (j + 1) * LANES]
        acc = _dot(jnp.exp2(s_l - m).astype(BF16), vl) + _dot(jnp.exp2(s_c - m).astype(BF16), vc)
        on = acc[:, :LANES] * (1.0 / acc[:, LANES:])
        o = on[:tq] * lo + on[tq:] * hi
        o_ref[0, :, j * LANES:(j + 1) * LANES] = o.astype(o_ref.dtype)

    pending = None
    for j in range(C_HEADS // 2):
        cur = scores(j)
        if pending is not None:
            finish(*pending)
        pending = cur
    finish(*pending)

    tprev = pl.multiple_of(jnp.maximum(i - 1, 0) * tq, tq)
    tnext = pl.multiple_of(jnp.minimum(i + 1, n_blk - 1) * tq, tq)
    has_prev = jnp.where(i > 0, 1.0, 0.0).astype(F32)
    has_next = jnp.where(i < n_blk - 1, 1.0, 0.0).astype(F32)
    tpos = t0 + lax.broadcasted_iota(jnp.int32, (tq, 1), 0)
    for g in range(D_GROUPS):
        half = POOL_WINDOWS[g] // 2
        gs = slice(g * D_GROUP_DIM, (g + 1) * D_GROUP_DIM)
        cur = du_ref[0, pl.ds(t0, tq), gs]
        prv = du_ref[0, pl.ds(tprev + (tq - POOL_EDGE), POOL_EDGE), gs]
        nxt = du_ref[0, pl.ds(tnext, POOL_EDGE), gs]

        def band_sum(band, u):
            u_hi, u_lo = _split_bf16(u)
            return _dot(band, u_hi) + _dot(band, u_lo)

        wsum = band_sum(band_ref[g], cur)
        wsum = jnp.concatenate([wsum[:POOL_EDGE] + has_prev * band_sum(edge_ref[g, 0], prv),
                                wsum[POOL_EDGE:tq - POOL_EDGE],
                                wsum[tq - POOL_EDGE:] + has_next * band_sum(edge_ref[g, 1], nxt)], axis=0)
        cnt = (jnp.minimum(tpos + half, seq) - jnp.maximum(tpos - half, 0)).astype(F32)
        pooled = wsum * (1.0 / cnt) - cur
        od = _dot(pooled.astype(BF16), pw_ref[g]) * ps_ref[:, gs]
        o_ref[0, :, cw + g * D_GROUP_DIM:cw + (g + 1) * D_GROUP_DIM] = od.astype(o_ref.dtype)


def _na_geometry(n_rows):
    kh = min(NA_ROWS, n_rows)
    slab = min(n_rows, NA_QROWS + kh)
    n_blk = n_rows // NA_QROWS
    bases = [min(max(i * NA_QROWS - NA_ROWS // 2, 0), n_rows - slab) for i in range(n_blk)]
    sigs, type_of = [], []
    for i in range(n_blk):
        sig = tuple((min(max(i * NA_QROWS + r - kh // 2, 0), n_rows - kh) - bases[i],
                     i * NA_QROWS + r - bases[i]) for r in range(NA_QROWS))
        if sig not in sigs:
            sigs.append(sig)
        type_of.append(sigs.index(sig))
    return kh, slab, n_blk, sigs, type_of


def _na_bias_table(rpb, n_rows):
    kh, slab, _, sigs, _ = _na_geometry(n_rows)
    kw = NA_COLS
    n_dc = 2 * NA_COLS - 1
    w = np.arange(GRID_W)
    cstart = np.clip(w - kw // 2, 0, GRID_W - kw)
    col_ok = (w[None, :] >= cstart[:, None]) & (w[None, :] < cstart[:, None] + kw)
    dc = np.clip(w[None, :] - w[:, None] + (NA_COLS - 1), 0, n_dc - 1)
    heads = rpb.shape[0]
    onehot = ((dc[None] == np.arange(n_dc)[:, None, None]) & col_ok[None]).astype(np.float32)
    toep = jnp.einsum("hrd,dwj->hwrj", rpb, jnp.asarray(onehot), precision=lax.Precision.HIGHEST)
    toep = jnp.where(jnp.asarray(col_ok)[:, None, :], toep, NEG_BIG)
    pad = slab + NA_QROWS
    toep = jnp.pad(toep, ((0, 0), (0, 0), (pad, pad), (0, 0)), constant_values=NEG_BIG)
    toep = toep.reshape(heads, GRID_W, -1)
    blocks = []
    for sig in sigs:
        for rs_rel, qr_rel in sig:
            d0 = pad - qr_rel + (NA_ROWS - 1)
            in_win = np.repeat(np.array([rs_rel <= m < rs_rel + kh for m in range(slab)]), GRID_W)
            blk = lax.slice_in_dim(toep, d0 * GRID_W, (d0 + slab) * GRID_W, axis=2)
            blocks.append(jnp.where(jnp.asarray(in_win), blk, NEG_BIG))
    big = jnp.stack(blocks, axis=1)
    return big.reshape(heads, len(sigs), NA_QROWS * GRID_W, slab * GRID_W)


POOL_EDGE = 16


def _pool_bands(tq):
    t = np.arange(tq)[:, None]
    sidx = np.arange(tq)[None, :]
    main, edge = [], []
    for wdw in POOL_WINDOWS:
        half = wdw // 2
        full = [((sidx + (m - 1) * tq >= t - half) & (sidx + (m - 1) * tq < t + half)).astype(np.float32)
                for m in range(3)]
        main.append(full[1])
        edge.append(np.stack([full[0][:POOL_EDGE, tq - POOL_EDGE:], full[2][tq - POOL_EDGE:, :POOL_EDGE]]))
    return (jnp.asarray(np.stack(main), BF16),
            jnp.asarray(np.stack(edge), BF16))


def _odd_mixer_call(qkv, ckv, rpb, du, pool_w, pool_scale):
    b, s, n = qkv.shape
    n_rows = s // GRID_W
    _, slab, n_blk, _, type_of = _na_geometry(n_rows)
    tq = NA_QROWS * GRID_W
    bias = _na_bias_table(rpb, n_rows)
    bands, edges = _pool_bands(tq)

    def bias_map(bi, i):
        t = jnp.int32(type_of[-1])
        for blk in range(n_blk - 2, -1, -1):
            t = jnp.where(i == blk, jnp.int32(type_of[blk]), t)
        return (0, t, 0, 0)

    kern = functools.partial(_odd_mixer_kernel, tq=tq, slab_rows=slab, n_rows=n_rows, n_blk=n_blk, seq=s)
    return pl.pallas_call(
        kern,
        out_shape=jax.ShapeDtypeStruct((b, s, 1024), BF16),
        grid=(b, n_blk),
        in_specs=[pl.BlockSpec((1, s, n), lambda bi, i: (bi, 0, 0)),
                  pl.BlockSpec((1,) + ckv.shape[1:], lambda bi, i: (bi, 0, 0)),
                  pl.BlockSpec((bias.shape[0], None) + bias.shape[2:], bias_map),
                  pl.BlockSpec((1, s, du.shape[2]), lambda bi, i: (bi, 0, 0)),
                  pl.BlockSpec(bands.shape, lambda bi, i: (0, 0, 0)),
                  pl.BlockSpec(edges.shape, lambda bi, i: (0, 0, 0, 0)),
                  pl.BlockSpec(pool_w.shape, lambda bi, i: (0, 0, 0)),
                  pl.BlockSpec(pool_scale.shape, lambda bi, i: (0, 0))],
        out_specs=pl.BlockSpec((1, tq, 1024), lambda bi, i: (bi, i, 0)),
        compiler_params=_params(("arbitrary", "arbitrary")),
        name="odd_mixer",
    )(qkv, ckv, bias, du, bands, edges, pool_w, pool_scale)


OUTPROJ_SUB = 256


def _outproj_kernel(mix_ref, w_ref, x_ref, g1_ref, lng_ref, lnb_ref, sc_ref, sh_ref, r2_ref, rh_ref,
                    x1_ref, hm_ref, aff_ref, *, n_exp):
    tm = x_ref.shape[1]
    subs = [slice(r0, r0 + OUTPROJ_SUB) for r0 in range(0, tm, OUTPROJ_SUB)]
    ys = [_dot(mix_ref[0, rows, :], w_ref[...]) for rows in subs]
    for rows, y in zip(subs, ys):
        x1 = _layer_norm(DEEPNORM_ALPHA * x_ref[0, rows, :] + g1_ref[0] * y, lng_ref[...], lnb_ref[...])
        x1_ref[0, rows, :] = x1
        hm = x1 * (1.0 + sc_ref[0]) + sh_ref[0]
        hm_hi = hm.astype(BF16)
        hm_ref[0, rows, :] = hm_hi
        hm_lo = (hm - hm_hi.astype(F32)).astype(BF16)
        part = _dot(hm_hi, r2_ref[...])
        logits = part[:, :n_exp] + part[:, n_exp:] + _dot(hm_lo, rh_ref[...])
        ex = jnp.exp(logits - logits.max(axis=-1, keepdims=True))
        aff_ref[0, rows, :] = ex * (1.0 / ex.sum(axis=-1, keepdims=True))


def _outproj_call(mix, w, x, g1, lng, lnb, sc2, sh2, router, name):
    b, s, d = x.shape
    dm = mix.shape[2]
    e = router.shape[1]
    tm = min(4 * OUTPROJ_SUB, s)
    r_hi = router.astype(BF16)
    r_lo = (router - r_hi.astype(F32)).astype(BF16)
    r2 = jnp.concatenate([r_hi, r_lo], axis=1)
    row = pl.BlockSpec((1, d), lambda bi, i: (0, 0))
    tile = lambda n: pl.BlockSpec((1, tm, n), lambda bi, i: (bi, i, 0))
    return pl.pallas_call(
        functools.partial(_outproj_kernel, n_exp=e),
        out_shape=[jax.ShapeDtypeStruct((b, s, d), F32),
                   jax.ShapeDtypeStruct((b, s, d), BF16),
                   jax.ShapeDtypeStruct((b, s, e), F32)],
        grid=(b, s // tm),
        in_specs=[tile(dm),
                  pl.BlockSpec((dm, d), lambda bi, i: (0, 0)),
                  tile(d),
                  g1.spec(), row, row, sc2.spec(), sh2.spec(),
                  pl.BlockSpec((d, 2 * e), lambda bi, i: (0, 0)),
                  pl.BlockSpec((d, e), lambda bi, i: (0, 0))],
        out_specs=[tile(d), tile(d), tile(e)],
        compiler_params=_params(("arbitrary", "arbitrary")),
        name=name,
    )(mix, w, x, g1.arr, lng, lnb, sc2.arr, sh2.arr, r2, r_hi)


def _lane_cumsum(m):
    rows, s = m.shape
    r_i = lax.broadcasted_iota(jnp.int32, (LANES, LANES), 0)
    c_i = lax.broadcasted_iota(jnp.int32, (LANES, LANES), 1)
    tri = jnp.where(r_i <= c_i, 1.0, 0.0).astype(BF16)
    carry = jnp.zeros((rows, 1), F32)
    out = []
    for c in range(s // LANES):
        blk = m[:, c * LANES:(c + 1) * LANES]
        out.append(_dot(blk.astype(BF16), tri) + carry)
        carry = carry + blk.sum(axis=-1, keepdims=True)
    return jnp.concatenate(out, axis=-1)


ROUTE_TILE = 256


def _route_kernel(aff_ref, pos_ref, cnt_ref, *, cap):
    a = aff_ref[...]
    thr = jnp.zeros((a.shape[0], 1), jnp.int32)
    for bit in range(30, -1, -1):
        cand = thr | jnp.int32(1 << bit)
        cnt = jnp.where(a >= pltpu.bitcast(cand, F32), 1.0, 0.0).sum(axis=-1, keepdims=True)
        thr = jnp.where(cnt >= cap, cand, thr)
    gt = jnp.where(a >= pltpu.bitcast(thr + 1, F32), 1.0, 0.0)
    eq = jnp.where(a >= pltpu.bitcast(thr, F32), 1.0, 0.0) - gt
    need = cap - gt.sum(axis=-1, keepdims=True)
    sel = gt + eq * jnp.where(_lane_cumsum(eq) <= need, 1.0, 0.0)
    pos_ref[...] = jnp.where(sel > 0.5, _lane_cumsum(sel) - 1.0, -1.0)
    tok = lax.broadcasted_iota(jnp.int32, (a.shape[1], LANES), 0)
    tile = lax.broadcasted_iota(jnp.int32, (a.shape[1], LANES), 1)
    before = jnp.where(tok < tile * ROUTE_TILE, 1.0, 0.0).astype(BF16)
    cnt_ref[...] = _dot(sel.astype(BF16), before)


def _route_call(aff, cap, name):
    b, e, s = aff.shape
    n_tiles = s // min(ROUTE_TILE, s)
    full = pl.BlockSpec((b * e, s), lambda i: (0, 0))
    pos, cnt = pl.pallas_call(
        functools.partial(_route_kernel, cap=cap),
        out_shape=[jax.ShapeDtypeStruct((b * e, s), F32), jax.ShapeDtypeStruct((b * e, LANES), F32)],
        grid=(1,),
        in_specs=[full],
        out_specs=[full, pl.BlockSpec((b * e, LANES), lambda i: (0, 0))],
        compiler_params=_params(("arbitrary",)),
        name=name,
    )(aff.reshape(b * e, s))
    return pos.reshape(b, e, s), cnt[:, :n_tiles + 1].astype(jnp.int32).reshape(-1)


def _gather_kernel(cnt_ref, pos_ref, aff_ref, hm_ref, xg_ref, gs_ref, *, cap, n_exp, win, n_tiles):
    bi, i = pl.program_id(0), pl.program_id(1)

    @pl.when(i == 0)
    def _():
        xg_ref[...] = jnp.zeros(xg_ref.shape, xg_ref.dtype)
        gs_ref[...] = jnp.zeros(gs_ref.shape, gs_ref.dtype)

    starts, short = [], None
    for ei in range(n_exp):
        base = (bi * n_exp + ei) * (n_tiles + 1) + i
        st = jnp.minimum((cnt_ref[base] // SLOT_ALIGN) * SLOT_ALIGN, cap - win)
        starts.append(pl.multiple_of(st, SLOT_ALIGN))
        miss = cnt_ref[base + 1] > st + win
        short = miss if short is None else jnp.logical_or(short, miss)

    def scatter_rows(n_rows, first, stacked):
        row = lax.broadcasted_iota(jnp.int32, (n_rows, 1), 0).astype(F32)
        blocks = []
        for ei in range(n_exp):
            rel = pos_ref[0, ei:ei + 1, :] if first is None else pos_ref[0, ei:ei + 1, :] - first[ei].astype(F32)
            hit = rel == row
            blocks.append(jnp.where(hit, 1.0, 0.0).astype(BF16))
            gate = jnp.where(hit, aff_ref[0, ei:ei + 1, :], 0.0).sum(axis=-1, keepdims=True)
            rows = slice(None) if first is None else pl.ds(first[ei], n_rows)
            gs_ref[0, ei, rows, :] += gate
        if stacked:
            part = _dot(jnp.concatenate(blocks, axis=0), hm_ref[0])
        for ei in range(n_exp):
            rows = slice(None) if first is None else pl.ds(first[ei], n_rows)
            sub = part[ei * n_rows:(ei + 1) * n_rows] if stacked else _dot(blocks[ei], hm_ref[0])
            xg_ref[0, ei, rows, :] += sub.astype(xg_ref.dtype)

    @pl.when(jnp.logical_not(short))
    def _():
        scatter_rows(win, starts, True)

    @pl.when(short)
    def _():
        scatter_rows(cap, None, False)


def _gather_call(cnt, pos, aff, hm, cap, name):
    b, e, s = pos.shape
    d = hm.shape[2]
    tm = min(ROUTE_TILE, s)
    n_tiles = s // tm
    win = min(64, cap)
    tok = pl.BlockSpec((1, e, tm), lambda bi, i, c: (bi, 0, i))
    return pl.pallas_call(
        functools.partial(_gather_kernel, cap=cap, n_exp=e, win=win, n_tiles=n_tiles),
        out_shape=[jax.ShapeDtypeStruct((b, e, cap, d), BF16),
                   jax.ShapeDtypeStruct((b, e, cap, 1), F32)],
        grid_spec=pltpu.PrefetchScalarGridSpec(
            num_scalar_prefetch=1,
            grid=(b, n_tiles),
            in_specs=[tok, tok, pl.BlockSpec((1, tm, d), lambda bi, i, c: (bi, i, 0))],
            out_specs=[pl.BlockSpec((1, e, cap, d), lambda bi, i, c: (bi, 0, 0, 0)),
                       pl.BlockSpec((1, e, cap, 1), lambda bi, i, c: (bi, 0, 0, 0))]),
        compiler_params=_params(("arbitrary", "arbitrary")),
        name=name,
    )(cnt, pos, aff, hm)


def _ffn_kernel(*refs, n_grp, row_chunks, n_steps):
    xg_refs = refs[:n_grp]
    gs_refs = refs[n_grp:2 * n_grp]
    wg_ref, wu_ref, wd_ref = refs[2 * n_grp:2 * n_grp + 3]
    y_refs = refs[2 * n_grp + 3:3 * n_grp + 3]
    acc_refs = refs[3 * n_grp + 3:]
    f = pl.program_id(1)
    last = pl.num_programs(1) - 1
    wg = wg_ref[0].astype(BF16)

    def body(first, final):
        wu = wd = None
        for xg_ref, gs_ref, y_ref, acc_ref, nb in zip(xg_refs, gs_refs, y_refs, acc_refs, row_chunks):
            bt, _, cap, d = xg_ref.shape
            for b0 in range(0, bt, nb):
                rows = nb * cap
                r0 = b0 * cap
                x = xg_ref[b0:b0 + nb, 0].reshape(rows, d)
                hg = _dot(x, wg)
                if wu is None:
                    wu = wu_ref[0].astype(BF16)
                hu = _dot(x, wu)
                if wd is None:
                    wd = wd_ref[0].astype(BF16)
                hid = (hg * (1.0 / (1.0 + jnp.exp(-hg))) * hu).astype(BF16)
                part = _dot(hid, wd)
                if not first:
                    part = acc_ref[r0:r0 + rows, :] + part
                if final:
                    gate = gs_ref[b0:b0 + nb, 0].reshape(rows, 1)
                    y_ref[b0:b0 + nb, 0] = (part * gate).reshape(nb, cap, d).astype(y_ref.dtype)
                else:
                    acc_ref[r0:r0 + rows, :] = part

    if n_steps == 1:
        body(True, True)
    else:
        pl.when(f == 0)(functools.partial(body, True, False))
        pl.when(jnp.logical_and(f > 0, f < last))(functools.partial(body, False, False))
        pl.when(f == last)(functools.partial(body, False, True))


def _ffn_call(xgs, gss, w_gate, w_up, w_down, layer, name):
    _, e, d, ff = w_gate.shape
    tf = 512 if ff % 512 == 0 else ff
    n_grp = len(xgs)
    row_chunks = []
    for xg in xgs:
        bt, _, cap, _ = xg.shape
        nb = max(1, min(bt, 512 // cap))
        while bt % nb:
            nb -= 1
        row_chunks.append(nb)
    tok = lambda a: pl.BlockSpec((a.shape[0], 1) + a.shape[2:], lambda ei, fi: (0, ei, 0, 0))
    kern = functools.partial(_ffn_kernel, n_grp=n_grp, row_chunks=tuple(row_chunks), n_steps=ff // tf)
    return pl.pallas_call(
        kern,
        out_shape=[jax.ShapeDtypeStruct(xg.shape, BF16) for xg in xgs],
        grid=(e, ff // tf),
        in_specs=[tok(a) for a in xgs] + [tok(a) for a in gss]
                 + [pl.BlockSpec((None, 1, d, tf), lambda ei, fi: (layer, ei, 0, fi)),
                    pl.BlockSpec((None, 1, d, tf), lambda ei, fi: (layer, ei, 0, fi)),
                    pl.BlockSpec((None, 1, tf, d), lambda ei, fi: (layer, ei, fi, 0))],
        out_specs=[tok(a) for a in xgs],
        scratch_shapes=[pltpu.VMEM((xg.shape[0] * xg.shape[2], d), F32) for xg in xgs],
        compiler_params=_params(("arbitrary", "arbitrary")),
        name=name,
    )(*xgs, *gss, w_gate, w_up, w_down)


MXU_DEPTH = 256
SLOT_ALIGN = 16


def _combine_kernel(cnt_ref, pos_ref, y_ref, x_ref, g_ref, lng_ref, lnb_ref, *rest, cap, n_exp, win, n_tiles,
                    n_sub, proj_cfg=None):
    if proj_cfg is None:
        (o_ref,), proj_refs = rest, ()
    else:
        proj_refs, o_ref = rest[:3] + rest[4:], rest[3]
    bi, i = pl.program_id(0), pl.program_id(1)
    sub = pos_ref.shape[2] // n_sub
    grp = MXU_DEPTH // win
    starts, short = [], None
    for r in range(n_sub):
        starts.append([])
        for ei in range(n_exp):
            base = (bi * n_exp + ei) * (n_tiles + 1) + i * n_sub + r
            st = jnp.minimum((cnt_ref[base] // SLOT_ALIGN) * SLOT_ALIGN, cap - win)
            starts[r].append(st)
            miss = cnt_ref[base + 1] > st + win
            short = miss if short is None else jnp.logical_or(short, miss)

    _TN = (((0,), (0,)), ((), ()))

    def windowed(r):
        cols = slice(r * sub, (r + 1) * sub)
        row = lax.broadcasted_iota(jnp.int32, (win, 1), 0).astype(F32)
        acc = None
        for k in range(n_exp // grp):
            hits, rows = [], []
            for u in range(grp):
                ei = k * grp + u
                st = starts[r][ei]
                hits.append(jnp.where(pos_ref[0, ei:ei + 1, cols] - st.astype(F32) == row, 1.0, 0.0).astype(BF16))
                rows.append(y_ref[0, pl.ds(pl.multiple_of(ei * cap + st, SLOT_ALIGN), win), :])
            part = _dot(jnp.concatenate(hits, axis=0), jnp.concatenate(rows, axis=0), _TN)
            acc = part if acc is None else acc + part
        return acc

    def dense(r):
        cols = slice(r * sub, (r + 1) * sub)
        slot = lax.broadcasted_iota(jnp.int32, (cap, 1), 0).astype(F32)
        acc = None
        for ei in range(n_exp):
            hit = jnp.where(pos_ref[0, ei:ei + 1, cols] == slot, 1.0, 0.0).astype(BF16)
            part = _dot(hit, y_ref[0, ei * cap:(ei + 1) * cap, :], _TN)
            acc = part if acc is None else acc + part
        return acc


    def epilogue(r, acc):
        rows = slice(r * sub, (r + 1) * sub)
        x_new = _layer_norm(DEEPNORM_ALPHA * x_ref[0, rows, :] + g_ref[0] * acc, lng_ref[...], lnb_ref[...])
        o_ref[0, rows, :] = x_new
        if proj_cfg is not None:
            sc_ref, sh_ref, w_ref = proj_refs[:3]
            h = (x_new * (1.0 + sc_ref[0]) + sh_ref[0]).astype(BF16)
            _inproj_body(h, w_ref, None, None, None, proj_refs[3:], rows=rows, **proj_cfg)

    @pl.when(jnp.logical_not(short))
    def _():
        accs = [windowed(r) for r in range(n_sub)]
        for r in range(n_sub):
            epilogue(r, accs[r])

    @pl.when(short)
    def _():
        for r in range(n_sub):
            epilogue(r, dense(r))


def _combine_call(cnt, pos_t, y, x, g2, lng, lnb, cap, name, proj=None):
    b, s, d = x.shape
    e = pos_t.shape[1]
    n_tiles = s // min(ROUTE_TILE, s)
    n_sub = 2 if n_tiles % 2 == 0 else 1
    tm = s // (n_tiles // n_sub)
    win = min(64, cap)
    row = pl.BlockSpec((1, d), lambda bi, i, c: (0, 0))
    tile = lambda n: pl.BlockSpec((1, tm, n), lambda bi, i, c: (bi, i, 0))
    in_specs = [pl.BlockSpec((1, e, tm), lambda bi, i, c: (bi, 0, i)),
                pl.BlockSpec((1, e * cap, d), lambda bi, i, c: (bi, 0, 0)), tile(d), g2.spec(), row, row]
    operands = [cnt, pos_t, y.reshape(b, e * cap, d), x, g2.arr, lng, lnb]
    out_shape = [jax.ShapeDtypeStruct((b, s, d), F32)]
    out_specs = [tile(d)]
    proj_cfg = None
    if proj is not None:
        sc, sh, w, blocks, out_defs, ones_cols = proj
        in_specs += [sc.spec(), sh.spec(), pl.BlockSpec(w.shape, lambda bi, i, c: (0, 0))]
        operands += [sc.arr, sh.arr, w]
        out_shape += [jax.ShapeDtypeStruct((b, s, nc), dt) for nc, dt in out_defs]
        out_specs += [tile(nc) for nc, _ in out_defs]
        proj_cfg = dict(blocks=tuple(blocks), ones_cols=tuple(ones_cols), chunk=512, use_rope=False)
    res = pl.pallas_call(
        functools.partial(_combine_kernel, cap=cap, n_exp=e, win=win, n_tiles=n_tiles, n_sub=n_sub,
                          proj_cfg=proj_cfg),
        out_shape=out_shape,
        grid_spec=pltpu.PrefetchScalarGridSpec(
            num_scalar_prefetch=1, grid=(b, n_tiles // n_sub), in_specs=in_specs, out_specs=out_specs),
        compiler_params=_params(("arbitrary", "arbitrary")),
        name=name,
    )(*operands)
    return res[0] if proj is None else res


def _ffn_sublayer(streams, projs, lng, lnb, router, w_gate, w_up, w_down, layer):
    staged = []
    for si, (mix, w_out, x, g1, sc2, sh2, g2) in enumerate(streams):
        tag = f"l{layer}s{si}"
        n_tok = x.shape[1]
        cap = EC_FACTOR * n_tok // N_EXPERTS
        x1, hm, aff_t = _outproj_call(mix, w_out, x, g1, lng[0:1], lnb[0:1], sc2, sh2, router, "outproj_" + tag)
        aff = jnp.swapaxes(aff_t, 1, 2)
        pos, cnt = _route_call(aff, cap, "route_" + tag)
        xg, gs = _gather_call(cnt, pos, aff, hm, cap, "gather_" + tag)
        staged.append((x1, pos, cnt, xg, gs, g2, cap, tag))
    ys = _ffn_call([st[3] for st in staged], [st[4] for st in staged], w_gate, w_up, w_down, layer,
                   f"ffn_l{layer}")
    outs = []
    for (x1, pos, cnt, _, _, g2, cap, tag), y, proj in zip(staged, ys, projs):
        outs.append(_combine_call(cnt, pos, y, x1, g2, lng[1:2], lnb[1:2], cap,
                                  "combine_" + tag, proj))
    return outs


def kernel(x, c, ctx, c_ctx, ada_w, ada_b, ln_g, ln_b, l0_w_in, l0_w_out, l0_lam_q1, l0_lam_k1, l0_lam_q2,
           l0_lam_k2, l0_subln_g, l0_qnorm_g, l0_knorm_g, l1_w_in, l1_w_out, l1_rpb, l1_pool_w, l1_pool_scale,
           moe_router, moe_w_gate, moe_w_up, moe_w_down):
    b, s, d = x.shape
    n_ctx = ctx.shape[1]

    rows = -(-(b + 1) // 8) * 8
    cc = jnp.concatenate([c, c_ctx[None], jnp.zeros((rows - b - 1, d), F32)], axis=0)
    mod = _ada_call(cc, ada_w, ada_b)

    mod5 = mod.reshape(DEPTH, rows, 6, 1, d)

    def mods(i):
        return ([_Mod(mod5, i, k, None) for k in range(6)], [_Mod(mod5, i, k, b) for k in range(6)])

    rope = _rope_tables(s)
    rope_id = (jnp.ones((n_ctx, LANES), F32), jnp.zeros((n_ctx, LANES), F32), jnp.zeros((n_ctx, LANES), F32))
    gmat = jnp.asarray(np.where((np.arange(LANES)[:, None] // HEAD_DIM) == (np.arange(LANES)[None, :] // HEAD_DIM),
                                1.0 / HEAD_DIM, 0.0), BF16)
    tile2 = lambda g: jnp.concatenate([g, g]).reshape(1, LANES)

    (sh1, sc1, g1, sh2, sc2, g2), (csh1, csc1, cg1, csh2, csc2, cg2) = mods(0)
    bq0 = 3 * 512
    pair_heads = [hh for j in range(B_HEADS // 2) for hh in (j, j + B_HEADS // 2)]
    w_in_b, w_out_b = l0_w_in.astype(BF16), l0_w_out.astype(BF16)
    w_in0 = jnp.concatenate([w_in_b[:, :bq0]]
                            + [w_in_b[:, bq0 + hh * HEAD_DIM:bq0 + (hh + 1) * HEAD_DIM] for hh in pair_heads]
                            + [w_in_b[:, bq0 + 512:]], axis=1)
    w_out0 = jnp.concatenate([w_out_b[:512]]
                             + [w_out_b[512 + hh * HEAD_DIM:512 + (hh + 1) * HEAD_DIM] for hh in pair_heads], axis=0)
    gains0 = jnp.concatenate([tile2(l0_qnorm_g), tile2(l0_knorm_g)], axis=0)
    q_exp2 = Q_SCALE * math.log2(math.e)
    blocks0 = ([("rope", 0, q_exp2, 0, EV_AQ + k * LANES) for k in range(4)]
               + [("rope", 0, 1.0, 0, EV_AK + k * LANES) for k in range(4)]
               + [("plain", 0, 1.0, 0, EV_AV + k * 2 * LANES) for k in range(4)]
               + [("norm", 0, q_exp2, 0, EV_BQ + k * LANES) for k in range(4)]
               + [("norm", 1, 1.0, 0, EV_BK), ("plain", 0, 1.0, 0, EV_BV)])
    ones0 = [(0, EV_AV + (2 * k + 1) * LANES) for k in range(4)] + [(0, EV_BV + LANES)]
    qkv = _inproj_call(x, sc1, sh1, w_in0, blocks0, [(EV_WIDTH, BF16)], rope, gains0, gmat,
                       use_rope=True, name="inproj_l0", ones_cols=ones0)[0]
    qkv_c = _inproj_call(ctx, csc1, csh1, w_in0, blocks0, [(EV_WIDTH, BF16)], rope_id, gains0, gmat,
                         use_rope=False, name="inproj_l0c", ones_cols=ones0)[0]
    lam_init = 0.8 - 0.6 * math.exp(-0.3 * 0)
    lamv = jnp.stack([l0_lam_q1, l0_lam_k1, l0_lam_q2, l0_lam_k2], axis=0)
    sub_g = l0_subln_g.reshape(1, LANES)
    mix = _attn_even_call(qkv, [qkv, qkv_c], lamv, sub_g, lam_init, "attn_l0")
    mix_c = _attn_even_call(qkv_c, [qkv_c], lamv, sub_g, lam_init, "attn_l0c")
    (sh1n, sc1n, g1n, sh2n, sc2n, g2n), (csh1n, csc1n, _, _, _, _) = mods(1)
    cw = C_HEADS * HEAD_DIM
    w_in1 = l1_w_in.astype(BF16)
    blocks1 = ([("plain", 0, q_exp2, 0, k * LANES) for k in range(4)]
               + [("plain", 0, 1.0, 0, cw + k * LANES) for k in range(4)]
               + [("plain", 0, 1.0, 0, 2 * cw + 2 * k * LANES) for k in range(4)]
               + [("plain", 0, 1.0, 1, k * LANES) for k in range(4)])
    ones1 = [(0, 2 * cw + (2 * k + 1) * LANES) for k in range(4)]
    blocks1c = ([("plain", 0, 1.0, 0, k * LANES) for k in range(4)]
                + [("plain", 0, 1.0, 0, cw + 2 * k * LANES) for k in range(4)])
    ones1c = [(0, cw + (2 * k + 1) * LANES) for k in range(4)]
    proj_lat = (sc1n, sh1n, w_in1, blocks1, [(4 * cw, BF16), (D_GROUPS * D_GROUP_DIM, F32)], ones1)
    proj_ctx = (csc1n, csh1n, w_in1[:, cw:3 * cw], blocks1c, [(3 * cw, BF16)], ones1c)
    (x, qkv1, du), (ctx, ckv) = _ffn_sublayer(
        [(mix, w_out0, x, g1, sc2, sh2, g2), (mix_c, w_out0, ctx, cg1, csc2, csh2, cg2)],
        [proj_lat, proj_ctx], ln_g[0], ln_b[0], moe_router[0], moe_w_gate, moe_w_up, moe_w_down, 0)

    mix1 = _odd_mixer_call(qkv1, ckv, l1_rpb * math.log2(math.e), du, l1_pool_w.astype(BF16),
                           l1_pool_scale.reshape(1, -1))
    (x,) = _ffn_sublayer([(mix1, l1_w_out.astype(BF16), x, g1n, sc2n, sh2n, g2n)],
                         [None], ln_g[1], ln_b[1], moe_router[1], moe_w_gate, moe_w_up, moe_w_down, 1)
    return x
```
